```python
import math
import jax, jax.numpy as jnp
from jax import lax
import numpy as np

D_MODEL = 2048
BATCH = 2
SEQ = 8192
DEPTH = 4

GRID_W = 64
CTX_LEN = 256
N_MIXERS = 4
GROUP_WIDTH = D_MODEL // N_MIXERS
MIX_WIDTH = N_MIXERS * GROUP_WIDTH
HEAD_DIM = 128
GROUP_HEADS = GROUP_WIDTH // HEAD_DIM
GDN_CONV = 3
GDN_CHUNK = 64
DIFF_SUB = HEAD_DIM // 2
MLA_Q_RANK = 384
MLA_KV_RANK = 128
MLA_NOPE = 128
MLA_ROPE = 64
NA_KR = 8
NA_KC = 16
D_FF = 5632
FFN_CONV = 3
ROPE_THETA = 10000.0
Q_BLOCK = 128
LN_EPS = 1e-5
RMS_EPS = 1e-6
DEEPNORM_ALPHA = (2 * DEPTH) ** 0.25
DEEPNORM_BETA = (8 * DEPTH) ** -0.25
GDN_COLS = 4 * GROUP_WIDTH + 4 * GROUP_HEADS
DIFF_COLS = 3 * GROUP_WIDTH
MLA_COLS = MLA_Q_RANK + MLA_KV_RANK + MLA_ROPE
NA_COLS = 3 * GROUP_WIDTH
N_IN = GDN_COLS + DIFF_COLS + MLA_COLS + NA_COLS

kernel_name = 'hybrid_parallel_heads_dit_trunk'


def _layernorm(x, g, b):
    xf = x.astype(jnp.float32)
    mu = jnp.mean(xf, -1, keepdims=True)
    var = jnp.mean(jnp.square(xf - mu), -1, keepdims=True)
    return ((xf - mu) * lax.rsqrt(var + LN_EPS) * g + b).astype(x.dtype)


def _rmsnorm(x, g):
    xf = x.astype(jnp.float32)
    return (xf * lax.rsqrt(jnp.mean(jnp.square(xf), -1, keepdims=True) + RMS_EPS) * g).astype(x.dtype)


def _l2norm(x):
    xf = x.astype(jnp.float32)
    return xf * lax.rsqrt(jnp.sum(jnp.square(xf), -1, keepdims=True) + RMS_EPS)


def _softmax_f32(s):
    return jax.nn.softmax(s.astype(jnp.float32), axis=-1)


def _dwconv(x, w):
    k = w.shape[0]
    pad = k // 2
    n = x.shape[1]
    xp = jnp.pad(x, ((0, 0), (pad, pad), (0, 0)))
    return sum(xp[:, i:i + n] * w[i] for i in range(k))


def _rope_1d(x, pos):
    d = x.shape[-1]
    inv = ROPE_THETA ** (-jnp.arange(0, d, 2, dtype=jnp.float32) / d)
    ang = pos[:, None] * inv[None, :]
    cos = jnp.cos(ang).astype(x.dtype)
    sin = jnp.sin(ang).astype(x.dtype)
    x1, x2 = jnp.split(x, 2, axis=-1)
    return jnp.concatenate([x1 * cos - x2 * sin, x1 * sin + x2 * cos], axis=-1)


def _rope_2d(x, row, col):
    xa, xb = jnp.split(x, 2, axis=-1)
    return jnp.concatenate([_rope_1d(xa, row), _rope_1d(xb, col)], axis=-1)


def _split_heads(t):
    b, n, _ = t.shape
    return t.reshape(b, n, GROUP_HEADS, -1).transpose(0, 2, 1, 3)


def _merge_heads(o):
    b, h, n, d = o.shape
    return o.transpose(0, 2, 1, 3).reshape(b, n, h * d)


def _sweep_query_blocks(fn, *qs):
    b, h, n = qs[0].shape[:3]
    nb = n // Q_BLOCK
    blocks = tuple(jnp.moveaxis(q.reshape(b, h, nb, Q_BLOCK, q.shape[-1]), 2, 0) for q in qs)
    out = lax.map(lambda qb: fn(*qb), blocks)
    return jnp.moveaxis(out, 0, 2).reshape(b, h, n, out.shape[-1])


def _dense_attend(q, k, v, scale):
    p = _softmax_f32(jnp.einsum('bhqd,bhkd->bhqk', q, k) * scale).astype(v.dtype)
    return jnp.einsum('bhqk,bhkd->bhqd', p, v)


def _gated_delta_chunked(q, k, v, log_a, beta, s0):
    b, h, n, dk = q.shape
    dv = v.shape[-1]
    nc = n // GDN_CHUNK
    cs = GDN_CHUNK
    ch = lambda t: t.reshape(b, h, nc, cs, *t.shape[3:])
    q, k, v, log_a, beta = ch(q), ch(k), ch(v), ch(log_a), ch(beta)
    g = jnp.cumsum(log_a, axis=-1)
    incl = jnp.tril(jnp.ones((cs, cs), bool))
    strict = jnp.tril(jnp.ones((cs, cs), bool), -1)
    decay = jnp.exp(jnp.where(incl, g[..., :, None] - g[..., None, :], -jnp.inf))
    kb = k * beta[..., None]
    a_mat = jnp.where(strict, jnp.einsum('bhnid,bhnjd->bhnij', kb, k) * decay, 0.0)
    eye = jnp.broadcast_to(jnp.eye(cs, dtype=q.dtype), a_mat.shape)
    t_inv = lax.linalg.triangular_solve(eye + a_mat, eye, left_side=True, lower=True, unit_diagonal=True)
    w = jnp.einsum('bhnij,bhnjd->bhnid', t_inv, kb * jnp.exp(g)[..., None])
    u = jnp.einsum('bhnij,bhnjd->bhnid', t_inv, v * beta[..., None])
    qk = jnp.where(incl, jnp.einsum('bhnid,bhnjd->bhnij', q, k) * decay, 0.0)

    def step(s, xs):
        q_i, k_i, w_i, u_i, g_i, qk_i = xs
        v_new = u_i - jnp.einsum('bhcd,bhdv->bhcv', w_i, s)
        o = (jnp.einsum('bhcd,bhdv->bhcv', q_i * jnp.exp(g_i)[..., None], s)
             + jnp.einsum('bhij,bhjv->bhiv', qk_i, v_new))
        g_last = g_i[..., -1]
        s = (s * jnp.exp(g_last)[..., None, None]
             + jnp.einsum('bhcd,bhcv->bhdv', k_i * jnp.exp(g_last[..., None] - g_i)[..., None], v_new))
        return s, o

    xs = tuple(jnp.moveaxis(t, 2, 0) for t in (q, k, w, u, g, qk))
    s_fin, o = lax.scan(step, s0, xs)
    return jnp.moveaxis(o, 0, 2).reshape(b, h, n, dv), s_fin


def _gdn_mixer(p, pc, conv_w, a_log, dt_bias, norm_g, with_ctx_out):
    wd = GROUP_WIDTH

    def prep(t):
        b, n, _ = t.shape
        q, k, v = jnp.split(jax.nn.silu(_dwconv(t[..., :3 * wd], conv_w)).astype(jnp.float32), 3, axis=-1)
        q = _l2norm(_split_heads(q)) * HEAD_DIM ** -0.5
        k = _l2norm(_split_heads(k))
        v = _split_heads(v)
        ab = t[..., 4 * wd:].astype(jnp.float32).reshape(b, n, 2, 2, GROUP_HEADS)
        log_a = -jnp.exp(a_log.astype(jnp.float32)) * jax.nn.softplus(ab[:, :, 0] + dt_bias.astype(jnp.float32))
        beta = jax.nn.sigmoid(ab[:, :, 1])
        return q, k, v, t[..., 3 * wd:4 * wd], log_a.transpose(2, 0, 3, 1), beta.transpose(2, 0, 3, 1)

    ql, kl, vl, zl, lal, btl = prep(p)
    qc, kc, vc, zc, lac, btc = prep(pc)
    b = p.shape[0]
    out_l = 0.0
    out_c = 0.0
    for d in range(2):
        f = (lambda t: jnp.flip(t, axis=2)) if d == 1 else (lambda t: t)
        s0 = jnp.zeros((b, GROUP_HEADS, HEAD_DIM, HEAD_DIM), jnp.float32)
        oc, s_ctx = _gated_delta_chunked(f(qc), f(kc), f(vc), f(lac[d]), f(btc[d]), s0)
        ol, _ = _gated_delta_chunked(f(ql), f(kl), f(vl), f(lal[d]), f(btl[d]), s_ctx)
        out_l = out_l + f(ol)
        out_c = out_c + f(oc)

    def finish(o, z):
        o = _rmsnorm(o.transpose(0, 2, 1, 3), norm_g)
        b_, n_ = o.shape[:2]
        return (o.reshape(b_, n_, GROUP_WIDTH) * jax.nn.silu(z.astype(jnp.float32))).astype(z.dtype)

    return finish(out_l, zl), (finish(out_c, zc) if with_ctx_out else None)


def _diff_mixer(p, pc, lam_qk, norm_g, layer_idx, row, col, with_ctx_out):
    wd = GROUP_WIDTH
    lam_init = 0.8 - 0.6 * math.exp(-0.3 * layer_idx)
    lf = lam_qk.astype(jnp.float32)
    lam = jnp.exp(jnp.sum(lf[0] * lf[1])) - jnp.exp(jnp.sum(lf[2] * lf[3])) + lam_init
    scale = DIFF_SUB ** -0.5

    def heads(t, rope):
        b, n, _ = t.shape
        q = t[..., :wd].reshape(b, n, GROUP_HEADS, 2, DIFF_SUB).transpose(3, 0, 2, 1, 4)
        k = t[..., wd:2 * wd].reshape(b, n, GROUP_HEADS, 2, DIFF_SUB).transpose(3, 0, 2, 1, 4)
        v = _split_heads(t[..., 2 * wd:3 * wd])
        if rope:
            q = _rope_2d(q, row, col)
            k = _rope_2d(k, row, col)
        return q, k, v

    ql, kl, vl = heads(p, True)
    qc, kc, vc = heads(pc, False)

    def attend(k1, k2, vv):
        def fn(q1b, q2b):
            p1 = _softmax_f32(jnp.einsum('bhqd,bhkd->bhqk', q1b, k1) * scale)
            p2 = _softmax_f32(jnp.einsum('bhqd,bhkd->bhqk', q2b, k2) * scale)
            return jnp.einsum('bhqk,bhkd->bhqd', (p1 - lam * p2).astype(vv.dtype), vv)
        return fn

    k1 = jnp.concatenate([kc[0], kl[0]], axis=2)
    k2 = jnp.concatenate([kc[1], kl[1]], axis=2)
    v_all = jnp.concatenate([vc, vl], axis=2)
    o_l = _sweep_query_blocks(attend(k1, k2, v_all), ql[0], ql[1])

    def finish(o):
        o = _rmsnorm(o.transpose(0, 2, 1, 3), norm_g) * (1.0 - lam_init)
        b_, n_ = o.shape[:2]
        return o.reshape(b_, n_, GROUP_WIDTH)

    y_c = finish(attend(kc[0], kc[1], vc)(qc[0], qc[1])) if with_ctx_out else None
    return finish(o_l), y_c


def _mla_mixer(p, pc, q_norm_g, kv_norm_g, w_uq, w_ukv, row, col, with_ctx_out):
    scale = (MLA_NOPE + MLA_ROPE) ** -0.5

    def heads(t, rope):
        b, n, _ = t.shape
        c_q = _rmsnorm(t[..., :MLA_Q_RANK], q_norm_g)
        c_kv = _rmsnorm(t[..., MLA_Q_RANK:MLA_Q_RANK + MLA_KV_RANK], kv_norm_g)
        k_r = t[..., MLA_Q_RANK + MLA_KV_RANK:][:, None]
        q = _split_heads(c_q @ w_uq)
        kv = _split_heads(c_kv @ w_ukv)
        q_n, q_r = q[..., :MLA_NOPE], q[..., MLA_NOPE:]
        k_n, v = kv[..., :MLA_NOPE], kv[..., MLA_NOPE:]
        if rope:
            q_r = _rope_2d(q_r, row, col)
            k_r = _rope_2d(k_r, row, col)
        q = jnp.concatenate([q_n, q_r], axis=-1)
        k = jnp.concatenate([k_n, jnp.broadcast_to(k_r, (b, GROUP_HEADS, n, MLA_ROPE))], axis=-1)
        return q, k, v

    ql, kl, vl = heads(p, True)
    qc, kc, vc = heads(pc, False)
    k_all = jnp.concatenate([kc, kl], axis=2)
    v_all = jnp.concatenate([vc, vl], axis=2)
    o_l = _sweep_query_blocks(lambda qb: _dense_attend(qb, k_all, v_all, scale), ql)
    y_c = _merge_heads(_dense_attend(qc, kc, vc, scale)) if with_ctx_out else None
    return _merge_heads(o_l), y_c


def _na_mixer(p, pc, rpb, with_ctx_out):
    wd = GROUP_WIDTH
    scale = HEAD_DIM ** -0.5
    ql, kl, vl = [_split_heads(p[..., i * wd:(i + 1) * wd]) for i in range(3)]
    qc, kc, vc = [_split_heads(pc[..., i * wd:(i + 1) * wd]) for i in range(3)]
    b, h, n, d = ql.shape
    rows = n // GRID_W
    kr = min(NA_KR, rows)
    grid = lambda t: t.reshape(b, h, rows, GRID_W, d)
    qg, kg, vg = grid(ql), grid(kl), grid(vl)
    r = jnp.arange(rows)
    row_idx = jnp.clip(r - kr // 2, 0, rows - kr)[:, None] + jnp.arange(kr)[None, :]
    k_nb = jnp.take(kg, row_idx, axis=2)
    v_nb = jnp.take(vg, row_idx, axis=2)
    cq = jnp.arange(GRID_W)
    col_start = jnp.clip(cq - NA_KC // 2, 0, GRID_W - NA_KC)
    col_ok = (cq[None, :] >= col_start[:, None]) & (cq[None, :] < col_start[:, None] + NA_KC)
    dy = row_idx - r[:, None] + (NA_KR - 1)
    dx = jnp.clip(cq[None, :] - cq[:, None] + (NA_KC - 1), 0, 2 * NA_KC - 2)
    bias = rpb[:, dy[:, None, :, None], dx[None, :, None, :]]
    s_nb = jnp.einsum('bhrqd,bhrikd->bhrqik', qg, k_nb).astype(jnp.float32) * scale + bias
    s_nb = jnp.where(col_ok[:, None, :], s_nb, -jnp.inf)
    s_ctx = jnp.einsum('bhrqd,bhcd->bhrqc', qg, kc).astype(jnp.float32) * scale
    n_nb = kr * GRID_W
    probs = jax.nn.softmax(jnp.concatenate([s_nb.reshape(b, h, rows, GRID_W, n_nb), s_ctx], axis=-1), axis=-1)
    probs = probs.astype(vl.dtype)
    o = (jnp.einsum('bhrqik,bhrikd->bhrqd', probs[..., :n_nb].reshape(b, h, rows, GRID_W, kr, GRID_W), v_nb)
         + jnp.einsum('bhrqc,bhcd->bhrqd', probs[..., n_nb:], vc))
    y_l = _merge_heads(o.reshape(b, h, n, d))
    y_c = _merge_heads(_dense_attend(qc, kc, vc, scale)) if with_ctx_out else None
    return y_l, y_c


def _conv_ffn(h, w_up, conv_w, w_down):
    gate, val = jnp.split(_dwconv(h @ w_up, conv_w), 2, axis=-1)
    return (jax.nn.silu(gate) * val) @ w_down


def setup_inputs(seed: int = 0) -> dict:
    key = jax.random.key(seed)
    ks = iter(jax.random.split(key, 40))
    f32 = jnp.float32
    nrm = lambda shape, s: jax.random.normal(next(ks), shape, f32) * s
    dt = jnp.exp(jax.random.uniform(next(ks), (DEPTH, 2, GROUP_HEADS), f32,
                                    minval=math.log(1e-3), maxval=math.log(1e-1)))
    return {
        'x': nrm((BATCH, SEQ, D_MODEL), 1.0),
        'c': nrm((BATCH, D_MODEL), 1.0),
        'ctx': nrm((BATCH, CTX_LEN, D_MODEL), 1.0),
        'c_ctx': nrm((D_MODEL,), 1.0),
        'w_mod': nrm((DEPTH, D_MODEL, 6 * D_MODEL), 0.5 * D_MODEL ** -0.5),
        'b_mod': nrm((DEPTH, 6 * D_MODEL), 0.02),
        'w_in': nrm((DEPTH, D_MODEL, N_IN), D_MODEL ** -0.5),
        'gdn_conv': nrm((DEPTH, GDN_CONV, 3 * GROUP_WIDTH), GDN_CONV ** -0.5),
        'gdn_a_log': jnp.log(jax.random.uniform(next(ks), (DEPTH, 2, GROUP_HEADS), f32, minval=1.0, maxval=16.0)),
        'gdn_dt_bias': dt + jnp.log(-jnp.expm1(-dt)),
        'gdn_norm_g': 1.0 + nrm((DEPTH, HEAD_DIM), 0.02),
        'diff_lambda': nrm((DEPTH, 4, DIFF_SUB), 0.1),
        'diff_norm_g': 1.0 + nrm((DEPTH, HEAD_DIM), 0.02),
        'mla_q_norm_g': 1.0 + nrm((DEPTH, MLA_Q_RANK), 0.02),
        'mla_kv_norm_g': 1.0 + nrm((DEPTH, MLA_KV_RANK), 0.02),
        'mla_w_uq': nrm((DEPTH, MLA_Q_RANK, GROUP_HEADS * (MLA_NOPE + MLA_ROPE)), MLA_Q_RANK ** -0.5),
        'mla_w_ukv': nrm((DEPTH, MLA_KV_RANK, GROUP_HEADS * (MLA_NOPE + HEAD_DIM)), MLA_KV_RANK ** -0.5),
        'na_rpb': nrm((DEPTH, GROUP_HEADS, 2 * NA_KR - 1, 2 * NA_KC - 1), 0.1),
        'w_out': nrm((DEPTH, MIX_WIDTH, D_MODEL), DEEPNORM_BETA * MIX_WIDTH ** -0.5),
        'ln_g': 1.0 + nrm((DEPTH, 2, D_MODEL), 0.02),
        'ln_b': nrm((DEPTH, 2, D_MODEL), 0.02),
        'ffn_w_up': nrm((DEPTH, D_MODEL, 2 * D_FF), D_MODEL ** -0.5),
        'ffn_conv': nrm((DEPTH, FFN_CONV, 2 * D_FF), FFN_CONV ** -0.5),
        'ffn_w_down': nrm((DEPTH, D_FF, D_MODEL), DEEPNORM_BETA * D_FF ** -0.5),
    }


def reference(x, c, ctx, c_ctx, w_mod, b_mod, w_in, gdn_conv, gdn_a_log, gdn_dt_bias, gdn_norm_g,
              diff_lambda, diff_norm_g, mla_q_norm_g, mla_kv_norm_g, mla_w_uq, mla_w_ukv,
              na_rpb, w_out, ln_g, ln_b, ffn_w_up, ffn_conv, ffn_w_down):
    n_lat = x.shape[1]
    pos = jnp.arange(n_lat)
    row = (pos // GRID_W).astype(jnp.float32)
    col = (pos % GRID_W).astype(jnp.float32)
    s_c = jax.nn.silu(c)
    s_cc = jax.nn.silu(c_ctx)
    o_gdn = GDN_COLS
    o_diff = GDN_COLS + DIFF_COLS
    o_mla = GDN_COLS + DIFF_COLS + MLA_COLS
    xc = ctx
    for l in range(DEPTH):
        ctx_out = l < DEPTH - 1
        m = [t[:, None] for t in jnp.split(s_c @ w_mod[l] + b_mod[l], 6, axis=-1)]
        mc = jnp.split(s_cc @ w_mod[l] + b_mod[l], 6, axis=-1)
        h = x * (1 + m[1]) + m[0]
        hc = xc * (1 + mc[1]) + mc[0]
        p = h @ w_in[l]
        pc = hc @ w_in[l]
        ya, yca = _gdn_mixer(p[..., :o_gdn], pc[..., :o_gdn], gdn_conv[l], gdn_a_log[l], gdn_dt_bias[l],
                             gdn_norm_g[l], ctx_out)
        yb, ycb = _diff_mixer(p[..., o_gdn:o_diff], pc[..., o_gdn:o_diff], diff_lambda[l], diff_norm_g[l],
                              l, row, col, ctx_out)
        ym, ycm = _mla_mixer(p[..., o_diff:o_mla], pc[..., o_diff:o_mla], mla_q_norm_g[l], mla_kv_norm_g[l],
                             mla_w_uq[l], mla_w_ukv[l], row, col, ctx_out)
        yn, ycn = _na_mixer(p[..., o_mla:], pc[..., o_mla:], na_rpb[l], ctx_out)
        y = jnp.concatenate([ya, yb, ym, yn], axis=-1) @ w_out[l]
        x = _layernorm(DEEPNORM_ALPHA * x + m[2] * y, ln_g[l, 0], ln_b[l, 0])
        h = x * (1 + m[4]) + m[3]
        x = _layernorm(DEEPNORM_ALPHA * x + m[5] * _conv_ffn(h, ffn_w_up[l], ffn_conv[l], ffn_w_down[l]),
                       ln_g[l, 1], ln_b[l, 1])
        if ctx_out:
            yc = jnp.concatenate([yca, ycb, ycm, ycn], axis=-1) @ w_out[l]
            xc = _layernorm(DEEPNORM_ALPHA * xc + mc[2] * yc, ln_g[l, 0], ln_b[l, 0])
            hc = xc * (1 + mc[4]) + mc[3]
            xc = _layernorm(DEEPNORM_ALPHA * xc + mc[5] * _conv_ffn(hc, ffn_w_up[l], ffn_conv[l], ffn_w_down[l]),
                            ln_g[l, 1], ln_b[l, 1])
    return x
```

```python
import functools
import math

import numpy as np
import jax
import jax.numpy as jnp
from jax import lax
from jax.experimental import pallas as pl
from jax.experimental.pallas import tpu as pltpu

F32 = jnp.float32
BF16 = jnp.bfloat16

HEADS = 4
HEAD_DIM = 128
GROUP_W = HEADS * HEAD_DIM
GRID_W = 64
NA_KR = 8
NA_KC = 16
NA_BLOCK_ROWS = 4
NA_UNION_ROWS = NA_BLOCK_ROWS + NA_KR - 1
MLA_Q_RANK = 384
MLA_KV_RANK = 128
MLA_NOPE = 128
MLA_ROPE = 64
ROPE_THETA = 10000.0
LN_EPS = 1e-5
RMS_EPS = 1e-6
NEG = -1e30

SUB = 256
GDN_CHUNK = 128
MOD_ROWS = 16
VMEM_LIMIT = 56 * 1024 * 1024

C_GQ, C_GK, C_GV, C_GZ = 0, 512, 1024, 1536
C_DQ, C_DK, C_DV = 2048, 2560, 3072
C_NQ, C_NK, C_NV = 3584, 4096, 4608
C_MLA = 5120
MLA_W = 640
C_AB = 5760
N_PROJ = 5888


def _cparams(sem):
    return pltpu.CompilerParams(dimension_semantics=sem, vmem_limit_bytes=VMEM_LIMIT)


def _sigmoid(x):
    return 1.0 / (1.0 + jnp.exp(-x))


def _dot(a, b):
    return jnp.dot(a, b, preferred_element_type=F32)


def _dot_nt(a, b):
    return lax.dot_general(a, b, (((1,), (1,)), ((), ())), preferred_element_type=F32)


def _group_of(blk, nblk_b, nb):
    t = blk % nblk_b
    return jnp.where(t == 0, nb, blk // nblk_b), t


def _mod_kernel(c_ref, w_ref, b_ref, o_ref):
    c = c_ref[...]
    s = (c * _sigmoid(c)).astype(BF16)
    o_ref[0] = _dot(s, w_ref[0].astype(BF16)) + b_ref[0]


def _modulation(cvec, w_mod, b_mod):
    depth, d, n = w_mod.shape
    tn = 1536
    return pl.pallas_call(
        _mod_kernel,
        out_shape=jax.ShapeDtypeStruct((depth, MOD_ROWS, n), F32),
        grid=(depth, n // tn),
        in_specs=[
            pl.BlockSpec((MOD_ROWS, d), lambda l, j: (0, 0)),
            pl.BlockSpec((1, d, tn), lambda l, j: (l, 0, j)),
            pl.BlockSpec((1, 1, tn), lambda l, j: (l, 0, j)),
        ],
        out_specs=pl.BlockSpec((1, MOD_ROWS, tn), lambda l, j: (l, 0, j)),
        compiler_params=_cparams(("parallel", "parallel")),
        name="modulation",
    )(cvec, w_mod, b_mod.reshape(depth, 1, n))


def _inproj_kernel(x_ref, mod_ref, w_ref, o_ref, og_ref, h_scr, *, nsub, nblk_b, nb, d, nj):
    i = pl.program_id(0)
    j = pl.program_id(1)

    @pl.when(j == 0)
    def _():
        for s in range(nsub):
            g, _ = _group_of(i * nsub + s, nblk_b, nb)
            shift = mod_ref[0, pl.ds(g, 1), 0:d]
            scale = mod_ref[0, pl.ds(g, 1), d:2 * d]
            xs = x_ref[s * SUB:(s + 1) * SUB, :]
            h_scr[s * SUB:(s + 1) * SUB, :] = (xs * (1.0 + scale) + shift).astype(BF16)

    acc = _dot(h_scr[...], w_ref[...])
    o_ref[...] = acc.astype(BF16)

    @pl.when(j == nj - 1)
    def _():
        og_ref[...] = acc[:, acc.shape[1] - 128:]


def _inproj(xu, mod_l, w_in_p, *, nblk_b, nb):
    rows, d = xu.shape
    n = w_in_p.shape[1]
    tm, tn = 2 * SUB, n // 2
    nj = n // tn
    kern = functools.partial(_inproj_kernel, nsub=tm // SUB, nblk_b=nblk_b, nb=nb, d=d, nj=nj)
    return pl.pallas_call(
        kern,
        out_shape=(jax.ShapeDtypeStruct((rows, n), BF16), jax.ShapeDtypeStruct((rows, 128), F32)),
        grid=(rows // tm, nj),
        in_specs=[
            pl.BlockSpec((tm, d), lambda i, j: (i, 0)),
            pl.BlockSpec((1, MOD_ROWS, 6 * d), lambda i, j: (0, 0, 0)),
            pl.BlockSpec((d, tn), lambda i, j: (0, j)),
        ],
        out_specs=(pl.BlockSpec((tm, tn), lambda i, j: (i, j)),
                   pl.BlockSpec((tm, 128), lambda i, j: (i, 0))),
        scratch_shapes=[pltpu.VMEM((tm, d), BF16)],
        compiler_params=_cparams(("parallel", "arbitrary")),
        name="inproj",
    )(xu, mod_l, w_in_p)


def _rope_slab(x, cos, sin):
    lane = lax.broadcasted_iota(jnp.int32, x.shape, 1)
    odd = (lane & 16) != 0
    partner = jnp.where(odd, pltpu.roll(x, 16, 1), pltpu.roll(x, 128 - 16, 1))
    return x * cos + partner * sin


def _diff_prep_kernel(p_ref, cos_ref, sin_ref, o_ref, *, scale):
    cos = cos_ref[...]
    sin = sin_ref[...]
    for s in range(2 * HEADS):
        x = p_ref[:, s * 128:(s + 1) * 128].astype(F32)
        r = _rope_slab(x, cos, sin)
        if s < HEADS:
            r = r * scale
        o_ref[:, s * 128:(s + 1) * 128] = r.astype(BF16)


def _diff_prep(pq, cos, sin, *, nblk_b):
    rows = pq.shape[0]
    tm = SUB
    kern = functools.partial(_diff_prep_kernel, scale=(HEAD_DIM // 2) ** -0.5)
    return pl.pallas_call(
        kern,
        out_shape=jax.ShapeDtypeStruct((rows, 2 * GROUP_W), BF16),
        grid=(rows // tm,),
        in_specs=[
            pl.BlockSpec((tm, 2 * GROUP_W), lambda i: (i, C_DQ // (2 * GROUP_W))),
            pl.BlockSpec((tm, 128), lambda i: (i % nblk_b, 0)),
            pl.BlockSpec((tm, 128), lambda i: (i % nblk_b, 0)),
        ],
        out_specs=pl.BlockSpec((tm, 2 * GROUP_W), lambda i: (i, 0)),
        compiler_params=_cparams(("parallel",)),
        name="diff_prep",
    )(pq, cos, sin)


def _mla_prep_kernel(p_ref, gq_ref, gkv_ref, wq_ref, wkv_ref, cos_ref, sin_ref,
                     q_ref, k_ref, v_ref, *, scale):
    cos = cos_ref[...]
    sin = sin_ref[...]
    cq = p_ref[:, 0:MLA_Q_RANK].astype(F32)
    cq = cq * lax.rsqrt(jnp.mean(cq * cq, axis=-1, keepdims=True) + RMS_EPS) * gq_ref[...]
    q = _dot(cq.astype(BF16), wq_ref[...])
    ckv = p_ref[:, MLA_Q_RANK:MLA_Q_RANK + MLA_KV_RANK].astype(F32)
    ckv = ckv * lax.rsqrt(jnp.mean(ckv * ckv, axis=-1, keepdims=True) + RMS_EPS) * gkv_ref[...]
    kv = _dot(ckv.astype(BF16), wkv_ref[...])
    kr = _rope_slab(p_ref[:, MLA_Q_RANK + MLA_KV_RANK:MLA_W].astype(F32), cos, sin).astype(BF16)
    for h in range(HEADS):
        q_ref[:, h * 256:h * 256 + 128] = (q[:, h * 256:h * 256 + 128] * scale).astype(BF16)
        qr = _rope_slab(q[:, h * 256 + 128:(h + 1) * 256], cos, sin)
        q_ref[:, h * 256 + 128:(h + 1) * 256] = (qr * scale).astype(BF16)
        k_ref[:, h * 256:h * 256 + 128] = kv[:, h * 128:(h + 1) * 128].astype(BF16)
        k_ref[:, h * 256 + 128:(h + 1) * 256] = kr
    v_ref[...] = kv[:, GROUP_W:].astype(BF16)


def _mla_prep(pq, gq, gkv, wq, wkv, cos, sin, *, nblk_b):
    rows = pq.shape[0]
    tm = SUB
    kern = functools.partial(_mla_prep_kernel, scale=(MLA_NOPE + MLA_ROPE) ** -0.5)
    return pl.pallas_call(
        kern,
        out_shape=(jax.ShapeDtypeStruct((rows, HEADS * 256), BF16),
                   jax.ShapeDtypeStruct((rows, HEADS * 256), BF16),
                   jax.ShapeDtypeStruct((rows, GROUP_W), BF16)),
        grid=(rows // tm,),
        in_specs=[
            pl.BlockSpec((tm, MLA_W), lambda i: (i, C_MLA // MLA_W)),
            pl.BlockSpec((1, MLA_Q_RANK), lambda i: (0, 0)),
            pl.BlockSpec((1, MLA_KV_RANK), lambda i: (0, 0)),
            pl.BlockSpec((MLA_Q_RANK, HEADS * 256), lambda i: (0, 0)),
            pl.BlockSpec((MLA_KV_RANK, 2 * GROUP_W), lambda i: (0, 0)),
            pl.BlockSpec((tm, 128), lambda i: (i % nblk_b, 0)),
            pl.BlockSpec((tm, 128), lambda i: (i % nblk_b, 0)),
        ],
        out_specs=(pl.BlockSpec((tm, HEADS * 256), lambda i: (i, 0)),
                   pl.BlockSpec((tm, HEADS * 256), lambda i: (i, 0)),
                   pl.BlockSpec((tm, GROUP_W), lambda i: (i, 0))),
        compiler_params=_cparams(("parallel",)),
        name="mla_prep",
    )(pq, gq, gkv, wq, wkv, cos, sin)


def _flash_kernel(*refs, nsub, tk, n_lat_steps, out_scale):
    if nsub == 2:
        lam_ref, g_ref, q_ref, k_ref, v_ref, o_ref = refs
    else:
        q_ref, k_ref, v_ref, o_ref = refs
    qt = pl.program_id(2)
    q = q_ref[...]
    tq = q.shape[0]
    if nsub == 2:
        lane = lax.broadcasted_iota(jnp.int32, q.shape, 1)
        zero = jnp.zeros_like(q)
        qs = (jnp.where(lane < 64, q, zero), jnp.where(lane >= 64, q, zero))
    else:
        qs = (q,)

    def step(kblk, vblk, carry):
        new = []
        for qi, (m, l, acc) in zip(qs, carry):
            s = _dot_nt(qi, kblk)
            m_new = jnp.maximum(m, jnp.max(s, axis=-1, keepdims=True))
            p = jnp.exp(s - m_new)
            alpha = jnp.exp(m - m_new)
            l = alpha * l + jnp.sum(p, axis=-1, keepdims=True)
            acc = alpha * acc + _dot(p.astype(BF16), vblk)
            new.append((m_new, l, acc))
        return tuple(new)

    init = tuple((jnp.full((tq, 1), NEG, F32), jnp.zeros((tq, 1), F32), jnp.zeros((tq, HEAD_DIM), F32))
                 for _ in qs)
    carry = step(k_ref[0:SUB, :], v_ref[0:SUB, :], init)

    def body(j, carry):
        start = pl.multiple_of(SUB + j * tk, SUB)
        return step(k_ref[pl.ds(start, tk), :], v_ref[pl.ds(start, tk), :], carry)

    n = jnp.where(qt == 0, 0, n_lat_steps)
    carry = lax.fori_loop(0, n, body, carry)

    if nsub == 2:
        (_, l1, a1), (_, l2, a2) = carry
        o = a1 / l1 - lam_ref[0] * (a2 / l2)
        o = o * lax.rsqrt(jnp.mean(o * o, axis=-1, keepdims=True) + RMS_EPS) * g_ref[...] * out_scale
    else:
        (_, l1, a1), = carry
        o = a1 / l1
    o_ref[...] = o.astype(o_ref.dtype)


def _flash(q_arr, k_arr, v_arr, *, nb, rb, dq, qcol, kcol, vcol, nsub=1, lam=None, norm_g=None,
           out_scale=1.0):
    rows = q_arr.shape[0]
    nblk_b = rb // SUB
    lat = rb - SUB
    tk = 512 if lat % 512 == 0 else SUB
    kern = functools.partial(_flash_kernel, nsub=nsub, tk=tk, n_lat_steps=lat // tk, out_scale=out_scale)
    in_specs = [
        pl.BlockSpec((SUB, dq), lambda b, h, t: (b * nblk_b + t, qcol // dq + h)),
        pl.BlockSpec((rb, dq), lambda b, h, t: (b, kcol // dq + h)),
        pl.BlockSpec((rb, HEAD_DIM), lambda b, h, t: (b, vcol // HEAD_DIM + h)),
    ]
    args = [q_arr, k_arr, v_arr]
    if nsub == 2:
        in_specs = [pl.BlockSpec(memory_space=pltpu.SMEM),
                    pl.BlockSpec((1, HEAD_DIM), lambda b, h, t: (0, 0))] + in_specs
        args = [lam, norm_g] + args
    return pl.pallas_call(
        kern,
        out_shape=jax.ShapeDtypeStruct((rows, GROUP_W), BF16),
        grid=(nb, HEADS, nblk_b),
        in_specs=in_specs,
        out_specs=pl.BlockSpec((SUB, HEAD_DIM), lambda b, h, t: (b * nblk_b + t, h)),
        compiler_params=_cparams(("parallel", "parallel", "arbitrary")),
        name="flash_diff" if nsub == 2 else "flash_mla",
    )(*args)


def _na_kernel(q_ref, k_ref, v_ref, bias_ref, o_ref, *, grid_rows, scale):
    qt = pl.program_id(2)
    qb = jnp.maximum(qt - 1, 0)
    u0 = jnp.clip(qb * NA_BLOCK_ROWS - NA_KR // 2, 0, grid_rows - NA_UNION_ROWS)
    start = pl.multiple_of(SUB + u0 * GRID_W, GRID_W)
    nn = NA_UNION_ROWS * GRID_W
    q = (q_ref[...].astype(F32) * scale).astype(BF16)
    kc = k_ref[0:SUB, :]
    vc = v_ref[0:SUB, :]
    kn = k_ref[pl.ds(start, nn), :]
    vn = v_ref[pl.ds(start, nn), :]
    s_c = _dot_nt(q, kc) + bias_ref[0, 0, :, 0:SUB]
    s_n = _dot_nt(q, kn) + bias_ref[0, 0, :, SUB:]
    m = jnp.maximum(jnp.max(s_c, axis=-1, keepdims=True), jnp.max(s_n, axis=-1, keepdims=True))
    e_c = jnp.exp(s_c - m)
    e_n = jnp.exp(s_n - m)
    l = jnp.sum(e_c, axis=-1, keepdims=True) + jnp.sum(e_n, axis=-1, keepdims=True)
    o = (_dot(e_c.astype(BF16), vc) + _dot(e_n.astype(BF16), vn)) / l
    o_ref[...] = o.astype(o_ref.dtype)


def _na_bias_table(rpb, grid_rows):
    assert grid_rows >= NA_UNION_ROWS + NA_BLOCK_ROWS
    j = np.arange(NA_BLOCK_ROWS)[:, None, None, None]
    cq = np.arange(GRID_W)[None, :, None, None]
    iu = np.arange(NA_UNION_ROWS)[None, None, :, None]
    ck = np.arange(GRID_W)[None, None, None, :]
    cs = np.clip(cq - NA_KC // 2, 0, GRID_W - NA_KC)
    col_ok = (ck >= cs) & (ck < cs + NA_KC)
    dx = np.clip(ck - cq + NA_KC - 1, 0, 2 * NA_KC - 2)
    dx = np.broadcast_to(dx, (NA_BLOCK_ROWS, GRID_W, NA_UNION_ROWS, GRID_W))
    tabs = []
    for r_off, w_off in ((j, 0 * j), (j + NA_KR // 2, j), (j + NA_UNION_ROWS - NA_BLOCK_ROWS, 0 * j + NA_UNION_ROWS - NA_KR)):
        dy = iu - r_off + NA_KR - 1
        ok = (iu - w_off >= 0) & (iu - w_off < NA_KR) & col_ok
        dy = np.broadcast_to(np.clip(dy, 0, 2 * NA_KR - 2), dx.shape)
        ok = np.broadcast_to(ok, dx.shape)
        vals = rpb[:, dy, dx]
        tabs.append(jnp.where(ok[None], vals, NEG))
    tabs.append(jnp.full_like(tabs[0], NEG))
    nb_part = jnp.stack(tabs, axis=1).reshape(rpb.shape[0], 4, NA_BLOCK_ROWS * GRID_W, NA_UNION_ROWS * GRID_W)
    ctx_part = jnp.zeros(nb_part.shape[:3] + (SUB,), F32)
    return jnp.concatenate([ctx_part, nb_part], axis=-1)


def _na(pq, bias, *, nb, rb):
    rows = pq.shape[0]
    nblk_b = rb // SUB
    grid_rows = (rb - SUB) // GRID_W
    nqb = nblk_b - 1
    kern = functools.partial(_na_kernel, grid_rows=grid_rows, scale=HEAD_DIM ** -0.5)

    def bias_map(b, h, t):
        ty = jnp.where(t == 0, 3, jnp.where(t == 1, 0, jnp.where(t == nqb, 2, 1)))
        return (h, ty, 0, 0)

    return pl.pallas_call(
        kern,
        out_shape=jax.ShapeDtypeStruct((rows, GROUP_W), BF16),
        grid=(nb, HEADS, nblk_b),
        in_specs=[
            pl.BlockSpec((SUB, HEAD_DIM), lambda b, h, t: (b * nblk_b + t, C_NQ // HEAD_DIM + h)),
            pl.BlockSpec((rb, HEAD_DIM), lambda b, h, t: (b, C_NK // HEAD_DIM + h)),
            pl.BlockSpec((rb, HEAD_DIM), lambda b, h, t: (b, C_NV // HEAD_DIM + h)),
            pl.BlockSpec((1, 1, SUB, bias.shape[-1]), bias_map),
        ],
        out_specs=pl.BlockSpec((SUB, HEAD_DIM), lambda b, h, t: (b * nblk_b + t, h)),
        compiler_params=_cparams(("parallel", "parallel", "arbitrary")),
        name="na",
    )(pq, pq, pq, bias)


def _gdn_prep_kernel(x_ref, prev_ref, next_ref, w_ref, g_ref, av_ref, dtb_ref, o_ref, og_ref, *, nblk_b):
    i = pl.program_id(0)
    t = i % nblk_b
    keep_prev = jnp.where((t == 0) | (t == 1), 0.0, 1.0)
    keep_next = jnp.where((t == 0) | (t == nblk_b - 1), 0.0, 1.0)
    x = x_ref[...].astype(F32)
    tm = x.shape[0]
    row = lax.broadcasted_iota(jnp.int32, x.shape, 0)
    halo_p = prev_ref[15:16, :].astype(F32) * keep_prev
    halo_n = next_ref[0:1, :].astype(F32) * keep_next
    xp = jnp.where(row == 0, halo_p, pltpu.roll(x, 1, 0))
    xn = jnp.where(row == tm - 1, halo_n, pltpu.roll(x, tm - 1, 0))
    c = xp * w_ref[0:1, :] + x * w_ref[1:2, :] + xn * w_ref[2:3, :]
    s = c * _sigmoid(c)
    for h in range(2 * HEADS):
        sh = s[:, h * 128:(h + 1) * 128]
        nrm = lax.rsqrt(jnp.sum(sh * sh, axis=-1, keepdims=True) + RMS_EPS)
        if h < HEADS:
            nrm = nrm * HEAD_DIM ** -0.5
        o_ref[:, h * 128:(h + 1) * 128] = (sh * nrm).astype(BF16)
    o_ref[:, 2 * GROUP_W:] = s[:, 2 * GROUP_W:].astype(BF16)
    g = g_ref[...]
    z = g + dtb_ref[...]
    softplus = jnp.maximum(z, 0.0) + jnp.log(1.0 + jnp.exp(-jnp.abs(z)))
    lane = lax.broadcasted_iota(jnp.int32, g.shape, 1)
    og_ref[...] = jnp.where(lane < 2 * HEADS, -av_ref[...] * softplus, _sigmoid(g))


def _gdn_prep(pq, pg, conv_w, a_vec, dtb_vec, *, nblk_b):
    rows = pq.shape[0]
    tm = SUB
    w3 = 3 * GROUP_W
    nhalo = rows // 16
    kern = functools.partial(_gdn_prep_kernel, nblk_b=nblk_b)
    return pl.pallas_call(
        kern,
        out_shape=(jax.ShapeDtypeStruct((rows, w3), BF16), jax.ShapeDtypeStruct((rows, 128), F32)),
        grid=(rows // tm,),
        in_specs=[
            pl.BlockSpec((tm, w3), lambda i: (i, 0)),
            pl.BlockSpec((16, w3), lambda i: (jnp.maximum(i * (tm // 16) - 1, 0), 0)),
            pl.BlockSpec((16, w3), lambda i: (jnp.minimum((i + 1) * (tm // 16), nhalo - 1), 0)),
            pl.BlockSpec((3, w3), lambda i: (0, 0)),
            pl.BlockSpec((tm, 128), lambda i: (i, 0)),
            pl.BlockSpec((1, 128), lambda i: (0, 0)),
            pl.BlockSpec((1, 128), lambda i: (0, 0)),
        ],
        out_specs=(pl.BlockSpec((tm, w3), lambda i: (i, 0)), pl.BlockSpec((tm, 128), lambda i: (i, 0))),
        compiler_params=_cparams(("parallel",)),
        name="gdn_prep",
    )(pq, pq, pq, conv_w, pg, a_vec, dtb_vec)


def _split3(x):
    hi = x.astype(BF16)
    r = x - hi.astype(F32)
    mid = r.astype(BF16)
    lo = (r - mid.astype(F32)).astype(BF16)
    return hi, mid, lo


def _gdn_chain(q, k, v, gcol, grow, beta, glast, s_prev, lower):
    c = q.shape[0]
    row = lax.broadcasted_iota(jnp.int32, (c, c), 0)
    col = lax.broadcasted_iota(jnp.int32, (c, c), 1)
    incl = (row >= col) if lower else (row <= col)
    strict = (row > col) if lower else (row < col)
    decay = jnp.exp(jnp.where(incl, gcol - grow, NEG))
    kf = k.astype(F32)
    kb = kf * beta
    a_mat = jnp.where(strict, _dot_nt(kb.astype(BF16), k) * decay, 0.0)
    qk = jnp.where(incl, _dot_nt(q, k) * decay, 0.0)
    xor = row ^ col
    tinv = jnp.where(row == col, 1.0, 0.0) - jnp.where(xor < 2, a_mat, 0.0)
    s = 2
    while s < c:
        a_off = jnp.where((xor >= s) & (xor < 2 * s), a_mat, 0.0).astype(BF16)
        y = _dot(a_off, tinv.astype(BF16))
        tinv = tinv - _dot(tinv.astype(BF16), y.astype(BF16))
        s *= 2
    eg = jnp.exp(gcol)
    rhs = jnp.concatenate([kb * eg, v.astype(F32) * beta], axis=1).astype(BF16)
    wu = _dot(tinv.astype(BF16), rhs)
    w = wu[:, :HEAD_DIM]
    u = wu[:, HEAD_DIM:]
    sb = s_prev.astype(BF16)
    v_new = u - _dot(w.astype(BF16), sb)
    vnb = v_new.astype(BF16)
    o = _dot((q.astype(F32) * eg).astype(BF16), sb) + _dot(qk.astype(BF16), vnb)
    kd_t = (kf * jnp.exp(glast - gcol)).T.astype(BF16)
    s_new = s_prev * jnp.exp(glast) + _dot(kd_t, vnb)
    return o, s_new


def _gdn_kernel(xf_ref, xb_ref, gf_ref, gb_ref, of_ref, ob_ref, s_scr):
    st = pl.program_id(1)

    @pl.when(st == 0)
    def _():
        s_scr[...] = jnp.zeros_like(s_scr)

    c = GDN_CHUNK
    row = lax.broadcasted_iota(jnp.int32, (c, c), 0)
    col = lax.broadcasted_iota(jnp.int32, (c, c), 1)
    for d, (x_ref, g_ref, o_ref) in enumerate(((xf_ref, gf_ref, of_ref), (xb_ref, gb_ref, ob_ref))):
        lower = d == 0
        gates = g_ref[...]
        tri = jnp.where((row >= col) if lower else (row <= col), 1.0, 0.0).astype(BF16)
        hi, mid, lo = _split3(gates)
        gsum = _dot(tri, hi) + _dot(tri, mid) + _dot(tri, lo)
        gsum_t = gsum.T
        for h in range(HEADS):
            gi = d * HEADS + h
            bi = 2 * HEADS + gi
            gcol = gsum[:, gi:gi + 1]
            grow = gsum_t[gi:gi + 1, :]
            glast = gsum[c - 1:c, gi:gi + 1] if lower else gsum[0:1, gi:gi + 1]
            beta = gates[:, bi:bi + 1]
            q = x_ref[:, h * 128:(h + 1) * 128]
            k = x_ref[:, GROUP_W + h * 128:GROUP_W + (h + 1) * 128]
            v = x_ref[:, 2 * GROUP_W + h * 128:2 * GROUP_W + (h + 1) * 128]
            o, s_new = _gdn_chain(q, k, v, gcol, grow, beta, glast, s_scr[d, h], lower)
            s_scr[d, h] = s_new
            o_ref[:, h * 128:(h + 1) * 128] = o


def _gdn(gx, gg, *, nb, rb):
    rows = gx.shape[0]
    c = GDN_CHUNK
    nch = rb // c
    nctx = SUB // c
    w3 = 3 * GROUP_W

    def fmap(b, s):
        return (b * nch + s, 0)

    def bmap(b, s):
        return (b * nch + jnp.where(s < nctx, nctx - 1 - s, nch + nctx - 1 - s), 0)

    return pl.pallas_call(
        _gdn_kernel,
        out_shape=(jax.ShapeDtypeStruct((rows, GROUP_W), F32), jax.ShapeDtypeStruct((rows, GROUP_W), F32)),
        grid=(nb, nch),
        in_specs=[
            pl.BlockSpec((c, w3), fmap), pl.BlockSpec((c, w3), bmap),
            pl.BlockSpec((c, 128), fmap), pl.BlockSpec((c, 128), bmap),
        ],
        out_specs=(pl.BlockSpec((c, GROUP_W), fmap), pl.BlockSpec((c, GROUP_W), bmap)),
        scratch_shapes=[pltpu.VMEM((2, HEADS, HEAD_DIM, HEAD_DIM), F32)],
        compiler_params=_cparams(("parallel", "arbitrary")),
        name="gdn",
    )(gx, gx, gg, gg)


def _gdn_finish_kernel(of_ref, ob_ref, z_ref, g_ref, o_ref):
    z = z_ref[...].astype(F32)
    gate = z * _sigmoid(z)
    for h in range(HEADS):
        sl = slice(h * 128, (h + 1) * 128)
        o = of_ref[:, sl] + ob_ref[:, sl]
        o = o * lax.rsqrt(jnp.mean(o * o, axis=-1, keepdims=True) + RMS_EPS) * g_ref[...]
        o_ref[:, sl] = (o * gate[:, sl]).astype(BF16)


def _gdn_finish(o_f, o_b, pq, norm_g):
    rows = o_f.shape[0]
    tm = 2 * SUB
    return pl.pallas_call(
        _gdn_finish_kernel,
        out_shape=jax.ShapeDtypeStruct((rows, GROUP_W), BF16),
        grid=(rows // tm,),
        in_specs=[
            pl.BlockSpec((tm, GROUP_W), lambda i: (i, 0)),
            pl.BlockSpec((tm, GROUP_W), lambda i: (i, 0)),
            pl.BlockSpec((tm, GROUP_W), lambda i: (i, C_GZ // GROUP_W)),
            pl.BlockSpec((1, HEAD_DIM), lambda i: (0, 0)),
        ],
        out_specs=pl.BlockSpec((tm, GROUP_W), lambda i: (i, 0)),
        compiler_params=_cparams(("parallel",)),
        name="gdn_finish",
    )(o_f, o_b, pq, norm_g)


def _layernorm_rows(z, g, b):
    mu = jnp.mean(z, axis=-1, keepdims=True)
    zc = z - mu
    var = jnp.mean(zc * zc, axis=-1, keepdims=True)
    return zc * lax.rsqrt(var + LN_EPS) * g + b


def _outproj_kernel(ya_ref, yb_ref, ym_ref, yn_ref, w_ref, x_ref, mod_ref, g_ref, b_ref, o_ref,
                    *, nsub, nblk_b, nb, d, alpha):
    i = pl.program_id(0)
    y = _dot(ya_ref[...], w_ref[0:GROUP_W, :])
    y = y + _dot(yb_ref[...], w_ref[GROUP_W:2 * GROUP_W, :])
    y = y + _dot(ym_ref[...], w_ref[2 * GROUP_W:3 * GROUP_W, :])
    y = y + _dot(yn_ref[...], w_ref[3 * GROUP_W:4 * GROUP_W, :])
    for s in range(nsub):
        g, _ = _group_of(i * nsub + s, nblk_b, nb)
        gate = mod_ref[0, pl.ds(g, 1), 2 * d:3 * d]
        sl = slice(s * SUB, (s + 1) * SUB)
        z = alpha * x_ref[sl, :] + gate * y[sl, :]
        o_ref[sl, :] = _layernorm_rows(z, g_ref[...], b_ref[...])


def _outproj(ys, w_out, xu, mod_l, ln_g, ln_b, *, nblk_b, nb, alpha):
    rows, d = xu.shape
    tm = 2 * SUB
    kern = functools.partial(_outproj_kernel, nsub=tm // SUB, nblk_b=nblk_b, nb=nb, d=d, alpha=alpha)
    yspec = pl.BlockSpec((tm, GROUP_W), lambda i: (i, 0))
    return pl.pallas_call(
        kern,
        out_shape=jax.ShapeDtypeStruct((rows, d), F32),
        grid=(rows // tm,),
        in_specs=[yspec, yspec, yspec, yspec,
                  pl.BlockSpec((4 * GROUP_W, d), lambda i: (0, 0)),
                  pl.BlockSpec((tm, d), lambda i: (i, 0)),
                  pl.BlockSpec((1, MOD_ROWS, 6 * d), lambda i: (0, 0, 0)),
                  pl.BlockSpec((1, d), lambda i: (0, 0)),
                  pl.BlockSpec((1, d), lambda i: (0, 0))],
        out_specs=pl.BlockSpec((tm, d), lambda i: (i, 0)),
        compiler_params=_cparams(("parallel",)),
        name="outproj",
    )(*ys, w_out, xu, mod_l, ln_g, ln_b)


def _ffn_kernel(x_ref, prev_ref, next_ref, mod_ref, wg_ref, wv_ref, cg_ref, cv_ref, wd_ref, g_ref, b_ref,
                o_ref, h_scr, acc_scr, *, nsub, nblk_b, nb, d, nj, alpha):
    i = pl.program_id(0)
    j = pl.program_id(1)
    tm = nsub * SUB

    @pl.when(j == 0)
    def _():
        acc_scr[...] = jnp.zeros_like(acc_scr)
        for s in range(nsub):
            g, _ = _group_of(i * nsub + s, nblk_b, nb)
            shift = mod_ref[0, pl.ds(g, 1), 3 * d:4 * d]
            scale = mod_ref[0, pl.ds(g, 1), 4 * d:5 * d]
            h_scr[8 + s * SUB:8 + (s + 1) * SUB, :] = x_ref[s * SUB:(s + 1) * SUB, :] * (1.0 + scale) + shift
            if s == 0:
                h_scr[0:8, :] = prev_ref[...] * (1.0 + scale) + shift
            if s == nsub - 1:
                h_scr[tm + 8:tm + 16, :] = next_ref[...] * (1.0 + scale) + shift

    row = lax.broadcasted_iota(jnp.int32, (tm, 1), 0)
    keep_prev = jnp.ones((tm, 1), F32)
    keep_next = jnp.ones((tm, 1), F32)
    for s in range(nsub):
        _, t = _group_of(i * nsub + s, nblk_b, nb)
        first = (t == 0) | (t == 1)
        last = (t == 0) | (t == nblk_b - 1)
        keep_prev = jnp.where((row == s * SUB) & first, 0.0, keep_prev)
        keep_next = jnp.where((row == s * SUB + SUB - 1) & last, 0.0, keep_next)

    hb = h_scr[...].astype(BF16)
    ext = tm + 16

    def conv(u, cw_ref):
        up = pltpu.roll(u, 1, 0)[8:tm + 8, :] * keep_prev
        un = pltpu.roll(u, ext - 1, 0)[8:tm + 8, :] * keep_next
        return up * cw_ref[0:1, :] + u[8:tm + 8, :] * cw_ref[1:2, :] + un * cw_ref[2:3, :]

    gate = conv(_dot(hb, wg_ref[...]), cg_ref)
    val = conv(_dot(hb, wv_ref[...]), cv_ref)
    act = (gate * _sigmoid(gate) * val).astype(BF16)
    acc_scr[...] += _dot(act, wd_ref[...])

    @pl.when(j == nj - 1)
    def _():
        for s in range(nsub):
            g, _ = _group_of(i * nsub + s, nblk_b, nb)
            gt = mod_ref[0, pl.ds(g, 1), 5 * d:6 * d]
            sl = slice(s * SUB, (s + 1) * SUB)
            z = alpha * x_ref[sl, :] + gt * acc_scr[sl, :]
            o_ref[sl, :] = _layernorm_rows(z, g_ref[...], b_ref[...])


def _ffn(xu, mod_l, w_up, conv_w, w_down, ln_g, ln_b, *, nblk_b, nb, alpha):
    rows, d = xu.shape
    dff = w_down.shape[0]
    tm, tc = 2 * SUB, 512
    nj = dff // tc
    nhalo = rows // 8
    kern = functools.partial(_ffn_kernel, nsub=tm // SUB, nblk_b=nblk_b, nb=nb, d=d, nj=nj, alpha=alpha)
    return pl.pallas_call(
        kern,
        out_shape=jax.ShapeDtypeStruct((rows, d), F32),
        grid=(rows // tm, nj),
        in_specs=[
            pl.BlockSpec((tm, d), lambda i, j: (i, 0)),
            pl.BlockSpec((8, d), lambda i, j: (jnp.maximum(i * (tm // 8) - 1, 0), 0)),
            pl.BlockSpec((8, d), lambda i, j: (jnp.minimum((i + 1) * (tm // 8), nhalo - 1), 0)),
            pl.BlockSpec((1, MOD_ROWS, 6 * d), lambda i, j: (0, 0, 0)),
            pl.BlockSpec((d, tc), lambda i, j: (0, j)),
            pl.BlockSpec((d, tc), lambda i, j: (0, nj + j)),
            pl.BlockSpec((3, tc), lambda i, j: (0, j)),
            pl.BlockSpec((3, tc), lambda i, j: (0, nj + j)),
            pl.BlockSpec((tc, d), lambda i, j: (j, 0)),
            pl.BlockSpec((1, d), lambda i, j: (0, 0)),
            pl.BlockSpec((1, d), lambda i, j: (0, 0)),
        ],
        out_specs=pl.BlockSpec((tm, d), lambda i, j: (i, 0)),
        scratch_shapes=[pltpu.VMEM((tm + 16, d), F32), pltpu.VMEM((tm, d), F32)],
        compiler_params=_cparams(("parallel", "arbitrary")),
        name="ffn",
    )(xu, xu, xu, mod_l, w_up, w_up, conv_w, conv_w, w_down, ln_g, ln_b)


def _rope_tables(n_lat):
    pos = jnp.arange(n_lat)
    row = (pos // GRID_W).astype(F32)
    col = (pos % GRID_W).astype(F32)
    half = MLA_ROPE // 2
    inv = ROPE_THETA ** (-jnp.arange(0, half, 2, dtype=F32) / half)
    ar = row[:, None] * inv[None, :]
    ac = col[:, None] * inv[None, :]
    cos64 = jnp.concatenate([jnp.cos(ar), jnp.cos(ar), jnp.cos(ac), jnp.cos(ac)], axis=-1)
    sin64 = jnp.concatenate([-jnp.sin(ar), jnp.sin(ar), -jnp.sin(ac), jnp.sin(ac)], axis=-1)
    one = jnp.ones((n_lat, 64), F32)
    zero = jnp.zeros((n_lat, 64), F32)

    def with_ctx(c, s):
        c = jnp.concatenate([jnp.ones((SUB, 128), F32), c], axis=0)
        s = jnp.concatenate([jnp.zeros((SUB, 128), F32), s], axis=0)
        return c, s

    diff = with_ctx(jnp.concatenate([cos64, cos64], -1), jnp.concatenate([sin64, sin64], -1))
    mla = with_ctx(jnp.concatenate([cos64, one], -1), jnp.concatenate([sin64, zero], -1))
    return diff, mla


def _prep_w_in(w_in):
    depth, d, _ = w_in.shape
    z = lambda n: jnp.zeros((depth, d, n), w_in.dtype)
    o_diff = 4 * GROUP_W + 4 * HEADS
    o_mla = o_diff + 3 * GROUP_W
    o_na = o_mla + MLA_Q_RANK + MLA_KV_RANK + MLA_ROPE
    w = jnp.concatenate([w_in[..., :4 * GROUP_W], w_in[..., o_diff:o_mla], w_in[..., o_na:],
                         w_in[..., o_mla:o_na], z(MLA_W - (o_na - o_mla)),
                         w_in[..., 4 * GROUP_W:o_diff], z(128 - 4 * HEADS)], axis=-1)
    assert w.shape[-1] == N_PROJ
    return w.astype(BF16)


def _prep_mla_weights(w_uq, w_ukv):
    depth = w_uq.shape[0]
    dq = MLA_NOPE + MLA_ROPE
    wq = w_uq.reshape(depth, MLA_Q_RANK, HEADS, dq)
    wq = jnp.concatenate([wq, jnp.zeros((depth, MLA_Q_RANK, HEADS, 256 - dq), w_uq.dtype)], axis=-1)
    wq = wq.reshape(depth, MLA_Q_RANK, HEADS * 256)
    wkv = w_ukv.reshape(depth, MLA_KV_RANK, HEADS, MLA_NOPE + HEAD_DIM)
    wkv = jnp.concatenate([wkv[..., :MLA_NOPE].reshape(depth, MLA_KV_RANK, GROUP_W),
                           wkv[..., MLA_NOPE:].reshape(depth, MLA_KV_RANK, GROUP_W)], axis=-1)
    return wq.astype(BF16), wkv.astype(BF16)


def _pad_lanes(v, n=128):
    return jnp.concatenate([v, jnp.zeros(v.shape[:-1] + (n - v.shape[-1],), v.dtype)], axis=-1)


def _mixers(pq, pg, lw, l, tabs, *, nb, rb):
    nblk_b = rb // SUB
    (dcos, dsin), (mcos, msin) = tabs
    gx, gg = _gdn_prep(pq, pg, lw["gdn_conv"], lw["gdn_a"], lw["gdn_dtb"], nblk_b=nblk_b)
    o_f, o_b = _gdn(gx, gg, nb=nb, rb=rb)
    ya = _gdn_finish(o_f, o_b, pq, lw["gdn_norm_g"])
    qk_d = _diff_prep(pq, dcos, dsin, nblk_b=nblk_b)
    lam_init = 0.8 - 0.6 * math.exp(-0.3 * l)
    yb = _flash(qk_d, qk_d, pq, nb=nb, rb=rb, dq=HEAD_DIM, qcol=0, kcol=GROUP_W, vcol=C_DV, nsub=2,
                lam=lw["diff_lam"], norm_g=lw["diff_norm_g"], out_scale=1.0 - lam_init)
    qm, km, vm = _mla_prep(pq, lw["mla_gq"], lw["mla_gkv"], lw["mla_wq"], lw["mla_wkv"], mcos, msin,
                           nblk_b=nblk_b)
    ym = _flash(qm, km, vm, nb=nb, rb=rb, dq=256, qcol=0, kcol=0, vcol=0)
    yn = _na(pq, lw["na_bias"], nb=nb, rb=rb)
    return ya, yb, ym, yn


def _layer(xu, lw, l, tabs, *, nb, rb, alpha):
    nblk_b = rb // SUB
    mod_l = lw["mod"]
    pq, pg = _inproj(xu, mod_l, lw["w_in"], nblk_b=nblk_b, nb=nb)
    ys = _mixers(pq, pg, lw, l, tabs, nb=nb, rb=rb)
    x1 = _outproj(ys, lw["w_out"], xu, mod_l, lw["ln_g0"], lw["ln_b0"], nblk_b=nblk_b, nb=nb, alpha=alpha)
    return _ffn(x1, mod_l, lw["ffn_w_up"], lw["ffn_conv"], lw["ffn_w_down"], lw["ln_g1"], lw["ln_b1"],
                nblk_b=nblk_b, nb=nb, alpha=alpha)


def _layer_weights(l, mod, w_in_p, gdn_conv, gdn_a_log, gdn_dt_bias, gdn_norm_g, diff_lambda, diff_norm_g,
                   mla_q_norm_g, mla_kv_norm_g, wq, wkv, na_bias, w_out_b, ln_g, ln_b, w_up_b, ffn_conv,
                   w_down_b):
    lf = diff_lambda[l].astype(F32)
    lam_init = 0.8 - 0.6 * math.exp(-0.3 * l)
    lam = jnp.exp(jnp.sum(lf[0] * lf[1])) - jnp.exp(jnp.sum(lf[2] * lf[3])) + lam_init
    return {
        "mod": mod[l:l + 1],
        "w_in": w_in_p[l],
        "gdn_conv": gdn_conv[l],
        "gdn_a": _pad_lanes(jnp.exp(gdn_a_log[l].astype(F32)).reshape(1, 2 * HEADS)),
        "gdn_dtb": _pad_lanes(gdn_dt_bias[l].astype(F32).reshape(1, 2 * HEADS)),
        "gdn_norm_g": gdn_norm_g[l].reshape(1, HEAD_DIM),
        "diff_lam": lam.reshape(1),
        "diff_norm_g": diff_norm_g[l].reshape(1, HEAD_DIM),
        "mla_gq": mla_q_norm_g[l].reshape(1, MLA_Q_RANK),
        "mla_gkv": mla_kv_norm_g[l].reshape(1, MLA_KV_RANK),
        "mla_wq": wq[l],
        "mla_wkv": wkv[l],
        "na_bias": na_bias[l],
        "w_out": w_out_b[l],
        "ln_g0": ln_g[l, 0:1], "ln_b0": ln_b[l, 0:1], "ln_g1": ln_g[l, 1:2], "ln_b1": ln_b[l, 1:2],
        "ffn_w_up": w_up_b[l],
        "ffn_conv": ffn_conv[l],
        "ffn_w_down": w_down_b[l],
    }


def kernel(x, c, ctx, c_ctx, w_mod, b_mod, w_in, gdn_conv, gdn_a_log, gdn_dt_bias, gdn_norm_g, diff_lambda,
           diff_norm_g, mla_q_norm_g, mla_kv_norm_g, mla_w_uq, mla_w_ukv, na_rpb, w_out, ln_g, ln_b,
           ffn_w_up, ffn_conv, ffn_w_down):
    nb, n_lat, d = x.shape
    depth = w_mod.shape[0]
    assert ctx.shape[1] == SUB and n_lat % (2 * SUB) == 0 and n_lat % GRID_W == 0
    assert (nb * (SUB + n_lat)) % (2 * SUB) == 0 and nb + 1 <= MOD_ROWS
    rb = SUB + n_lat
    alpha = (2 * depth) ** 0.25

    cvec = jnp.concatenate([c, c_ctx[None, :], jnp.zeros((MOD_ROWS - nb - 1, d), c.dtype)], axis=0)
    mod = _modulation(cvec, w_mod, b_mod)
    tabs = _rope_tables(n_lat)
    w_in_p = _prep_w_in(w_in)
    wq, wkv = _prep_mla_weights(mla_w_uq, mla_w_ukv)
    na_bias = jnp.stack([_na_bias_table(na_rpb[l], n_lat // GRID_W) for l in range(depth)])
    w_out_b = w_out.astype(BF16)
    w_up_b = ffn_w_up.astype(BF16)
    w_down_b = ffn_w_down.astype(BF16)

    xu = jnp.concatenate([ctx, x], axis=1).reshape(nb * rb, d)
    for l in range(depth):
        lw = _layer_weights(l, mod, w_in_p, gdn_conv, gdn_a_log, gdn_dt_bias, gdn_norm_g, diff_lambda,
                            diff_norm_g, mla_q_norm_g, mla_kv_norm_g, wq, wkv, na_bias, w_out_b, ln_g, ln_b,
                            w_up_b, ffn_conv, w_down_b)
        xu = _layer(xu, lw, l, tabs, nb=nb, rb=rb, alpha=alpha)
    return xu.reshape(nb, rb, d)[:, SUB:, :]
```

```python
import functools
import math

import numpy as np
import jax
import jax.numpy as jnp
from jax import lax
from jax.experimental import pallas as pl
from jax.experimental.pallas import tpu as pltpu

F32 = jnp.float32
BF16 = jnp.bfloat16

HEADS = 4
HEAD_DIM = 128
GROUP_W = HEADS * HEAD_DIM
GRID_W = 64
NA_KR = 8
NA_KC = 16
NA_BLOCK_ROWS = 4
NA_UNION_ROWS = NA_BLOCK_ROWS + NA_KR - 1
MLA_Q_RANK = 384
MLA_KV_RANK = 128
MLA_NOPE = 128
MLA_ROPE = 64
ROPE_THETA = 10000.0
LN_EPS = 1e-5
RMS_EPS = 1e-6
NEG = -1e30
LOG2E = math.log2(math.e)

SUB = 256
GDN_CHUNK = 128
MOD_ROWS = 16
VMEM_LIMIT = 56 * 1024 * 1024

C_GQ, C_GK, C_GV, C_GZ = 0, 512, 1024, 1536
C_DQ, C_DK, C_DV = 2048, 2560, 3072
C_NQ, C_NK, C_NV = 3584, 4096, 4608
C_MLA = 5120
MLA_W = 640
C_AB = 5760
N_PROJ = 5888


def _cparams(sem):
    return pltpu.CompilerParams(dimension_semantics=sem, vmem_limit_bytes=VMEM_LIMIT)


def _sigmoid(x):
    return 1.0 / (1.0 + jnp.exp(-x))


def _dot(a, b):
    return jnp.dot(a, b, preferred_element_type=F32)


def _dot_nt(a, b):
    return lax.dot_general(a, b, (((1,), (1,)), ((), ())), preferred_element_type=F32)


def _group_of(blk, nblk_b, nb):
    t = blk % nblk_b
    return jnp.where(t == 0, nb, blk // nblk_b), t


def _mod_kernel(c_ref, w_ref, b_ref, o_ref):
    c = c_ref[...]
    s = (c * _sigmoid(c)).astype(BF16)
    o_ref[0] = _dot(s, w_ref[0].astype(BF16)) + b_ref[0]


def _modulation(cvec, w_mod, b_mod):
    depth, d, n = w_mod.shape
    tn = 1536
    return pl.pallas_call(
        _mod_kernel,
        out_shape=jax.ShapeDtypeStruct((depth, MOD_ROWS, n), F32),
        grid=(depth, n // tn),
        in_specs=[
            pl.BlockSpec((MOD_ROWS, d), lambda l, j: (0, 0)),
            pl.BlockSpec((1, d, tn), lambda l, j: (l, 0, j)),
            pl.BlockSpec((1, 1, tn), lambda l, j: (l, 0, j)),
        ],
        out_specs=pl.BlockSpec((1, MOD_ROWS, tn), lambda l, j: (l, 0, j)),
        compiler_params=_cparams(("parallel", "parallel")),
        name="modulation",
    )(cvec, w_mod, b_mod.reshape(depth, 1, n))


def _inproj_kernel(x_ref, mod_ref, w_ref, o_ref, og_ref, h_scr, *, nsub, nblk_b, nb, d, nj):
    i = pl.program_id(0)
    j = pl.program_id(1)

    @pl.when(j == 0)
    def _():
        for s in range(nsub):
            g, _ = _group_of(i * nsub + s, nblk_b, nb)
            shift = mod_ref[0, pl.ds(g, 1), 0:d]
            scale = mod_ref[0, pl.ds(g, 1), d:2 * d]
            xs = x_ref[s * SUB:(s + 1) * SUB, :]
            h_scr[s * SUB:(s + 1) * SUB, :] = (xs * (1.0 + scale) + shift).astype(BF16)

    acc = _dot(h_scr[...], w_ref[...])
    o_ref[...] = acc.astype(BF16)

    @pl.when(j == nj - 1)
    def _():
        og_ref[...] = acc[:, acc.shape[1] - 128:]


def _inproj(xu, mod_l, w_in_p, *, nblk_b, nb):
    rows, d = xu.shape
    n = w_in_p.shape[1]
    tm, tn = 2 * SUB, n // 2
    nj = n // tn
    kern = functools.partial(_inproj_kernel, nsub=tm // SUB, nblk_b=nblk_b, nb=nb, d=d, nj=nj)
    return pl.pallas_call(
        kern,
        out_shape=(jax.ShapeDtypeStruct((rows, n), BF16), jax.ShapeDtypeStruct((rows, 128), F32)),
        grid=(rows // tm, nj),
        in_specs=[
            pl.BlockSpec((tm, d), lambda i, j: (i, 0)),
            pl.BlockSpec((1, MOD_ROWS, 6 * d), lambda i, j: (0, 0, 0)),
            pl.BlockSpec((d, tn), lambda i, j: (0, j)),
        ],
        out_specs=(pl.BlockSpec((tm, tn), lambda i, j: (i, j)),
                   pl.BlockSpec((tm, 128), lambda i, j: (i, 0))),
        scratch_shapes=[pltpu.VMEM((tm, d), BF16)],
        compiler_params=_cparams(("parallel", "arbitrary")),
        name="inproj",
    )(xu, mod_l, w_in_p)


def _rope_slab(x, cos, sin):
    lane = lax.broadcasted_iota(jnp.int32, x.shape, 1)
    odd = (lane & 16) != 0
    partner = jnp.where(odd, pltpu.roll(x, 16, 1), pltpu.roll(x, 128 - 16, 1))
    return x * cos + partner * sin


def _diff_prep_kernel(p_ref, v_ref, cos_ref, sin_ref, o_ref, vt_ref, *, scale):
    vt_ref[0] = v_ref[...].astype(F32).T.astype(BF16)
    cos = cos_ref[...]
    sin = sin_ref[...]
    for s in range(2 * HEADS):
        x = p_ref[:, s * 128:(s + 1) * 128].astype(F32)
        r = _rope_slab(x, cos, sin)
        if s < HEADS:
            r = r * scale
        o_ref[:, s * 128:(s + 1) * 128] = r.astype(BF16)


def _diff_prep(pq, cos, sin, *, nblk_b):
    rows = pq.shape[0]
    tm = SUB
    kern = functools.partial(_diff_prep_kernel, scale=(HEAD_DIM // 2) ** -0.5 * LOG2E)
    return pl.pallas_call(
        kern,
        out_shape=(jax.ShapeDtypeStruct((rows, 2 * GROUP_W), BF16),
                   jax.ShapeDtypeStruct((rows // tm, GROUP_W, tm), BF16)),
        grid=(rows // tm,),
        in_specs=[
            pl.BlockSpec((tm, 2 * GROUP_W), lambda i: (i, C_DQ // (2 * GROUP_W))),
            pl.BlockSpec((tm, GROUP_W), lambda i: (i, C_DV // GROUP_W)),
            pl.BlockSpec((tm, 128), lambda i: (i % nblk_b, 0)),
            pl.BlockSpec((tm, 128), lambda i: (i % nblk_b, 0)),
        ],
        out_specs=(pl.BlockSpec((tm, 2 * GROUP_W), lambda i: (i, 0)),
                   pl.BlockSpec((1, GROUP_W, tm), lambda i: (i, 0, 0))),
        compiler_params=_cparams(("parallel",)),
        name="diff_prep",
    )(pq, pq, cos, sin)


def _mla_prep_kernel(p_ref, gq_ref, gkv_ref, wq_ref, wkv_ref, cos_ref, sin_ref,
                     q_ref, k_ref, v_ref, *, scale):
    cos = cos_ref[...]
    sin = sin_ref[...]
    cq = p_ref[:, 0:MLA_Q_RANK].astype(F32)
    cq = cq * lax.rsqrt(jnp.mean(cq * cq, axis=-1, keepdims=True) + RMS_EPS) * gq_ref[...]
    q = _dot(cq.astype(BF16), wq_ref[...])
    ckv = p_ref[:, MLA_Q_RANK:MLA_Q_RANK + MLA_KV_RANK].astype(F32)
    ckv = ckv * lax.rsqrt(jnp.mean(ckv * ckv, axis=-1, keepdims=True) + RMS_EPS) * gkv_ref[...]
    kv = _dot(ckv.astype(BF16), wkv_ref[...])
    kr = _rope_slab(p_ref[:, MLA_Q_RANK + MLA_KV_RANK:MLA_W].astype(F32), cos, sin).astype(BF16)
    for h in range(HEADS):
        q_ref[:, h * 256:h * 256 + 128] = (q[:, h * 256:h * 256 + 128] * scale).astype(BF16)
        qr = _rope_slab(q[:, h * 256 + 128:(h + 1) * 256], cos, sin)
        q_ref[:, h * 256 + 128:(h + 1) * 256] = (qr * scale).astype(BF16)
        k_ref[:, h * 256:h * 256 + 128] = kv[:, h * 128:(h + 1) * 128].astype(BF16)
        k_ref[:, h * 256 + 128:(h + 1) * 256] = kr
    v_ref[0] = kv[:, GROUP_W:].T.astype(BF16)


def _mla_prep(pq, gq, gkv, wq, wkv, cos, sin, *, nblk_b):
    rows = pq.shape[0]
    tm = SUB
    kern = functools.partial(_mla_prep_kernel, scale=(MLA_NOPE + MLA_ROPE) ** -0.5 * LOG2E)
    return pl.pallas_call(
        kern,
        out_shape=(jax.ShapeDtypeStruct((rows, HEADS * 256), BF16),
                   jax.ShapeDtypeStruct((rows, HEADS * 256), BF16),
                   jax.ShapeDtypeStruct((rows // tm, GROUP_W, tm), BF16)),
        grid=(rows // tm,),
        in_specs=[
            pl.BlockSpec((tm, MLA_W), lambda i: (i, C_MLA // MLA_W)),
            pl.BlockSpec((1, MLA_Q_RANK), lambda i: (0, 0)),
            pl.BlockSpec((1, MLA_KV_RANK), lambda i: (0, 0)),
            pl.BlockSpec((MLA_Q_RANK, HEADS * 256), lambda i: (0, 0)),
            pl.BlockSpec((MLA_KV_RANK, 2 * GROUP_W), lambda i: (0, 0)),
            pl.BlockSpec((tm, 128), lambda i: (i % nblk_b, 0)),
            pl.BlockSpec((tm, 128), lambda i: (i % nblk_b, 0)),
        ],
        out_specs=(pl.BlockSpec((tm, HEADS * 256), lambda i: (i, 0)),
                   pl.BlockSpec((tm, HEADS * 256), lambda i: (i, 0)),
                   pl.BlockSpec((1, GROUP_W, tm), lambda i: (i, 0, 0))),
        compiler_params=_cparams(("parallel",)),
        name="mla_prep",
    )(pq, gq, gkv, wq, wkv, cos, sin)


def _flash_kernel(*refs, nsub, hp, dq, n_lat_pairs):
    if nsub == 2:
        lam_ref, g_ref, q_ref, k_ref, vt_ref, o_ref, s_scr, p_scr, acc_scr = refs
    else:
        q_ref, k_ref, vt_ref, o_ref, s_scr, p_scr, acc_scr = refs
    qt = pl.program_id(2)
    tq = q_ref.shape[0]
    tk = 2 * SUB
    qs = []
    for hh in range(hp):
        q = q_ref[:, hh * dq:(hh + 1) * dq]
        if nsub == 2:
            lane = lax.broadcasted_iota(jnp.int32, q.shape, 1)
            zero = jnp.zeros_like(q)
            qs.append((hh, jnp.where(lane < 64, q, zero)))
            qs.append((hh, jnp.where(lane >= 64, q, zero)))
        else:
            qs.append((hh, q))
    nch = len(qs)

    def scores_into(slot, rows, nrows):
        for c, (hh, qi) in enumerate(qs):
            s_scr[slot, c, 0:nrows, :] = _dot_nt(k_ref[rows, hh * dq:(hh + 1) * dq], qi)

    def pv_from(slot, vblocks):
        for c, (hh, _) in enumerate(qs):
            pv = None
            for i, blk in enumerate(vblocks):
                part = _dot(vt_ref[blk, hh * HEAD_DIM:(hh + 1) * HEAD_DIM, :],
                            p_scr[slot, c, i * SUB:(i + 1) * SUB, :])
                pv = part if pv is None else pv + part
            acc_scr[c] += pv

    def softmax(slot, nrows, stats):
        ms = [jnp.maximum(m, jnp.max(s_scr[slot, c, 0:nrows, :], axis=0, keepdims=True))
              for c, (m, _) in enumerate(stats)]
        new = []
        for c, (m_new, (m, l)) in enumerate(zip(ms, stats)):
            p = jnp.exp2(s_scr[slot, c, 0:nrows, :] - m_new)
            p_scr[slot, c, 0:nrows, :] = p.astype(BF16)
            alpha = jnp.exp2(m - m_new)
            acc_scr[c] = acc_scr[c] * alpha
            new.append((m_new, alpha * l + jnp.sum(p, axis=0, keepdims=True)))
        return new

    acc_scr[...] = jnp.zeros_like(acc_scr)
    stats = [(jnp.full((1, tq), NEG, F32), jnp.zeros((1, tq), F32)) for _ in qs]
    scores_into(0, slice(0, SUB), SUB)
    stats = softmax(0, SUB, stats)
    pv_from(0, (0,))

    def lat_rows(j):
        return pl.ds(pl.multiple_of(SUB + j * tk, SUB), tk)

    scores_into(0, lat_rows(0), tk)
    p_scr[1] = jnp.zeros(p_scr.shape[1:], BF16)
    n = jnp.where(qt == 0, 0, n_lat_pairs)
    last = 2 * n_lat_pairs - 1

    def body(jj, stats):
        stats = tuple(zip(stats[0::2], stats[1::2]))
        a = 2 * jj
        ap = jnp.maximum(a - 1, 0)
        pv_from(1, (1 + 2 * ap, 2 + 2 * ap))
        scores_into(1, lat_rows(a + 1), tk)
        stats = softmax(0, tk, stats)
        pv_from(0, (1 + 2 * a, 2 + 2 * a))
        scores_into(0, lat_rows(jnp.minimum(a + 2, last)), tk)
        stats = softmax(1, tk, stats)
        return tuple(x for st in stats for x in st)

    flat = lax.fori_loop(0, n, body, tuple(x for st in stats for x in st))
    stats = tuple(zip(flat[0::2], flat[1::2]))
    jl = jnp.maximum(2 * n - 1, 0)
    pv_from(1, (1 + 2 * jl, 2 + 2 * jl))

    for hh in range(hp):
        if nsub == 2:
            o1 = acc_scr[2 * hh] / stats[2 * hh][1]
            o2 = acc_scr[2 * hh + 1] / stats[2 * hh + 1][1]
            o = (o1 - lam_ref[0] * o2).T
            o = o * lax.rsqrt(jnp.mean(o * o, axis=-1, keepdims=True) + RMS_EPS) * g_ref[...] * lam_ref[1]
        else:
            o = (acc_scr[hh] / stats[hh][1]).T
        o_ref[:, hh * HEAD_DIM:(hh + 1) * HEAD_DIM] = o.astype(o_ref.dtype)


def _flash(q_arr, k_arr, vt_arr, *, nb, rb, dq, qcol, kcol, nsub=1, lam=None, norm_g=None):
    rows = q_arr.shape[0]
    nblk_b = rb // SUB
    lat = rb - SUB
    hp = 2
    assert lat % (2 * SUB) == 0 and qcol % (hp * dq) == 0 and kcol % (hp * dq) == 0
    assert lat % (4 * SUB) == 0
    nch = hp * nsub
    kern = functools.partial(_flash_kernel, nsub=nsub, hp=hp, dq=dq, n_lat_pairs=lat // (4 * SUB))
    in_specs = [
        pl.BlockSpec((SUB, hp * dq), lambda b, h, t: (b * nblk_b + t, qcol // (hp * dq) + h)),
        pl.BlockSpec((rb, hp * dq), lambda b, h, t: (b, kcol // (hp * dq) + h)),
        pl.BlockSpec((nblk_b, hp * HEAD_DIM, SUB), lambda b, h, t: (b, h, 0)),
    ]
    args = [q_arr, k_arr, vt_arr]
    if nsub == 2:
        in_specs = [pl.BlockSpec(memory_space=pltpu.SMEM),
                    pl.BlockSpec((1, HEAD_DIM), lambda b, h, t: (0, 0))] + in_specs
        args = [lam, norm_g] + args
    return pl.pallas_call(
        kern,
        out_shape=jax.ShapeDtypeStruct((rows, GROUP_W), BF16),
        grid=(nb, HEADS // hp, nblk_b),
        in_specs=in_specs,
        out_specs=pl.BlockSpec((SUB, hp * HEAD_DIM), lambda b, h, t: (b * nblk_b + t, h)),
        scratch_shapes=[pltpu.VMEM((2, nch, 2 * SUB, SUB), F32), pltpu.VMEM((2, nch, 2 * SUB, SUB), BF16),
                        pltpu.VMEM((nch, HEAD_DIM, SUB), F32)],
        compiler_params=_cparams(("parallel", "parallel", "arbitrary")),
        name="flash_diff" if nsub == 2 else "flash_mla",
    )(*args)


def _na_kernel(q_ref, k_ref, v_ref, bias_ref, o_ref, *, grid_rows, scale):
    qt = pl.program_id(2)
    qb = jnp.maximum(qt - 1, 0)
    u0 = jnp.clip(qb * NA_BLOCK_ROWS - NA_KR // 2, 0, grid_rows - NA_UNION_ROWS)
    start = pl.multiple_of(SUB + u0 * GRID_W, GRID_W)
    nn = NA_UNION_ROWS * GRID_W
    q = (q_ref[...].astype(F32) * scale).astype(BF16)
    kc = k_ref[0:SUB, :]
    vc = v_ref[0:SUB, :]
    kn = k_ref[pl.ds(start, nn), :]
    vn = v_ref[pl.ds(start, nn), :]
    s_c = _dot_nt(q, kc) + bias_ref[0, 0, :, 0:SUB]
    s_n = _dot_nt(q, kn) + bias_ref[0, 0, :, SUB:]
    m = jnp.maximum(jnp.max(s_c, axis=-1, keepdims=True), jnp.max(s_n, axis=-1, keepdims=True))
    e_c = jnp.exp2(s_c - m)
    e_n = jnp.exp2(s_n - m)
    l = jnp.sum(e_c, axis=-1, keepdims=True) + jnp.sum(e_n, axis=-1, keepdims=True)
    o = (_dot(e_c.astype(BF16), vc) + _dot(e_n.astype(BF16), vn)) / l
    o_ref[...] = o.astype(o_ref.dtype)


def _na_bias_table(rpb, grid_rows):
    assert grid_rows >= NA_UNION_ROWS + NA_BLOCK_ROWS
    j = np.arange(NA_BLOCK_ROWS)[:, None]
    iu = np.arange(NA_UNION_ROWS)[None, :]
    cq = np.arange(GRID_W)[:, None]
    ck = np.arange(GRID_W)[None, :]
    cs = np.clip(cq - NA_KC // 2, 0, GRID_W - NA_KC)
    col_ok = (ck >= cs) & (ck < cs + NA_KC)
    dx = np.clip(ck - cq + NA_KC - 1, 0, 2 * NA_KC - 2)
    ex = np.eye(2 * NA_KC - 1, dtype=np.float32)[dx]
    eys, oks = [], []
    for r_off, w_off in ((j, 0 * j), (j + NA_KR // 2, j),
                         (j + NA_UNION_ROWS - NA_BLOCK_ROWS, 0 * j + NA_UNION_ROWS - NA_KR)):
        dy = np.clip(iu - r_off + NA_KR - 1, 0, 2 * NA_KR - 2)
        eys.append(np.eye(2 * NA_KR - 1, dtype=np.float32)[dy])
        oks.append((iu - w_off >= 0) & (iu - w_off < NA_KR))
    ey = np.stack(eys)
    ok = np.stack(oks)[:, :, None, :, None] & col_ok[None, None, :, None, :]
    vals = jnp.einsum("tjia,lhab,qkb->lhtjqik", ey, rpb.astype(F32), ex, precision=lax.Precision.HIGHEST)
    nb_part = jnp.where(ok[None, None], vals * LOG2E, NEG)
    nb_part = jnp.concatenate([nb_part, jnp.full_like(nb_part[:, :, :1], NEG)], axis=2)
    nb_part = nb_part.reshape(rpb.shape[:2] + (4, NA_BLOCK_ROWS * GRID_W, NA_UNION_ROWS * GRID_W))
    ctx_part = jnp.zeros(nb_part.shape[:4] + (SUB,), F32)
    return jnp.concatenate([ctx_part, nb_part], axis=-1)


def _na(pq, bias, *, nb, rb):
    rows = pq.shape[0]
    nblk_b = rb // SUB
    grid_rows = (rb - SUB) // GRID_W
    nqb = nblk_b - 1
    kern = functools.partial(_na_kernel, grid_rows=grid_rows, scale=HEAD_DIM ** -0.5 * LOG2E)

    def bias_map(b, h, t):
        ty = jnp.where(t == 0, 3, jnp.where(t == 1, 0, jnp.where(t == nqb, 2, 1)))
        return (h, ty, 0, 0)

    return pl.pallas_call(
        kern,
        out_shape=jax.ShapeDtypeStruct((rows, GROUP_W), BF16),
        grid=(nb, HEADS, nblk_b),
        in_specs=[
            pl.BlockSpec((SUB, HEAD_DIM), lambda b, h, t: (b * nblk_b + t, C_NQ // HEAD_DIM + h)),
            pl.BlockSpec((rb, HEAD_DIM), lambda b, h, t: (b, C_NK // HEAD_DIM + h)),
            pl.BlockSpec((rb, HEAD_DIM), lambda b, h, t: (b, C_NV // HEAD_DIM + h)),
            pl.BlockSpec((1, 1, SUB, bias.shape[-1]), bias_map),
        ],
        out_specs=pl.BlockSpec((SUB, HEAD_DIM), lambda b, h, t: (b * nblk_b + t, h)),
        compiler_params=_cparams(("parallel", "parallel", "arbitrary")),
        name="na",
    )(pq, pq, pq, bias)


def _gdn_prep_kernel(x_ref, prev_ref, next_ref, w_ref, g_ref, av_ref, dtb_ref, o_ref, og_ref, *, nblk_b):
    i = pl.program_id(0)
    t = i % nblk_b
    keep_prev = jnp.where((t == 0) | (t == 1), 0.0, 1.0)
    keep_next = jnp.where((t == 0) | (t == nblk_b - 1), 0.0, 1.0)
    x = x_ref[...].astype(F32)
    tm = x.shape[0]
    row = lax.broadcasted_iota(jnp.int32, x.shape, 0)
    halo_p = prev_ref[15:16, :].astype(F32) * keep_prev
    halo_n = next_ref[0:1, :].astype(F32) * keep_next
    xp = jnp.where(row == 0, halo_p, pltpu.roll(x, 1, 0))
    xn = jnp.where(row == tm - 1, halo_n, pltpu.roll(x, tm - 1, 0))
    c = xp * w_ref[0:1, :] + x * w_ref[1:2, :] + xn * w_ref[2:3, :]
    s = c * _sigmoid(c)
    for h in range(2 * HEADS):
        sh = s[:, h * 128:(h + 1) * 128]
        nrm = lax.rsqrt(jnp.sum(sh * sh, axis=-1, keepdims=True) + RMS_EPS)
        if h < HEADS:
            nrm = nrm * HEAD_DIM ** -0.5
        o_ref[:, h * 128:(h + 1) * 128] = (sh * nrm).astype(BF16)
    o_ref[:, 2 * GROUP_W:] = s[:, 2 * GROUP_W:].astype(BF16)
    g = g_ref[...]
    z = g + dtb_ref[...]
    softplus = jnp.maximum(z, 0.0) + jnp.log(1.0 + jnp.exp(-jnp.abs(z)))
    lane = lax.broadcasted_iota(jnp.int32, g.shape, 1)
    og_ref[...] = jnp.where(lane < 2 * HEADS, -av_ref[...] * softplus, _sigmoid(g))


def _gdn_prep(pq, pg, conv_w, a_vec, dtb_vec, *, nblk_b):
    rows = pq.shape[0]
    tm = SUB
    w3 = 3 * GROUP_W
    nhalo = rows // 16
    kern = functools.partial(_gdn_prep_kernel, nblk_b=nblk_b)
    return pl.pallas_call(
        kern,
        out_shape=(jax.ShapeDtypeStruct((rows, w3), BF16), jax.ShapeDtypeStruct((rows, 128), F32)),
        grid=(rows // tm,),
        in_specs=[
            pl.BlockSpec((tm, w3), lambda i: (i, 0)),
            pl.BlockSpec((16, w3), lambda i: (jnp.maximum(i * (tm // 16) - 1, 0), 0)),
            pl.BlockSpec((16, w3), lambda i: (jnp.minimum((i + 1) * (tm // 16), nhalo - 1), 0)),
            pl.BlockSpec((3, w3), lambda i: (0, 0)),
            pl.BlockSpec((tm, 128), lambda i: (i, 0)),
            pl.BlockSpec((1, 128), lambda i: (0, 0)),
            pl.BlockSpec((1, 128), lambda i: (0, 0)),
        ],
        out_specs=(pl.BlockSpec((tm, w3), lambda i: (i, 0)), pl.BlockSpec((tm, 128), lambda i: (i, 0))),
        compiler_params=_cparams(("parallel",)),
        name="gdn_prep",
    )(pq, pq, pq, conv_w, pg, a_vec, dtb_vec)


def _split3(x):
    hi = x.astype(BF16)
    r = x - hi.astype(F32)
    mid = r.astype(BF16)
    lo = (r - mid.astype(F32)).astype(BF16)
    return hi, mid, lo


def _gdn_chains(chains):
    c = chains[0][0].shape[0]
    row = lax.broadcasted_iota(jnp.int32, (c, c), 0)
    col = lax.broadcasted_iota(jnp.int32, (c, c), 1)
    xor = row ^ col
    eye = jnp.where(row == col, 1.0, 0.0)
    a_mats, qks, kbs, kfs, egs = [], [], [], [], []
    for q, k, v, gcol, grow, beta, glast, s_prev, lower in chains:
        incl = (row >= col) if lower else (row <= col)
        strict = (row > col) if lower else (row < col)
        decay = jnp.exp(jnp.where(incl, gcol - grow, NEG))
        kf = k.astype(F32)
        kb = kf * beta
        a_mats.append(jnp.where(strict, _dot_nt(kb.astype(BF16), k) * decay, 0.0))
        qks.append(jnp.where(incl, _dot_nt(q, k) * decay, 0.0).astype(BF16))
        kbs.append(kb)
        kfs.append(kf)
        egs.append(jnp.exp(gcol))
    tinvs = [eye - jnp.where(xor < 2, a, 0.0) for a in a_mats]
    s = 2
    while s < c:
        ys = [_dot(jnp.where((xor >= s) & (xor < 2 * s), a, 0.0).astype(BF16), t.astype(BF16))
              for a, t in zip(a_mats, tinvs)]
        tinvs = [t - _dot(t.astype(BF16), y.astype(BF16)) for t, y in zip(tinvs, ys)]
        s *= 2
    wus = []
    for (q, k, v, gcol, grow, beta, glast, s_prev, lower), kb, eg, t in zip(chains, kbs, egs, tinvs):
        rhs = jnp.concatenate([kb * eg, v.astype(F32) * beta], axis=1).astype(BF16)
        wus.append(_dot(t.astype(BF16), rhs))
    sbs = [ch[7].astype(BF16) for ch in chains]
    vnbs = [(wu[:, HEAD_DIM:] - _dot(wu[:, :HEAD_DIM].astype(BF16), sb)).astype(BF16) for wu, sb in zip(wus, sbs)]
    outs = []
    for (q, k, v, gcol, grow, beta, glast, s_prev, lower), kf, eg, qk, sb, vnb in zip(chains, kfs, egs, qks, sbs, vnbs):
        o = _dot((q.astype(F32) * eg).astype(BF16), sb) + _dot(qk, vnb)
        kd_t = (kf * jnp.exp(glast - gcol)).T.astype(BF16)
        s_new = s_prev * jnp.exp(glast) + _dot(kd_t, vnb)
        outs.append((o, s_new))
    return outs


def _gdn_kernel(xf_ref, xb_ref, gf_ref, gb_ref, of_ref, ob_ref, s_scr):
    st = pl.program_id(1)

    @pl.when(st == 0)
    def _():
        s_scr[...] = jnp.zeros_like(s_scr)

    c = GDN_CHUNK
    row = lax.broadcasted_iota(jnp.int32, (c, c), 0)
    col = lax.broadcasted_iota(jnp.int32, (c, c), 1)
    chains = []
    for d, (x_ref, g_ref) in enumerate(((xf_ref, gf_ref), (xb_ref, gb_ref))):
        lower = d == 0
        gates = g_ref[...]
        tri = jnp.where((row >= col) if lower else (row <= col), 1.0, 0.0).astype(BF16)
        hi, mid, lo = _split3(gates)
        gsum = _dot(tri, hi) + _dot(tri, mid) + _dot(tri, lo)
        gsum_t = gsum.T
        for h in range(HEADS):
            gi = d * HEADS + h
            bi = 2 * HEADS + gi
            glast = gsum[c - 1:c, gi:gi + 1] if lower else gsum[0:1, gi:gi + 1]
            chains.append((x_ref[:, h * 128:(h + 1) * 128],
                           x_ref[:, GROUP_W + h * 128:GROUP_W + (h + 1) * 128],
                           x_ref[:, 2 * GROUP_W + h * 128:2 * GROUP_W + (h + 1) * 128],
                           gsum[:, gi:gi + 1], gsum_t[gi:gi + 1, :], gates[:, bi:bi + 1], glast,
                           s_scr[d, h], lower))
    outs = _gdn_chains(chains)
    for idx, (o, s_new) in enumerate(outs):
        d, h = divmod(idx, HEADS)
        s_scr[d, h] = s_new
        (of_ref, ob_ref)[d][:, h * 128:(h + 1) * 128] = o


def _gdn(gx, gg, *, nb, rb):
    rows = gx.shape[0]
    c = GDN_CHUNK
    nch = rb // c
    nctx = SUB // c
    w3 = 3 * GROUP_W

    def fmap(b, s):
        return (b * nch + s, 0)

    def bmap(b, s):
        return (b * nch + jnp.where(s < nctx, nctx - 1 - s, nch + nctx - 1 - s), 0)

    return pl.pallas_call(
        _gdn_kernel,
        out_shape=(jax.ShapeDtypeStruct((rows, GROUP_W), F32), jax.ShapeDtypeStruct((rows, GROUP_W), F32)),
        grid=(nb, nch),
        in_specs=[
            pl.BlockSpec((c, w3), fmap), pl.BlockSpec((c, w3), bmap),
            pl.BlockSpec((c, 128), fmap), pl.BlockSpec((c, 128), bmap),
        ],
        out_specs=(pl.BlockSpec((c, GROUP_W), fmap), pl.BlockSpec((c, GROUP_W), bmap)),
        scratch_shapes=[pltpu.VMEM((2, HEADS, HEAD_DIM, HEAD_DIM), F32)],
        compiler_params=_cparams(("parallel", "arbitrary")),
        name="gdn",
    )(gx, gx, gg, gg)


def _gdn_finish_kernel(of_ref, ob_ref, z_ref, g_ref, o_ref):
    z = z_ref[...].astype(F32)
    gate = z * _sigmoid(z)
    for h in range(HEADS):
        sl = slice(h * 128, (h + 1) * 128)
        o = of_ref[:, sl] + ob_ref[:, sl]
        o = o * lax.rsqrt(jnp.mean(o * o, axis=-1, keepdims=True) + RMS_EPS) * g_ref[...]
        o_ref[:, sl] = (o * gate[:, sl]).astype(BF16)


def _gdn_finish(o_f, o_b, pq, norm_g):
    rows = o_f.shape[0]
    tm = 2 * SUB
    return pl.pallas_call(
        _gdn_finish_kernel,
        out_shape=jax.ShapeDtypeStruct((rows, GROUP_W), BF16),
        grid=(rows // tm,),
        in_specs=[
            pl.BlockSpec((tm, GROUP_W), lambda i: (i, 0)),
            pl.BlockSpec((tm, GROUP_W), lambda i: (i, 0)),
            pl.BlockSpec((tm, GROUP_W), lambda i: (i, C_GZ // GROUP_W)),
            pl.BlockSpec((1, HEAD_DIM), lambda i: (0, 0)),
        ],
        out_specs=pl.BlockSpec((tm, GROUP_W), lambda i: (i, 0)),
        compiler_params=_cparams(("parallel",)),
        name="gdn_finish",
    )(o_f, o_b, pq, norm_g)


def _layernorm_rows(z, g, b):
    mu = jnp.mean(z, axis=-1, keepdims=True)
    zc = z - mu
    var = jnp.mean(zc * zc, axis=-1, keepdims=True)
    return zc * lax.rsqrt(var + LN_EPS) * g + b


def _outproj_kernel(ya_ref, yb_ref, ym_ref, yn_ref, w_ref, x_ref, mod_ref, g_ref, b_ref, o_ref,
                    *, nsub, nblk_b, nb, d, alpha):
    i = pl.program_id(0)
    y = _dot(ya_ref[...], w_ref[0:GROUP_W, :])
    y = y + _dot(yb_ref[...], w_ref[GROUP_W:2 * GROUP_W, :])
    y = y + _dot(ym_ref[...], w_ref[2 * GROUP_W:3 * GROUP_W, :])
    y = y + _dot(yn_ref[...], w_ref[3 * GROUP_W:4 * GROUP_W, :])
    for s in range(nsub):
        g, _ = _group_of(i * nsub + s, nblk_b, nb)
        gate = mod_ref[0, pl.ds(g, 1), 2 * d:3 * d]
        sl = slice(s * SUB, (s + 1) * SUB)
        z = alpha * x_ref[sl, :] + gate * y[sl, :]
        o_ref[sl, :] = _layernorm_rows(z, g_ref[...], b_ref[...])


def _outproj(ys, w_out, xu, mod_l, ln_g, ln_b, *, nblk_b, nb, alpha):
    rows, d = xu.shape
    tm = 2 * SUB
    kern = functools.partial(_outproj_kernel, nsub=tm // SUB, nblk_b=nblk_b, nb=nb, d=d, alpha=alpha)
    yspec = pl.BlockSpec((tm, GROUP_W), lambda i: (i, 0))
    return pl.pallas_call(
        kern,
        out_shape=jax.ShapeDtypeStruct((rows, d), F32),
        grid=(rows // tm,),
        in_specs=[yspec, yspec, yspec, yspec,
                  pl.BlockSpec((4 * GROUP_W, d), lambda i: (0, 0)),
                  pl.BlockSpec((tm, d), lambda i: (i, 0)),
                  pl.BlockSpec((1, MOD_ROWS, 6 * d), lambda i: (0, 0, 0)),
                  pl.BlockSpec((1, d), lambda i: (0, 0)),
                  pl.BlockSpec((1, d), lambda i: (0, 0))],
        out_specs=pl.BlockSpec((tm, d), lambda i: (i, 0)),
        compiler_params=_cparams(("parallel",)),
        name="outproj",
    )(*ys, w_out, xu, mod_l, ln_g, ln_b)


def _ffn_kernel(x_ref, prev_ref, next_ref, mod_ref, wg_ref, wv_ref, cg_ref, cv_ref, wd_ref, g_ref, b_ref,
                o_ref, h_scr, acc_scr, *, nsub, nblk_b, nb, d, nj, alpha):
    i = pl.program_id(0)
    j = pl.program_id(1)
    tm = nsub * SUB

    @pl.when(j == 0)
    def _():
        acc_scr[...] = jnp.zeros_like(acc_scr)
        for s in range(nsub):
            g, _ = _group_of(i * nsub + s, nblk_b, nb)
            shift = mod_ref[0, pl.ds(g, 1), 3 * d:4 * d]
            scale = mod_ref[0, pl.ds(g, 1), 4 * d:5 * d]
            h_scr[8 + s * SUB:8 + (s + 1) * SUB, :] = x_ref[s * SUB:(s + 1) * SUB, :] * (1.0 + scale) + shift
            if s == 0:
                h_scr[0:8, :] = prev_ref[...] * (1.0 + scale) + shift
            if s == nsub - 1:
                h_scr[tm + 8:tm + 16, :] = next_ref[...] * (1.0 + scale) + shift

    row = lax.broadcasted_iota(jnp.int32, (tm, 1), 0)
    keep_prev = jnp.ones((tm, 1), F32)
    keep_next = jnp.ones((tm, 1), F32)
    for s in range(nsub):
        _, t = _group_of(i * nsub + s, nblk_b, nb)
        first = (t == 0) | (t == 1)
        last = (t == 0) | (t == nblk_b - 1)
        keep_prev = jnp.where((row == s * SUB) & first, 0.0, keep_prev)
        keep_next = jnp.where((row == s * SUB + SUB - 1) & last, 0.0, keep_next)

    hb = h_scr[...].astype(BF16)
    ext = tm + 16

    def conv(u, cw_ref):
        up = pltpu.roll(u, 1, 0)[8:tm + 8, :] * keep_prev
        un = pltpu.roll(u, ext - 1, 0)[8:tm + 8, :] * keep_next
        return up * cw_ref[0:1, :] + u[8:tm + 8, :] * cw_ref[1:2, :] + un * cw_ref[2:3, :]

    gate = conv(_dot(hb, wg_ref[...]), cg_ref)
    val = conv(_dot(hb, wv_ref[...]), cv_ref)
    act = (gate * _sigmoid(gate) * val).astype(BF16)
    acc_scr[...] += _dot(act, wd_ref[...])

    @pl.when(j == nj - 1)
    def _():
        for s in range(nsub):
            g, _ = _group_of(i * nsub + s, nblk_b, nb)
            gt = mod_ref[0, pl.ds(g, 1), 5 * d:6 * d]
            sl = slice(s * SUB, (s + 1) * SUB)
            z = alpha * x_ref[sl, :] + gt * acc_scr[sl, :]
            o_ref[sl, :] = _layernorm_rows(z, g_ref[...], b_ref[...])


def _ffn(xu, mod_l, w_up, conv_w, w_down, ln_g, ln_b, *, nblk_b, nb, alpha):
    rows, d = xu.shape
    dff = w_down.shape[0]
    tm, tc = 2 * SUB, 512
    nj = dff // tc
    nhalo = rows // 8
    kern = functools.partial(_ffn_kernel, nsub=tm // SUB, nblk_b=nblk_b, nb=nb, d=d, nj=nj, alpha=alpha)
    return pl.pallas_call(
        kern,
        out_shape=jax.ShapeDtypeStruct((rows, d), F32),
        grid=(rows // tm, nj),
        in_specs=[
            pl.BlockSpec((tm, d), lambda i, j: (i, 0)),
            pl.BlockSpec((8, d), lambda i, j: (jnp.maximum(i * (tm // 8) - 1, 0), 0)),
            pl.BlockSpec((8, d), lambda i, j: (jnp.minimum((i + 1) * (tm // 8), nhalo - 1), 0)),
            pl.BlockSpec((1, MOD_ROWS, 6 * d), lambda i, j: (0, 0, 0)),
            pl.BlockSpec((d, tc), lambda i, j: (0, j)),
            pl.BlockSpec((d, tc), lambda i, j: (0, nj + j)),
            pl.BlockSpec((3, tc), lambda i, j: (0, j)),
            pl.BlockSpec((3, tc), lambda i, j: (0, nj + j)),
            pl.BlockSpec((tc, d), lambda i, j: (j, 0)),
            pl.BlockSpec((1, d), lambda i, j: (0, 0)),
            pl.BlockSpec((1, d), lambda i, j: (0, 0)),
        ],
        out_specs=pl.BlockSpec((tm, d), lambda i, j: (i, 0)),
        scratch_shapes=[pltpu.VMEM((tm + 16, d), F32), pltpu.VMEM((tm, d), F32)],
        compiler_params=_cparams(("parallel", "arbitrary")),
        name="ffn",
    )(xu, xu, xu, mod_l, w_up, w_up, conv_w, conv_w, w_down, ln_g, ln_b)


def _rope_tables(n_lat):
    pos = jnp.arange(n_lat)
    row = (pos // GRID_W).astype(F32)
    col = (pos % GRID_W).astype(F32)
    half = MLA_ROPE // 2
    inv = ROPE_THETA ** (-jnp.arange(0, half, 2, dtype=F32) / half)
    ar = row[:, None] * inv[None, :]
    ac = col[:, None] * inv[None, :]
    cos64 = jnp.concatenate([jnp.cos(ar), jnp.cos(ar), jnp.cos(ac), jnp.cos(ac)], axis=-1)
    sin64 = jnp.concatenate([-jnp.sin(ar), jnp.sin(ar), -jnp.sin(ac), jnp.sin(ac)], axis=-1)
    one = jnp.ones((n_lat, 64), F32)
    zero = jnp.zeros((n_lat, 64), F32)

    def with_ctx(c, s):
        c = jnp.concatenate([jnp.ones((SUB, 128), F32), c], axis=0)
        s = jnp.concatenate([jnp.zeros((SUB, 128), F32), s], axis=0)
        return c, s

    diff = with_ctx(jnp.concatenate([cos64, cos64], -1), jnp.concatenate([sin64, sin64], -1))
    mla = with_ctx(jnp.concatenate([cos64, one], -1), jnp.concatenate([sin64, zero], -1))
    return diff, mla


def _prep_w_in(w_in):
    depth, d, _ = w_in.shape
    z = lambda n: jnp.zeros((depth, d, n), w_in.dtype)
    o_diff = 4 * GROUP_W + 4 * HEADS
    o_mla = o_diff + 3 * GROUP_W
    o_na = o_mla + MLA_Q_RANK + MLA_KV_RANK + MLA_ROPE
    w = jnp.concatenate([w_in[..., :4 * GROUP_W], w_in[..., o_diff:o_mla], w_in[..., o_na:],
                         w_in[..., o_mla:o_na], z(MLA_W - (o_na - o_mla)),
                         w_in[..., 4 * GROUP_W:o_diff], z(128 - 4 * HEADS)], axis=-1)
    assert w.shape[-1] == N_PROJ
    return w.astype(BF16)


def _prep_mla_weights(w_uq, w_ukv):
    depth = w_uq.shape[0]
    dq = MLA_NOPE + MLA_ROPE
    wq = w_uq.reshape(depth, MLA_Q_RANK, HEADS, dq)
    wq = jnp.concatenate([wq, jnp.zeros((depth, MLA_Q_RANK, HEADS, 256 - dq), w_uq.dtype)], axis=-1)
    wq = wq.reshape(depth, MLA_Q_RANK, HEADS * 256)
    wkv = w_ukv.reshape(depth, MLA_KV_RANK, HEADS, MLA_NOPE + HEAD_DIM)
    wkv = jnp.concatenate([wkv[..., :MLA_NOPE].reshape(depth, MLA_KV_RANK, GROUP_W),
                           wkv[..., MLA_NOPE:].reshape(depth, MLA_KV_RANK, GROUP_W)], axis=-1)
    return wq.astype(BF16), wkv.astype(BF16)


def _pad_lanes(v, n=128):
    return jnp.concatenate([v, jnp.zeros(v.shape[:-1] + (n - v.shape[-1],), v.dtype)], axis=-1)


def _mixers(pq, pg, lw, l, tabs, *, nb, rb):
    nblk_b = rb // SUB
    (dcos, dsin), (mcos, msin) = tabs
    gx, gg = _gdn_prep(pq, pg, lw["gdn_conv"], lw["gdn_a"], lw["gdn_dtb"], nblk_b=nblk_b)
    o_f, o_b = _gdn(gx, gg, nb=nb, rb=rb)
    ya = _gdn_finish(o_f, o_b, pq, lw["gdn_norm_g"])
    qk_d, vt_d = _diff_prep(pq, dcos, dsin, nblk_b=nblk_b)
    yb = _flash(qk_d, qk_d, vt_d, nb=nb, rb=rb, dq=HEAD_DIM, qcol=0, kcol=GROUP_W, nsub=2,
                lam=lw["diff_lam"], norm_g=lw["diff_norm_g"])
    qm, km, vm = _mla_prep(pq, lw["mla_gq"], lw["mla_gkv"], lw["mla_wq"], lw["mla_wkv"], mcos, msin,
                           nblk_b=nblk_b)
    ym = _flash(qm, km, vm, nb=nb, rb=rb, dq=256, qcol=0, kcol=0)
    yn = _na(pq, lw["na_bias"], nb=nb, rb=rb)
    return ya, yb, ym, yn


def _layer(xu, lw, l, tabs, *, nb, rb, alpha):
    nblk_b = rb // SUB
    mod_l = lw["mod"]
    pq, pg = _inproj(xu, mod_l, lw["w_in"], nblk_b=nblk_b, nb=nb)
    ys = _mixers(pq, pg, lw, l, tabs, nb=nb, rb=rb)
    x1 = _outproj(ys, lw["w_out"], xu, mod_l, lw["ln_g0"], lw["ln_b0"], nblk_b=nblk_b, nb=nb, alpha=alpha)
    return _ffn(x1, mod_l, lw["ffn_w_up"], lw["ffn_conv"], lw["ffn_w_down"], lw["ln_g1"], lw["ln_b1"],
                nblk_b=nblk_b, nb=nb, alpha=alpha)


def _layer_weights(l, mod, w_in_p, gdn_conv, gdn_a_log, gdn_dt_bias, gdn_norm_g, diff_lambda, diff_norm_g,
                   mla_q_norm_g, mla_kv_norm_g, wq, wkv, na_bias, w_out_b, ln_g, ln_b, w_up_b, ffn_conv,
                   w_down_b):
    lf = diff_lambda[l].astype(F32)
    lam_init = 0.8 - 0.6 * math.exp(-0.3 * l)
    lam = jnp.exp(jnp.sum(lf[0] * lf[1])) - jnp.exp(jnp.sum(lf[2] * lf[3])) + lam_init
    return {
        "mod": mod[l:l + 1],
        "w_in": w_in_p[l],
        "gdn_conv": gdn_conv[l],
        "gdn_a": _pad_lanes(jnp.exp(gdn_a_log[l].astype(F32)).reshape(1, 2 * HEADS)),
        "gdn_dtb": _pad_lanes(gdn_dt_bias[l].astype(F32).reshape(1, 2 * HEADS)),
        "gdn_norm_g": gdn_norm_g[l].reshape(1, HEAD_DIM),
        "diff_lam": jnp.stack([lam, jnp.asarray(1.0 - lam_init, F32)]),
        "diff_norm_g": diff_norm_g[l].reshape(1, HEAD_DIM),
        "mla_gq": mla_q_norm_g[l].reshape(1, MLA_Q_RANK),
        "mla_gkv": mla_kv_norm_g[l].reshape(1, MLA_KV_RANK),
        "mla_wq": wq[l],
        "mla_wkv": wkv[l],
        "na_bias": na_bias[l],
        "w_out": w_out_b[l],
        "ln_g0": ln_g[l, 0:1], "ln_b0": ln_b[l, 0:1], "ln_g1": ln_g[l, 1:2], "ln_b1": ln_b[l, 1:2],
        "ffn_w_up": w_up_b[l],
        "ffn_conv": ffn_conv[l],
        "ffn_w_down": w_down_b[l],
    }


def kernel(x, c, ctx, c_ctx, w_mod, b_mod, w_in, gdn_conv, gdn_a_log, gdn_dt_bias, gdn_norm_g, diff_lambda,
           diff_norm_g, mla_q_norm_g, mla_kv_norm_g, mla_w_uq, mla_w_ukv, na_rpb, w_out, ln_g, ln_b,
           ffn_w_up, ffn_conv, ffn_w_down):
    nb, n_lat, d = x.shape
    depth = w_mod.shape[0]
    assert ctx.shape[1] == SUB and n_lat % (2 * SUB) == 0 and n_lat % GRID_W == 0
    assert (nb * (SUB + n_lat)) % (2 * SUB) == 0 and nb + 1 <= MOD_ROWS
    rb = SUB + n_lat
    alpha = (2 * depth) ** 0.25

    cvec = jnp.concatenate([c, c_ctx[None, :], jnp.zeros((MOD_ROWS - nb - 1, d), c.dtype)], axis=0)
    mod = _modulation(cvec, w_mod, b_mod)
    tabs = _rope_tables(n_lat)
    w_in_p = _prep_w_in(w_in)
    wq, wkv = _prep_mla_weights(mla_w_uq, mla_w_ukv)
    na_bias = _na_bias_table(na_rpb, n_lat // GRID_W)
    w_out_b = w_out.astype(BF16)
    w_up_b = ffn_w_up.astype(BF16)
    w_down_b = ffn_w_down.astype(BF16)

    xu = jnp.concatenate([ctx, x], axis=1).reshape(nb * rb, d)
    for l in range(depth):
        lw = _layer_weights(l, mod, w_in_p, gdn_conv, gdn_a_log, gdn_dt_bias, gdn_norm_g, diff_lambda,
                            diff_norm_g, mla_q_norm_g, mla_kv_norm_g, wq, wkv, na_bias, w_out_b, ln_g, ln_b,
                            w_up_b, ffn_conv, w_down_b)
        xu = _layer(xu, lw, l, tabs, nb=nb, rb=rb, alpha=alpha)
    return xu.reshape(nb, rb, d)[:, SUB:, :]
```

```python
import functools
import math

import numpy as np
import jax
import jax.numpy as jnp
from jax import lax
from jax.experimental import pallas as pl
from jax.experimental.pallas import tpu as pltpu

F32 = jnp.float32
BF16 = jnp.bfloat16

HEADS = 4
HEAD_DIM = 128
GROUP_W = HEADS * HEAD_DIM
GRID_W = 64
NA_KR = 8
NA_KC = 16
NA_BLOCK_ROWS = 4
NA_UNION_ROWS = NA_BLOCK_ROWS + NA_KR - 1
MLA_Q_RANK = 384
MLA_KV_RANK = 128
MLA_NOPE = 128
MLA_ROPE = 64
ROPE_THETA = 10000.0
LN_EPS = 1e-5
RMS_EPS = 1e-6
NEG = -1e30
LOG2E = math.log2(math.e)

SUB = 256
GDN_CHUNK = 128
MOD_ROWS = 16
VMEM_LIMIT = 56 * 1024 * 1024

C_GQ, C_GK, C_GV, C_GZ = 0, 512, 1024, 1536
C_DQ, C_DK, C_DV = 2048, 2560, 3072
C_NQ, C_NK, C_NV = 3584, 4096, 4608
C_MLA = 5120
MLA_W = 640
C_AB = 5760
N_PROJ = 5888


def _cparams(sem):
    return pltpu.CompilerParams(dimension_semantics=sem, vmem_limit_bytes=VMEM_LIMIT)


def _sigmoid(x):
    return 1.0 / (1.0 + jnp.exp(-x))


def _dot(a, b):
    return jnp.dot(a, b, preferred_element_type=F32)


def _dot_nt(a, b):
    return lax.dot_general(a, b, (((1,), (1,)), ((), ())), preferred_element_type=F32)


def _group_of(blk, nblk_b, nb):
    t = blk % nblk_b
    return jnp.where(t == 0, nb, blk // nblk_b), t


def _mod_kernel(c_ref, w_ref, b_ref, o_ref):
    c = c_ref[...]
    s = (c * _sigmoid(c)).astype(BF16)
    o_ref[0] = _dot(s, w_ref[0].astype(BF16)) + b_ref[0]


def _modulation(cvec, w_mod, b_mod):
    depth, d, n = w_mod.shape
    tn = 1536
    return pl.pallas_call(
        _mod_kernel,
        out_shape=jax.ShapeDtypeStruct((depth, MOD_ROWS, n), F32),
        grid=(depth, n // tn),
        in_specs=[
            pl.BlockSpec((MOD_ROWS, d), lambda l, j: (0, 0)),
            pl.BlockSpec((1, d, tn), lambda l, j: (l, 0, j)),
            pl.BlockSpec((1, 1, tn), lambda l, j: (l, 0, j)),
        ],
        out_specs=pl.BlockSpec((1, MOD_ROWS, tn), lambda l, j: (l, 0, j)),
        compiler_params=_cparams(("parallel", "parallel")),
        name="modulation",
    )(cvec, w_mod, b_mod.reshape(depth, 1, n))


def _inproj_kernel(x_ref, mod_ref, w_ref, o_ref, og_ref, h_scr, *, nsub, nblk_b, nb, d, nj):
    i = pl.program_id(0)
    j = pl.program_id(1)

    @pl.when(j == 0)
    def _():
        for s in range(nsub):
            g, _ = _group_of(i * nsub + s, nblk_b, nb)
            shift = mod_ref[0, pl.ds(g, 1), 0:d]
            scale = mod_ref[0, pl.ds(g, 1), d:2 * d]
            xs = x_ref[s * SUB:(s + 1) * SUB, :]
            h_scr[s * SUB:(s + 1) * SUB, :] = (xs * (1.0 + scale) + shift).astype(BF16)

    acc = _dot(h_scr[...], w_ref[...])
    o_ref[...] = acc.astype(BF16)

    @pl.when(j == nj - 1)
    def _():
        og_ref[...] = acc[:, acc.shape[1] - 128:]


def _inproj(xu, mod_l, w_in_p, *, nblk_b, nb):
    rows, d = xu.shape
    n = w_in_p.shape[1]
    tm, tn = 2 * SUB, n // 2
    nj = n // tn
    kern = functools.partial(_inproj_kernel, nsub=tm // SUB, nblk_b=nblk_b, nb=nb, d=d, nj=nj)
    return pl.pallas_call(
        kern,
        out_shape=(jax.ShapeDtypeStruct((rows, n), BF16), jax.ShapeDtypeStruct((rows, 128), F32)),
        grid=(rows // tm, nj),
        in_specs=[
            pl.BlockSpec((tm, d), lambda i, j: (i, 0)),
            pl.BlockSpec((1, MOD_ROWS, 6 * d), lambda i, j: (0, 0, 0)),
            pl.BlockSpec((d, tn), lambda i, j: (0, j)),
        ],
        out_specs=(pl.BlockSpec((tm, tn), lambda i, j: (i, j)),
                   pl.BlockSpec((tm, 128), lambda i, j: (i, 0))),
        scratch_shapes=[pltpu.VMEM((tm, d), BF16)],
        compiler_params=_cparams(("parallel", "arbitrary")),
        name="inproj",
    )(xu, mod_l, w_in_p)


def _rope_slab(x, cos, sin):
    lane = lax.broadcasted_iota(jnp.int32, x.shape, 1)
    odd = (lane & 16) != 0
    partner = jnp.where(odd, pltpu.roll(x, 16, 1), pltpu.roll(x, 128 - 16, 1))
    return x * cos + partner * sin


def _diff_prep_kernel(p_ref, v_ref, cos_ref, sin_ref, o_ref, vt_ref, *, scale):
    vt_ref[0] = v_ref[...].astype(F32).T.astype(BF16)
    cos = cos_ref[...]
    sin = sin_ref[...]
    for s in range(2 * HEADS):
        x = p_ref[:, s * 128:(s + 1) * 128].astype(F32)
        r = _rope_slab(x, cos, sin)
        if s < HEADS:
            r = r * scale
        o_ref[:, s * 128:(s + 1) * 128] = r.astype(BF16)


def _diff_prep(pq, cos, sin, *, nblk_b):
    rows = pq.shape[0]
    tm = SUB
    kern = functools.partial(_diff_prep_kernel, scale=(HEAD_DIM // 2) ** -0.5 * LOG2E)
    return pl.pallas_call(
        kern,
        out_shape=(jax.ShapeDtypeStruct((rows, 2 * GROUP_W), BF16),
                   jax.ShapeDtypeStruct((rows // tm, GROUP_W, tm), BF16)),
        grid=(rows // tm,),
        in_specs=[
            pl.BlockSpec((tm, 2 * GROUP_W), lambda i: (i, C_DQ // (2 * GROUP_W))),
            pl.BlockSpec((tm, GROUP_W), lambda i: (i, C_DV // GROUP_W)),
            pl.BlockSpec((tm, 128), lambda i: (i % nblk_b, 0)),
            pl.BlockSpec((tm, 128), lambda i: (i % nblk_b, 0)),
        ],
        out_specs=(pl.BlockSpec((tm, 2 * GROUP_W), lambda i: (i, 0)),
                   pl.BlockSpec((1, GROUP_W, tm), lambda i: (i, 0, 0))),
        compiler_params=_cparams(("parallel",)),
        name="diff_prep",
    )(pq, pq, cos, sin)


def _mla_prep_kernel(p_ref, gq_ref, gkv_ref, wq_ref, wkv_ref, cos_ref, sin_ref,
                     q_ref, k_ref, v_ref, *, scale):
    cos = cos_ref[...]
    sin = sin_ref[...]
    cq = p_ref[:, 0:MLA_Q_RANK].astype(F32)
    cq = cq * lax.rsqrt(jnp.mean(cq * cq, axis=-1, keepdims=True) + RMS_EPS) * gq_ref[...]
    q = _dot(cq.astype(BF16), wq_ref[...])
    ckv = p_ref[:, MLA_Q_RANK:MLA_Q_RANK + MLA_KV_RANK].astype(F32)
    ckv = ckv * lax.rsqrt(jnp.mean(ckv * ckv, axis=-1, keepdims=True) + RMS_EPS) * gkv_ref[...]
    kv = _dot(ckv.astype(BF16), wkv_ref[...])
    kr = _rope_slab(p_ref[:, MLA_Q_RANK + MLA_KV_RANK:MLA_W].astype(F32), cos, sin).astype(BF16)
    for h in range(HEADS):
        q_ref[:, h * 256:h * 256 + 128] = (q[:, h * 256:h * 256 + 128] * scale).astype(BF16)
        qr = _rope_slab(q[:, h * 256 + 128:(h + 1) * 256], cos, sin)
        q_ref[:, h * 256 + 128:(h + 1) * 256] = (qr * scale).astype(BF16)
        k_ref[:, h * 256:h * 256 + 128] = kv[:, h * 128:(h + 1) * 128].astype(BF16)
        k_ref[:, h * 256 + 128:(h + 1) * 256] = kr
    v_ref[0] = kv[:, GROUP_W:].T.astype(BF16)


def _mla_prep(pq, gq, gkv, wq, wkv, cos, sin, *, nblk_b):
    rows = pq.shape[0]
    tm = SUB
    kern = functools.partial(_mla_prep_kernel, scale=(MLA_NOPE + MLA_ROPE) ** -0.5 * LOG2E)
    return pl.pallas_call(
        kern,
        out_shape=(jax.ShapeDtypeStruct((rows, HEADS * 256), BF16),
                   jax.ShapeDtypeStruct((rows, HEADS * 256), BF16),
                   jax.ShapeDtypeStruct((rows // tm, GROUP_W, tm), BF16)),
        grid=(rows // tm,),
        in_specs=[
            pl.BlockSpec((tm, MLA_W), lambda i: (i, C_MLA // MLA_W)),
            pl.BlockSpec((1, MLA_Q_RANK), lambda i: (0, 0)),
            pl.BlockSpec((1, MLA_KV_RANK), lambda i: (0, 0)),
            pl.BlockSpec((MLA_Q_RANK, HEADS * 256), lambda i: (0, 0)),
            pl.BlockSpec((MLA_KV_RANK, 2 * GROUP_W), lambda i: (0, 0)),
            pl.BlockSpec((tm, 128), lambda i: (i % nblk_b, 0)),
            pl.BlockSpec((tm, 128), lambda i: (i % nblk_b, 0)),
        ],
        out_specs=(pl.BlockSpec((tm, HEADS * 256), lambda i: (i, 0)),
                   pl.BlockSpec((tm, HEADS * 256), lambda i: (i, 0)),
                   pl.BlockSpec((1, GROUP_W, tm), lambda i: (i, 0, 0))),
        compiler_params=_cparams(("parallel",)),
        name="mla_prep",
    )(pq, gq, gkv, wq, wkv, cos, sin)


def _flash_kernel(*refs, nsub, hp, dq, n_lat_pairs):
    if nsub == 2:
        lam_ref, g_ref, q_ref, k_ref, vt_ref, o_ref, s_scr, p_scr, acc_scr = refs
    else:
        q_ref, k_ref, vt_ref, o_ref, s_scr, p_scr, acc_scr = refs
    qt = pl.program_id(2)
    tq = q_ref.shape[0]
    tk = 2 * SUB
    qs = []
    for hh in range(hp):
        q = q_ref[:, hh * dq:(hh + 1) * dq]
        if nsub == 2:
            lane = lax.broadcasted_iota(jnp.int32, q.shape, 1)
            zero = jnp.zeros_like(q)
            qs.append((hh, jnp.where(lane < 64, q, zero)))
            qs.append((hh, jnp.where(lane >= 64, q, zero)))
        else:
            qs.append((hh, q))
    nch = len(qs)

    def scores_into(slot, rows, nrows):
        for c, (hh, qi) in enumerate(qs):
            s_scr[slot, c, 0:nrows, :] = _dot_nt(k_ref[rows, hh * dq:(hh + 1) * dq], qi)

    def pv_from(slot, vblocks):
        for c, (hh, _) in enumerate(qs):
            pv = None
            for i, blk in enumerate(vblocks):
                part = _dot(vt_ref[blk, hh * HEAD_DIM:(hh + 1) * HEAD_DIM, :],
                            p_scr[slot, c, i * SUB:(i + 1) * SUB, :])
                pv = part if pv is None else pv + part
            acc_scr[c] += pv

    def softmax(slot, nrows, stats):
        ch = 64
        new = []
        for c, (m, l) in enumerate(stats):
            mx = s_scr[slot, c, 0:ch, :]
            for r in range(ch, nrows, ch):
                mx = jnp.maximum(mx, s_scr[slot, c, r:r + ch, :])
            m_new = jnp.maximum(m, jnp.max(mx, axis=0, keepdims=True))
            psum = None
            for r in range(0, nrows, ch):
                p = jnp.exp2(s_scr[slot, c, r:r + ch, :] - m_new)
                p_scr[slot, c, r:r + ch, :] = p.astype(BF16)
                psum = p if psum is None else psum + p
            alpha = jnp.exp2(m - m_new)
            acc_scr[c] = acc_scr[c] * alpha
            new.append((m_new, alpha * l + jnp.sum(psum, axis=0, keepdims=True)))
        return new

    def lat_rows(j):
        return pl.ds(pl.multiple_of(SUB + j * tk, SUB), tk)

    acc_scr[...] = jnp.zeros_like(acc_scr)
    stats = [(jnp.full((1, tq), NEG, F32), jnp.zeros((1, tq), F32)) for _ in qs]
    scores_into(1, slice(0, SUB), SUB)
    scores_into(0, lat_rows(0), tk)
    p_scr[1, :, SUB:, :] = jnp.zeros((nch, SUB, tq), BF16)
    stats = softmax(1, SUB, stats)
    unroll = 2 if n_lat_pairs % 2 == 0 else 1
    n = jnp.where(qt == 0, 0, n_lat_pairs // unroll)
    last = 2 * n_lat_pairs - 1

    def pair(a, stats):
        bp = jnp.where(a == 0, 0, 2 * a - 1)
        pv_from(1, (bp, bp + 1))
        scores_into(1, lat_rows(a + 1), tk)
        stats = softmax(0, tk, stats)
        pv_from(0, (1 + 2 * a, 2 + 2 * a))
        scores_into(0, lat_rows(jnp.minimum(a + 2, last)), tk)
        return softmax(1, tk, stats)

    def body(jj, stats):
        stats = tuple(zip(stats[0::2], stats[1::2]))
        for u in range(unroll):
            stats = pair(2 * (jj * unroll + u), stats)
        return tuple(x for st in stats for x in st)

    flat = lax.fori_loop(0, n, body, tuple(x for st in stats for x in st))
    stats = tuple(zip(flat[0::2], flat[1::2]))
    done = 2 * unroll * n
    bl = jnp.where(done == 0, 0, 2 * done - 1)
    pv_from(1, (bl, bl + 1))

    for hh in range(hp):
        if nsub == 2:
            o1 = acc_scr[2 * hh] / stats[2 * hh][1]
            o2 = acc_scr[2 * hh + 1] / stats[2 * hh + 1][1]
            o = (o1 - lam_ref[0] * o2).T
            o = o * lax.rsqrt(jnp.mean(o * o, axis=-1, keepdims=True) + RMS_EPS) * g_ref[...] * lam_ref[1]
        else:
            o = (acc_scr[hh] / stats[hh][1]).T
        o_ref[:, hh * HEAD_DIM:(hh + 1) * HEAD_DIM] = o.astype(o_ref.dtype)


def _flash(q_arr, k_arr, vt_arr, *, nb, rb, dq, qcol, kcol, nsub=1, lam=None, norm_g=None):
    rows = q_arr.shape[0]
    nblk_b = rb // SUB
    lat = rb - SUB
    hp = 2
    assert lat % (2 * SUB) == 0 and qcol % (hp * dq) == 0 and kcol % (hp * dq) == 0
    assert lat % (4 * SUB) == 0
    nch = hp * nsub
    kern = functools.partial(_flash_kernel, nsub=nsub, hp=hp, dq=dq, n_lat_pairs=lat // (4 * SUB))
    in_specs = [
        pl.BlockSpec((SUB, hp * dq), lambda b, h, t: (b * nblk_b + t, qcol // (hp * dq) + h)),
        pl.BlockSpec((rb, hp * dq), lambda b, h, t: (b, kcol // (hp * dq) + h)),
        pl.BlockSpec((nblk_b, hp * HEAD_DIM, SUB), lambda b, h, t: (b, h, 0)),
    ]
    args = [q_arr, k_arr, vt_arr]
    if nsub == 2:
        in_specs = [pl.BlockSpec(memory_space=pltpu.SMEM),
                    pl.BlockSpec((1, HEAD_DIM), lambda b, h, t: (0, 0))] + in_specs
        args = [lam, norm_g] + args
    return pl.pallas_call(
        kern,
        out_shape=jax.ShapeDtypeStruct((rows, GROUP_W), BF16),
        grid=(nb, HEADS // hp, nblk_b),
        in_specs=in_specs,
        out_specs=pl.BlockSpec((SUB, hp * HEAD_DIM), lambda b, h, t: (b * nblk_b + t, h)),
        scratch_shapes=[pltpu.VMEM((2, nch, 2 * SUB, SUB), F32), pltpu.VMEM((2, nch, 2 * SUB, SUB), BF16),
                        pltpu.VMEM((nch, HEAD_DIM, SUB), F32)],
        compiler_params=_cparams(("parallel", "parallel", "arbitrary")),
        name="flash_diff" if nsub == 2 else "flash_mla",
    )(*args)


def _na_kernel(q_ref, k_ref, v_ref, bias_ref, o_ref, *, grid_rows, scale):
    qt = pl.program_id(2)
    qb = jnp.maximum(qt - 1, 0)
    u0 = jnp.clip(qb * NA_BLOCK_ROWS - NA_KR // 2, 0, grid_rows - NA_UNION_ROWS)
    start = pl.multiple_of(SUB + u0 * GRID_W, GRID_W)
    nn = NA_UNION_ROWS * GRID_W
    q = (q_ref[...].astype(F32) * scale).astype(BF16)
    kc = k_ref[0:SUB, :]
    vc = v_ref[0:SUB, :]
    kn = k_ref[pl.ds(start, nn), :]
    vn = v_ref[pl.ds(start, nn), :]
    s_c = _dot_nt(q, kc)
    s_n = _dot_nt(q, kn) + bias_ref[0, 0]
    m = jnp.maximum(jnp.max(s_c, axis=-1, keepdims=True), jnp.max(s_n, axis=-1, keepdims=True))
    e_c = jnp.exp2(s_c - m)
    e_n = jnp.exp2(s_n - m)
    l = jnp.sum(e_c, axis=-1, keepdims=True) + jnp.sum(e_n, axis=-1, keepdims=True)
    o = (_dot(e_c.astype(BF16), vc) + _dot(e_n.astype(BF16), vn)) / l
    o_ref[...] = o.astype(o_ref.dtype)


def _na_bias_table(rpb, grid_rows):
    assert grid_rows >= NA_UNION_ROWS + NA_BLOCK_ROWS
    j = np.arange(NA_BLOCK_ROWS)[:, None]
    iu = np.arange(NA_UNION_ROWS)[None, :]
    cq = np.arange(GRID_W)[:, None]
    ck = np.arange(GRID_W)[None, :]
    cs = np.clip(cq - NA_KC // 2, 0, GRID_W - NA_KC)
    col_ok = (ck >= cs) & (ck < cs + NA_KC)
    dx = np.clip(ck - cq + NA_KC - 1, 0, 2 * NA_KC - 2)
    ex = np.eye(2 * NA_KC - 1, dtype=np.float32)[dx]
    eys, oks = [], []
    for r_off, w_off in ((j, 0 * j), (j + NA_KR // 2, j),
                         (j + NA_UNION_ROWS - NA_BLOCK_ROWS, 0 * j + NA_UNION_ROWS - NA_KR)):
        dy = np.clip(iu - r_off + NA_KR - 1, 0, 2 * NA_KR - 2)
        eys.append(np.eye(2 * NA_KR - 1, dtype=np.float32)[dy])
        oks.append((iu - w_off >= 0) & (iu - w_off < NA_KR))
    ey = np.stack(eys)
    ok = np.stack(oks)[:, :, None, :, None] & col_ok[None, None, :, None, :]
    vals = jnp.einsum("tjia,lhab,qkb->lhtjqik", ey, rpb.astype(F32), ex, precision=lax.Precision.HIGHEST)
    nb_part = jnp.where(ok[None, None], vals * LOG2E, NEG)
    nb_part = jnp.concatenate([nb_part, jnp.full_like(nb_part[:, :, :1], NEG)], axis=2)
    return nb_part.reshape(rpb.shape[:2] + (4, NA_BLOCK_ROWS * GRID_W, NA_UNION_ROWS * GRID_W))


def _na(pq, bias, *, nb, rb):
    rows = pq.shape[0]
    nblk_b = rb // SUB
    grid_rows = (rb - SUB) // GRID_W
    nqb = nblk_b - 1
    kern = functools.partial(_na_kernel, grid_rows=grid_rows, scale=HEAD_DIM ** -0.5 * LOG2E)

    def bias_map(b, h, t):
        ty = jnp.where(t == 0, 3, jnp.where(t == 1, 0, jnp.where(t == nqb, 2, 1)))
        return (h, ty, 0, 0)

    return pl.pallas_call(
        kern,
        out_shape=jax.ShapeDtypeStruct((rows, GROUP_W), BF16),
        grid=(nb, HEADS, nblk_b),
        in_specs=[
            pl.BlockSpec((SUB, HEAD_DIM), lambda b, h, t: (b * nblk_b + t, C_NQ // HEAD_DIM + h)),
            pl.BlockSpec((rb, HEAD_DIM), lambda b, h, t: (b, C_NK // HEAD_DIM + h)),
            pl.BlockSpec((rb, HEAD_DIM), lambda b, h, t: (b, C_NV // HEAD_DIM + h)),
            pl.BlockSpec((1, 1, SUB, bias.shape[-1]), bias_map),
        ],
        out_specs=pl.BlockSpec((SUB, HEAD_DIM), lambda b, h, t: (b * nblk_b + t, h)),
        compiler_params=_cparams(("parallel", "parallel", "arbitrary")),
        name="na",
    )(pq, pq, pq, bias)


def _gdn_prep_kernel(x_ref, prev_ref, next_ref, w_ref, g_ref, av_ref, dtb_ref, o_ref, og_ref, *, nblk_b):
    i = pl.program_id(0)
    t = i % nblk_b
    keep_prev = jnp.where((t == 0) | (t == 1), 0.0, 1.0)
    keep_next = jnp.where((t == 0) | (t == nblk_b - 1), 0.0, 1.0)
    x = x_ref[...].astype(F32)
    tm = x.shape[0]
    row = lax.broadcasted_iota(jnp.int32, x.shape, 0)
    halo_p = prev_ref[15:16, :].astype(F32) * keep_prev
    halo_n = next_ref[0:1, :].astype(F32) * keep_next
    xp = jnp.where(row == 0, halo_p, pltpu.roll(x, 1, 0))
    xn = jnp.where(row == tm - 1, halo_n, pltpu.roll(x, tm - 1, 0))
    c = xp * w_ref[0:1, :] + x * w_ref[1:2, :] + xn * w_ref[2:3, :]
    s = c * _sigmoid(c)
    for h in range(2 * HEADS):
        sh = s[:, h * 128:(h + 1) * 128]
        nrm = lax.rsqrt(jnp.sum(sh * sh, axis=-1, keepdims=True) + RMS_EPS)
        if h < HEADS:
            nrm = nrm * HEAD_DIM ** -0.5
        o_ref[:, h * 128:(h + 1) * 128] = (sh * nrm).astype(BF16)
    o_ref[:, 2 * GROUP_W:] = s[:, 2 * GROUP_W:].astype(BF16)
    g = g_ref[...]
    z = g + dtb_ref[...]
    softplus = jnp.maximum(z, 0.0) + jnp.log(1.0 + jnp.exp(-jnp.abs(z)))
    lane = lax.broadcasted_iota(jnp.int32, g.shape, 1)
    og_ref[...] = jnp.where(lane < 2 * HEADS, -av_ref[...] * softplus, _sigmoid(g))


def _gdn_prep(pq, pg, conv_w, a_vec, dtb_vec, *, nblk_b):
    rows = pq.shape[0]
    tm = SUB
    w3 = 3 * GROUP_W
    nhalo = rows // 16
    kern = functools.partial(_gdn_prep_kernel, nblk_b=nblk_b)
    return pl.pallas_call(
        kern,
        out_shape=(jax.ShapeDtypeStruct((rows, w3), BF16), jax.ShapeDtypeStruct((rows, 128), F32)),
        grid=(rows // tm,),
        in_specs=[
            pl.BlockSpec((tm, w3), lambda i: (i, 0)),
            pl.BlockSpec((16, w3), lambda i: (jnp.maximum(i * (tm // 16) - 1, 0), 0)),
            pl.BlockSpec((16, w3), lambda i: (jnp.minimum((i + 1) * (tm // 16), nhalo - 1), 0)),
            pl.BlockSpec((3, w3), lambda i: (0, 0)),
            pl.BlockSpec((tm, 128), lambda i: (i, 0)),
            pl.BlockSpec((1, 128), lambda i: (0, 0)),
            pl.BlockSpec((1, 128), lambda i: (0, 0)),
        ],
        out_specs=(pl.BlockSpec((tm, w3), lambda i: (i, 0)), pl.BlockSpec((tm, 128), lambda i: (i, 0))),
        compiler_params=_cparams(("parallel",)),
        name="gdn_prep",
    )(pq, pq, pq, conv_w, pg, a_vec, dtb_vec)


def _split3(x):
    hi = x.astype(BF16)
    r = x - hi.astype(F32)
    mid = r.astype(BF16)
    lo = (r - mid.astype(F32)).astype(BF16)
    return hi, mid, lo


def _gdn_chains(chains):
    c = chains[0][0].shape[0]
    row = lax.broadcasted_iota(jnp.int32, (c, c), 0)
    col = lax.broadcasted_iota(jnp.int32, (c, c), 1)
    xor = row ^ col
    eye = jnp.where(row == col, 1.0, 0.0)
    a_mats, qks, kbs, kfs, egs = [], [], [], [], []
    for q, k, v, gcol, grow, beta, glast, s_prev, lower in chains:
        incl = (row >= col) if lower else (row <= col)
        strict = (row > col) if lower else (row < col)
        decay = jnp.exp(jnp.where(incl, gcol - grow, NEG))
        kf = k.astype(F32)
        kb = kf * beta
        a_mats.append(jnp.where(strict, _dot_nt(kb.astype(BF16), k) * decay, 0.0))
        qks.append(jnp.where(incl, _dot_nt(q, k) * decay, 0.0).astype(BF16))
        kbs.append(kb)
        kfs.append(kf)
        egs.append(jnp.exp(gcol))
    tinvs = [eye - jnp.where(xor < 2, a, 0.0) for a in a_mats]
    s = 2
    while s < c:
        ys = [_dot(jnp.where((xor >= s) & (xor < 2 * s), a, 0.0).astype(BF16), t.astype(BF16))
              for a, t in zip(a_mats, tinvs)]
        tinvs = [t - _dot(t.astype(BF16), y.astype(BF16)) for t, y in zip(tinvs, ys)]
        s *= 2
    wus = []
    for (q, k, v, gcol, grow, beta, glast, s_prev, lower), kb, eg, t in zip(chains, kbs, egs, tinvs):
        rhs = jnp.concatenate([kb * eg, v.astype(F32) * beta], axis=1).astype(BF16)
        wus.append(_dot(t.astype(BF16), rhs))
    sbs = [ch[7].astype(BF16) for ch in chains]
    vnbs = [(wu[:, HEAD_DIM:] - _dot(wu[:, :HEAD_DIM].astype(BF16), sb)).astype(BF16) for wu, sb in zip(wus, sbs)]
    outs = []
    for (q, k, v, gcol, grow, beta, glast, s_prev, lower), kf, eg, qk, sb, vnb in zip(chains, kfs, egs, qks, sbs, vnbs):
        o = _dot((q.astype(F32) * eg).astype(BF16), sb) + _dot(qk, vnb)
        kd_t = (kf * jnp.exp(glast - gcol)).T.astype(BF16)
        s_new = s_prev * jnp.exp(glast) + _dot(kd_t, vnb)
        outs.append((o, s_new))
    return outs


def _gdn_kernel(xf_ref, xb_ref, gf_ref, gb_ref, of_ref, ob_ref, s_scr):
    st = pl.program_id(1)

    @pl.when(st == 0)
    def _():
        s_scr[...] = jnp.zeros_like(s_scr)

    c = GDN_CHUNK
    row = lax.broadcasted_iota(jnp.int32, (c, c), 0)
    col = lax.broadcasted_iota(jnp.int32, (c, c), 1)
    chains = []
    for d, (x_ref, g_ref) in enumerate(((xf_ref, gf_ref), (xb_ref, gb_ref))):
        lower = d == 0
        gates = g_ref[...]
        tri = jnp.where((row >= col) if lower else (row <= col), 1.0, 0.0).astype(BF16)
        hi, mid, lo = _split3(gates)
        gsum = _dot(tri, hi) + _dot(tri, mid) + _dot(tri, lo)
        gsum_t = gsum.T
        for h in range(HEADS):
            gi = d * HEADS + h
            bi = 2 * HEADS + gi
            glast = gsum[c - 1:c, gi:gi + 1] if lower else gsum[0:1, gi:gi + 1]
            chains.append((x_ref[:, h * 128:(h + 1) * 128],
                           x_ref[:, GROUP_W + h * 128:GROUP_W + (h + 1) * 128],
                           x_ref[:, 2 * GROUP_W + h * 128:2 * GROUP_W + (h + 1) * 128],
                           gsum[:, gi:gi + 1], gsum_t[gi:gi + 1, :], gates[:, bi:bi + 1], glast,
                           s_scr[d, h], lower))
    outs = _gdn_chains(chains)
    for idx, (o, s_new) in enumerate(outs):
        d, h = divmod(idx, HEADS)
        s_scr[d, h] = s_new
        (of_ref, ob_ref)[d][:, h * 128:(h + 1) * 128] = o


def _gdn(gx, gg, *, nb, rb):
    rows = gx.shape[0]
    c = GDN_CHUNK
    nch = rb // c
    nctx = SUB // c
    w3 = 3 * GROUP_W

    def fmap(b, s):
        return (b * nch + s, 0)

    def bmap(b, s):
        return (b * nch + jnp.where(s < nctx, nctx - 1 - s, nch + nctx - 1 - s), 0)

    return pl.pallas_call(
        _gdn_kernel,
        out_shape=(jax.ShapeDtypeStruct((rows, GROUP_W), F32), jax.ShapeDtypeStruct((rows, GROUP_W), F32)),
        grid=(nb, nch),
        in_specs=[
            pl.BlockSpec((c, w3), fmap), pl.BlockSpec((c, w3), bmap),
            pl.BlockSpec((c, 128), fmap), pl.BlockSpec((c, 128), bmap),
        ],
        out_specs=(pl.BlockSpec((c, GROUP_W), fmap), pl.BlockSpec((c, GROUP_W), bmap)),
        scratch_shapes=[pltpu.VMEM((2, HEADS, HEAD_DIM, HEAD_DIM), F32)],
        compiler_params=_cparams(("parallel", "arbitrary")),
        name="gdn",
    )(gx, gx, gg, gg)


def _gdn_finish_kernel(of_ref, ob_ref, z_ref, g_ref, o_ref):
    z = z_ref[...].astype(F32)
    gate = z * _sigmoid(z)
    for h in range(HEADS):
        sl = slice(h * 128, (h + 1) * 128)
        o = of_ref[:, sl] + ob_ref[:, sl]
        o = o * lax.rsqrt(jnp.mean(o * o, axis=-1, keepdims=True) + RMS_EPS) * g_ref[...]
        o_ref[:, sl] = (o * gate[:, sl]).astype(BF16)


def _gdn_finish(o_f, o_b, pq, norm_g):
    rows = o_f.shape[0]
    tm = 2 * SUB
    return pl.pallas_call(
        _gdn_finish_kernel,
        out_shape=jax.ShapeDtypeStruct((rows, GROUP_W), BF16),
        grid=(rows // tm,),
        in_specs=[
            pl.BlockSpec((tm, GROUP_W), lambda i: (i, 0)),
            pl.BlockSpec((tm, GROUP_W), lambda i: (i, 0)),
            pl.BlockSpec((tm, GROUP_W), lambda i: (i, C_GZ // GROUP_W)),
            pl.BlockSpec((1, HEAD_DIM), lambda i: (0, 0)),
        ],
        out_specs=pl.BlockSpec((tm, GROUP_W), lambda i: (i, 0)),
        compiler_params=_cparams(("parallel",)),
        name="gdn_finish",
    )(o_f, o_b, pq, norm_g)


def _layernorm_rows(z, g, b):
    mu = jnp.mean(z, axis=-1, keepdims=True)
    zc = z - mu
    var = jnp.mean(zc * zc, axis=-1, keepdims=True)
    return zc * lax.rsqrt(var + LN_EPS) * g + b


def _outproj_kernel(ya_ref, yb_ref, ym_ref, yn_ref, w_ref, x_ref, mod_ref, g_ref, b_ref, o_ref,
                    *, nsub, nblk_b, nb, d, alpha):
    i = pl.program_id(0)
    y = _dot(ya_ref[...], w_ref[0:GROUP_W, :])
    y = y + _dot(yb_ref[...], w_ref[GROUP_W:2 * GROUP_W, :])
    y = y + _dot(ym_ref[...], w_ref[2 * GROUP_W:3 * GROUP_W, :])
    y = y + _dot(yn_ref[...], w_ref[3 * GROUP_W:4 * GROUP_W, :])
    for s in range(nsub):
        g, _ = _group_of(i * nsub + s, nblk_b, nb)
        gate = mod_ref[0, pl.ds(g, 1), 2 * d:3 * d]
        sl = slice(s * SUB, (s + 1) * SUB)
        z = alpha * x_ref[sl, :] + gate * y[sl, :]
        o_ref[sl, :] = _layernorm_rows(z, g_ref[...], b_ref[...])


def _outproj(ys, w_out, xu, mod_l, ln_g, ln_b, *, nblk_b, nb, alpha):
    rows, d = xu.shape
    tm = 2 * SUB
    kern = functools.partial(_outproj_kernel, nsub=tm // SUB, nblk_b=nblk_b, nb=nb, d=d, alpha=alpha)
    yspec = pl.BlockSpec((tm, GROUP_W), lambda i: (i, 0))
    return pl.pallas_call(
        kern,
        out_shape=jax.ShapeDtypeStruct((rows, d), F32),
        grid=(rows // tm,),
        in_specs=[yspec, yspec, yspec, yspec,
                  pl.BlockSpec((4 * GROUP_W, d), lambda i: (0, 0)),
                  pl.BlockSpec((tm, d), lambda i: (i, 0)),
                  pl.BlockSpec((1, MOD_ROWS, 6 * d), lambda i: (0, 0, 0)),
                  pl.BlockSpec((1, d), lambda i: (0, 0)),
                  pl.BlockSpec((1, d), lambda i: (0, 0))],
        out_specs=pl.BlockSpec((tm, d), lambda i: (i, 0)),
        compiler_params=_cparams(("parallel",)),
        name="outproj",
    )(*ys, w_out, xu, mod_l, ln_g, ln_b)


def _ffn_kernel(x_ref, prev_ref, next_ref, mod_ref, wg_ref, wv_ref, cg_ref, cv_ref, wd_ref, g_ref, b_ref,
                o_ref, h_scr, hb_scr, acc_scr, *, nsub, nblk_b, nb, d, nj, alpha):
    i = pl.program_id(0)
    j = pl.program_id(1)
    tm = nsub * SUB

    @pl.when(j == 0)
    def _():
        acc_scr[...] = jnp.zeros_like(acc_scr)
        for s in range(nsub):
            g, _ = _group_of(i * nsub + s, nblk_b, nb)
            shift = mod_ref[0, pl.ds(g, 1), 3 * d:4 * d]
            scale = mod_ref[0, pl.ds(g, 1), 4 * d:5 * d]
            h_scr[8 + s * SUB:8 + (s + 1) * SUB, :] = x_ref[s * SUB:(s + 1) * SUB, :] * (1.0 + scale) + shift
            if s == 0:
                h_scr[0:8, :] = prev_ref[...] * (1.0 + scale) + shift
            if s == nsub - 1:
                h_scr[tm + 8:tm + 16, :] = next_ref[...] * (1.0 + scale) + shift
        hb_scr[...] = h_scr[...].astype(BF16)

    row = lax.broadcasted_iota(jnp.int32, (tm, 1), 0)
    keep_prev = jnp.ones((tm, 1), F32)
    keep_next = jnp.ones((tm, 1), F32)
    for s in range(nsub):
        _, t = _group_of(i * nsub + s, nblk_b, nb)
        first = (t == 0) | (t == 1)
        last = (t == 0) | (t == nblk_b - 1)
        keep_prev = jnp.where((row == s * SUB) & first, 0.0, keep_prev)
        keep_next = jnp.where((row == s * SUB + SUB - 1) & last, 0.0, keep_next)

    hb = hb_scr[...]
    ext = tm + 16

    def conv(u, cw_ref):
        up = pltpu.roll(u, 1, 0)[8:tm + 8, :] * keep_prev
        un = pltpu.roll(u, ext - 1, 0)[8:tm + 8, :] * keep_next
        return up * cw_ref[0:1, :] + u[8:tm + 8, :] * cw_ref[1:2, :] + un * cw_ref[2:3, :]

    gate = conv(_dot(hb, wg_ref[...]), cg_ref)
    val = conv(_dot(hb, wv_ref[...]), cv_ref)
    act = (gate * _sigmoid(gate) * val).astype(BF16)
    acc_scr[...] += _dot(act, wd_ref[...])

    @pl.when(j == nj - 1)
    def _():
        for s in range(nsub):
            g, _ = _group_of(i * nsub + s, nblk_b, nb)
            gt = mod_ref[0, pl.ds(g, 1), 5 * d:6 * d]
            sl = slice(s * SUB, (s + 1) * SUB)
            z = alpha * x_ref[sl, :] + gt * acc_scr[sl, :]
            o_ref[sl, :] = _layernorm_rows(z, g_ref[...], b_ref[...])


def _ffn(xu, mod_l, w_up, conv_w, w_down, ln_g, ln_b, *, nblk_b, nb, alpha):
    rows, d = xu.shape
    dff = w_down.shape[0]
    tm, tc = 2 * SUB, 512
    nj = dff // tc
    nhalo = rows // 8
    kern = functools.partial(_ffn_kernel, nsub=tm // SUB, nblk_b=nblk_b, nb=nb, d=d, nj=nj, alpha=alpha)
    return pl.pallas_call(
        kern,
        out_shape=jax.ShapeDtypeStruct((rows, d), F32),
        grid=(rows // tm, nj),
        in_specs=[
            pl.BlockSpec((tm, d), lambda i, j: (i, 0)),
            pl.BlockSpec((8, d), lambda i, j: (jnp.maximum(i * (tm // 8) - 1, 0), 0)),
            pl.BlockSpec((8, d), lambda i, j: (jnp.minimum((i + 1) * (tm // 8), nhalo - 1), 0)),
            pl.BlockSpec((1, MOD_ROWS, 6 * d), lambda i, j: (0, 0, 0)),
            pl.BlockSpec((d, tc), lambda i, j: (0, j)),
            pl.BlockSpec((d, tc), lambda i, j: (0, nj + j)),
            pl.BlockSpec((3, tc), lambda i, j: (0, j)),
            pl.BlockSpec((3, tc), lambda i, j: (0, nj + j)),
            pl.BlockSpec((tc, d), lambda i, j: (j, 0)),
            pl.BlockSpec((1, d), lambda i, j: (0, 0)),
            pl.BlockSpec((1, d), lambda i, j: (0, 0)),
        ],
        out_specs=pl.BlockSpec((tm, d), lambda i, j: (i, 0)),
        scratch_shapes=[pltpu.VMEM((tm + 16, d), F32), pltpu.VMEM((tm + 16, d), BF16), pltpu.VMEM((tm, d), F32)],
        compiler_params=_cparams(("parallel", "arbitrary")),
        name="ffn",
    )(xu, xu, xu, mod_l, w_up, w_up, conv_w, conv_w, w_down, ln_g, ln_b)


def _rope_tables(n_lat):
    pos = jnp.arange(n_lat)
    row = (pos // GRID_W).astype(F32)
    col = (pos % GRID_W).astype(F32)
    half = MLA_ROPE // 2
    inv = ROPE_THETA ** (-jnp.arange(0, half, 2, dtype=F32) / half)
    ar = row[:, None] * inv[None, :]
    ac = col[:, None] * inv[None, :]
    cos64 = jnp.concatenate([jnp.cos(ar), jnp.cos(ar), jnp.cos(ac), jnp.cos(ac)], axis=-1)
    sin64 = jnp.concatenate([-jnp.sin(ar), jnp.sin(ar), -jnp.sin(ac), jnp.sin(ac)], axis=-1)
    one = jnp.ones((n_lat, 64), F32)
    zero = jnp.zeros((n_lat, 64), F32)

    def with_ctx(c, s):
        c = jnp.concatenate([jnp.ones((SUB, 128), F32), c], axis=0)
        s = jnp.concatenate([jnp.zeros((SUB, 128), F32), s], axis=0)
        return c, s

    diff = with_ctx(jnp.concatenate([cos64, cos64], -1), jnp.concatenate([sin64, sin64], -1))
    mla = with_ctx(jnp.concatenate([cos64, one], -1), jnp.concatenate([sin64, zero], -1))
    return diff, mla


def _prep_w_in(w_in):
    depth, d, _ = w_in.shape
    z = lambda n: jnp.zeros((depth, d, n), w_in.dtype)
    o_diff = 4 * GROUP_W + 4 * HEADS
    o_mla = o_diff + 3 * GROUP_W
    o_na = o_mla + MLA_Q_RANK + MLA_KV_RANK + MLA_ROPE
    w = jnp.concatenate([w_in[..., :4 * GROUP_W], w_in[..., o_diff:o_mla], w_in[..., o_na:],
                         w_in[..., o_mla:o_na], z(MLA_W - (o_na - o_mla)),
                         w_in[..., 4 * GROUP_W:o_diff], z(128 - 4 * HEADS)], axis=-1)
    assert w.shape[-1] == N_PROJ
    return w.astype(BF16)


def _prep_mla_weights(w_uq, w_ukv):
    depth = w_uq.shape[0]
    dq = MLA_NOPE + MLA_ROPE
    wq = w_uq.reshape(depth, MLA_Q_RANK, HEADS, dq)
    wq = jnp.concatenate([wq, jnp.zeros((depth, MLA_Q_RANK, HEADS, 256 - dq), w_uq.dtype)], axis=-1)
    wq = wq.reshape(depth, MLA_Q_RANK, HEADS * 256)
    wkv = w_ukv.reshape(depth, MLA_KV_RANK, HEADS, MLA_NOPE + HEAD_DIM)
    wkv = jnp.concatenate([wkv[..., :MLA_NOPE].reshape(depth, MLA_KV_RANK, GROUP_W),
                           wkv[..., MLA_NOPE:].reshape(depth, MLA_KV_RANK, GROUP_W)], axis=-1)
    return wq.astype(BF16), wkv.astype(BF16)


def _pad_lanes(v, n=128):
    return jnp.concatenate([v, jnp.zeros(v.shape[:-1] + (n - v.shape[-1],), v.dtype)], axis=-1)


def _mixers(pq, pg, lw, l, tabs, *, nb, rb):
    nblk_b = rb // SUB
    (dcos, dsin), (mcos, msin) = tabs
    gx, gg = _gdn_prep(pq, pg, lw["gdn_conv"], lw["gdn_a"], lw["gdn_dtb"], nblk_b=nblk_b)
    o_f, o_b = _gdn(gx, gg, nb=nb, rb=rb)
    ya = _gdn_finish(o_f, o_b, pq, lw["gdn_norm_g"])
    qk_d, vt_d = _diff_prep(pq, dcos, dsin, nblk_b=nblk_b)
    yb = _flash(qk_d, qk_d, vt_d, nb=nb, rb=rb, dq=HEAD_DIM, qcol=0, kcol=GROUP_W, nsub=2,
                lam=lw["diff_lam"], norm_g=lw["diff_norm_g"])
    qm, km, vm = _mla_prep(pq, lw["mla_gq"], lw["mla_gkv"], lw["mla_wq"], lw["mla_wkv"], mcos, msin,
                           nblk_b=nblk_b)
    ym = _flash(qm, km, vm, nb=nb, rb=rb, dq=256, qcol=0, kcol=0)
    yn = _na(pq, lw["na_bias"], nb=nb, rb=rb)
    return ya, yb, ym, yn


def _layer(xu, lw, l, tabs, *, nb, rb, alpha):
    nblk_b = rb // SUB
    mod_l = lw["mod"]
    pq, pg = _inproj(xu, mod_l, lw["w_in"], nblk_b=nblk_b, nb=nb)
    ys = _mixers(pq, pg, lw, l, tabs, nb=nb, rb=rb)
    x1 = _outproj(ys, lw["w_out"], xu, mod_l, lw["ln_g0"], lw["ln_b0"], nblk_b=nblk_b, nb=nb, alpha=alpha)
    return _ffn(x1, mod_l, lw["ffn_w_up"], lw["ffn_conv"], lw["ffn_w_down"], lw["ln_g1"], lw["ln_b1"],
                nblk_b=nblk_b, nb=nb, alpha=alpha)


def _layer_weights(l, mod, w_in_p, gdn_conv, gdn_a_log, gdn_dt_bias, gdn_norm_g, diff_lambda, diff_norm_g,
                   mla_q_norm_g, mla_kv_norm_g, wq, wkv, na_bias, w_out_b, ln_g, ln_b, w_up_b, ffn_conv,
                   w_down_b):
    lf = diff_lambda[l].astype(F32)
    lam_init = 0.8 - 0.6 * math.exp(-0.3 * l)
    lam = jnp.exp(jnp.sum(lf[0] * lf[1])) - jnp.exp(jnp.sum(lf[2] * lf[3])) + lam_init
    return {
        "mod": mod[l:l + 1],
        "w_in": w_in_p[l],
        "gdn_conv": gdn_conv[l],
        "gdn_a": _pad_lanes(jnp.exp(gdn_a_log[l].astype(F32)).reshape(1, 2 * HEADS)),
        "gdn_dtb": _pad_lanes(gdn_dt_bias[l].astype(F32).reshape(1, 2 * HEADS)),
        "gdn_norm_g": gdn_norm_g[l].reshape(1, HEAD_DIM),
        "diff_lam": jnp.stack([lam, jnp.asarray(1.0 - lam_init, F32)]),
        "diff_norm_g": diff_norm_g[l].reshape(1, HEAD_DIM),
        "mla_gq": mla_q_norm_g[l].reshape(1, MLA_Q_RANK),
        "mla_gkv": mla_kv_norm_g[l].reshape(1, MLA_KV_RANK),
        "mla_wq": wq[l],
        "mla_wkv": wkv[l],
        "na_bias": na_bias[l],
        "w_out": w_out_b[l],
        "ln_g0": ln_g[l, 0:1], "ln_b0": ln_b[l, 0:1], "ln_g1": ln_g[l, 1:2], "ln_b1": ln_b[l, 1:2],
        "ffn_w_up": w_up_b[l],
        "ffn_conv": ffn_conv[l],
        "ffn_w_down": w_down_b[l],
    }


def kernel(x, c, ctx, c_ctx, w_mod, b_mod, w_in, gdn_conv, gdn_a_log, gdn_dt_bias, gdn_norm_g, diff_lambda,
           diff_norm_g, mla_q_norm_g, mla_kv_norm_g, mla_w_uq, mla_w_ukv, na_rpb, w_out, ln_g, ln_b,
           ffn_w_up, ffn_conv, ffn_w_down):
    nb, n_lat, d = x.shape
    depth = w_mod.shape[0]
    assert ctx.shape[1] == SUB and n_lat % (2 * SUB) == 0 and n_lat % GRID_W == 0
    assert (nb * (SUB + n_lat)) % (2 * SUB) == 0 and nb + 1 <= MOD_ROWS
    rb = SUB + n_lat
    alpha = (2 * depth) ** 0.25

    cvec = jnp.concatenate([c, c_ctx[None, :], jnp.zeros((MOD_ROWS - nb - 1, d), c.dtype)], axis=0)
    mod = _modulation(cvec, w_mod, b_mod)
    tabs = _rope_tables(n_lat)
    w_in_p = _prep_w_in(w_in)
    wq, wkv = _prep_mla_weights(mla_w_uq, mla_w_ukv)
    na_bias = _na_bias_table(na_rpb, n_lat // GRID_W)
    w_out_b = w_out.astype(BF16)
    w_up_b = ffn_w_up.astype(BF16)
    w_down_b = ffn_w_down.astype(BF16)

    xu = jnp.concatenate([ctx, x], axis=1).reshape(nb * rb, d)
    for l in range(depth):
        lw = _layer_weights(l, mod, w_in_p, gdn_conv, gdn_a_log, gdn_dt_bias, gdn_norm_g, diff_lambda,
                            diff_norm_g, mla_q_norm_g, mla_kv_norm_g, wq, wkv, na_bias, w_out_b, ln_g, ln_b,
                            w_up_b, ffn_conv, w_down_b)
        xu = _layer(xu, lw, l, tabs, nb=nb, rb=rb, alpha=alpha)
    return xu.reshape(nb, rb, d)[:, SUB:, :]
```

```python
import functools
import math

import numpy as np
import jax
import jax.numpy as jnp
from jax import lax
from jax.experimental import pallas as pl
from jax.experimental.pallas import tpu as pltpu

F32 = jnp.float32
BF16 = jnp.bfloat16

HEADS = 4
HEAD_DIM = 128
GROUP_W = HEADS * HEAD_DIM
GRID_W = 64
NA_KR = 8
NA_KC = 16
NA_BLOCK_ROWS = 4
NA_UNION_ROWS = NA_BLOCK_ROWS + NA_KR - 1
MLA_Q_RANK = 384
MLA_KV_RANK = 128
MLA_NOPE = 128
MLA_ROPE = 64
ROPE_THETA = 10000.0
LN_EPS = 1e-5
RMS_EPS = 1e-6
NEG = -1e30
LOG2E = math.log2(math.e)

SUB = 256
GDN_CHUNK = 128
MOD_ROWS = 16
VMEM_LIMIT = 56 * 1024 * 1024

C_GQ, C_GK, C_GV, C_GZ = 0, 512, 1024, 1536
C_DQ, C_DK, C_DV = 2048, 2560, 3072
C_NQ, C_NK, C_NV = 3584, 4096, 4608
C_MLA = 5120
MLA_W = 640
C_AB = 5760
N_PROJ = 5888


def _cparams(sem):
    return pltpu.CompilerParams(dimension_semantics=sem, vmem_limit_bytes=VMEM_LIMIT)


def _sigmoid(x):
    return 1.0 / (1.0 + jnp.exp(-x))


def _dot(a, b):
    return jnp.dot(a, b, preferred_element_type=F32)


def _dot_nt(a, b):
    return lax.dot_general(a, b, (((1,), (1,)), ((), ())), preferred_element_type=F32)


def _group_of(blk, nblk_b, nb):
    t = blk % nblk_b
    return jnp.where(t == 0, nb, blk // nblk_b), t


def _mod_kernel(c_ref, w_ref, b_ref, o_ref):
    c = c_ref[...]
    s = (c * _sigmoid(c)).astype(BF16)
    o_ref[0] = _dot(s, w_ref[0].astype(BF16)) + b_ref[0]


def _modulation(cvec, w_mod, b_mod):
    depth, d, n = w_mod.shape
    tn = 1536
    return pl.pallas_call(
        _mod_kernel,
        out_shape=jax.ShapeDtypeStruct((depth, MOD_ROWS, n), F32),
        grid=(depth, n // tn),
        in_specs=[
            pl.BlockSpec((MOD_ROWS, d), lambda l, j: (0, 0)),
            pl.BlockSpec((1, d, tn), lambda l, j: (l, 0, j)),
            pl.BlockSpec((1, 1, tn), lambda l, j: (l, 0, j)),
        ],
        out_specs=pl.BlockSpec((1, MOD_ROWS, tn), lambda l, j: (l, 0, j)),
        compiler_params=_cparams(("parallel", "parallel")),
        name="modulation",
    )(cvec, w_mod, b_mod.reshape(depth, 1, n))


def _inproj_kernel(x_ref, mod_ref, w_ref, o_ref, og_ref, h_scr, *, nsub, nblk_b, nb, d, nj):
    i = pl.program_id(0)
    j = pl.program_id(1)

    @pl.when(j == 0)
    def _():
        for s in range(nsub):
            g, _ = _group_of(i * nsub + s, nblk_b, nb)
            shift = mod_ref[0, pl.ds(g, 1), 0:d]
            scale = mod_ref[0, pl.ds(g, 1), d:2 * d]
            xs = x_ref[s * SUB:(s + 1) * SUB, :]
            h_scr[s * SUB:(s + 1) * SUB, :] = (xs * (1.0 + scale) + shift).astype(BF16)

    acc = _dot(h_scr[...], w_ref[...])
    o_ref[...] = acc.astype(BF16)

    @pl.when(j == nj - 1)
    def _():
        og_ref[...] = acc[:, acc.shape[1] - 128:]


def _inproj(xu, mod_l, w_in_p, *, nblk_b, nb):
    rows, d = xu.shape
    n = w_in_p.shape[1]
    tm, tn = 2 * SUB, n // 2
    nj = n // tn
    kern = functools.partial(_inproj_kernel, nsub=tm // SUB, nblk_b=nblk_b, nb=nb, d=d, nj=nj)
    return pl.pallas_call(
        kern,
        out_shape=(jax.ShapeDtypeStruct((rows, n), BF16), jax.ShapeDtypeStruct((rows, 128), F32)),
        grid=(rows // tm, nj),
        in_specs=[
            pl.BlockSpec((tm, d), lambda i, j: (i, 0)),
            pl.BlockSpec((1, MOD_ROWS, 6 * d), lambda i, j: (0, 0, 0)),
            pl.BlockSpec((d, tn), lambda i, j: (0, j)),
        ],
        out_specs=(pl.BlockSpec((tm, tn), lambda i, j: (i, j)),
                   pl.BlockSpec((tm, 128), lambda i, j: (i, 0))),
        scratch_shapes=[pltpu.VMEM((tm, d), BF16)],
        compiler_params=_cparams(("parallel", "arbitrary")),
        name="inproj",
    )(xu, mod_l, w_in_p)


def _rope_slab(x, cos, sin):
    lane = lax.broadcasted_iota(jnp.int32, x.shape, 1)
    odd = (lane & 16) != 0
    partner = jnp.where(odd, pltpu.roll(x, 16, 1), pltpu.roll(x, 128 - 16, 1))
    return x * cos + partner * sin


def _diff_prep_kernel(p_ref, v_ref, cos_ref, sin_ref, o_ref, vt_ref, *, scale):
    vt_ref[0] = v_ref[...].astype(F32).T.astype(BF16)
    cos = cos_ref[...]
    sin = sin_ref[...]
    for s in range(2 * HEADS):
        x = p_ref[:, s * 128:(s + 1) * 128].astype(F32)
        r = _rope_slab(x, cos, sin)
        if s < HEADS:
            r = r * scale
        o_ref[:, s * 128:(s + 1) * 128] = r.astype(BF16)


def _diff_prep(pq, cos, sin, *, nblk_b):
    rows = pq.shape[0]
    tm = SUB
    kern = functools.partial(_diff_prep_kernel, scale=(HEAD_DIM // 2) ** -0.5 * LOG2E)
    return pl.pallas_call(
        kern,
        out_shape=(jax.ShapeDtypeStruct((rows, 2 * GROUP_W), BF16),
                   jax.ShapeDtypeStruct((rows // tm, GROUP_W, tm), BF16)),
        grid=(rows // tm,),
        in_specs=[
            pl.BlockSpec((tm, 2 * GROUP_W), lambda i: (i, C_DQ // (2 * GROUP_W))),
            pl.BlockSpec((tm, GROUP_W), lambda i: (i, C_DV // GROUP_W)),
            pl.BlockSpec((tm, 128), lambda i: (i % nblk_b, 0)),
            pl.BlockSpec((tm, 128), lambda i: (i % nblk_b, 0)),
        ],
        out_specs=(pl.BlockSpec((tm, 2 * GROUP_W), lambda i: (i, 0)),
                   pl.BlockSpec((1, GROUP_W, tm), lambda i: (i, 0, 0))),
        compiler_params=_cparams(("parallel",)),
        name="diff_prep",
    )(pq, pq, cos, sin)


def _mla_prep_kernel(p_ref, gq_ref, gkv_ref, wq_ref, wkv_ref, cos_ref, sin_ref,
                     q_ref, k_ref, v_ref, *, scale):
    cos = cos_ref[...]
    sin = sin_ref[...]
    cq = p_ref[:, 0:MLA_Q_RANK].astype(F32)
    cq = cq * lax.rsqrt(jnp.mean(cq * cq, axis=-1, keepdims=True) + RMS_EPS) * gq_ref[...]
    q = _dot(cq.astype(BF16), wq_ref[...])
    ckv = p_ref[:, MLA_Q_RANK:MLA_Q_RANK + MLA_KV_RANK].astype(F32)
    ckv = ckv * lax.rsqrt(jnp.mean(ckv * ckv, axis=-1, keepdims=True) + RMS_EPS) * gkv_ref[...]
    kv = _dot(ckv.astype(BF16), wkv_ref[...])
    kr = _rope_slab(p_ref[:, MLA_Q_RANK + MLA_KV_RANK:MLA_W].astype(F32), cos, sin).astype(BF16)
    for h in range(HEADS):
        q_ref[:, h * 256:h * 256 + 128] = (q[:, h * 256:h * 256 + 128] * scale).astype(BF16)
        qr = _rope_slab(q[:, h * 256 + 128:(h + 1) * 256], cos, sin)
        q_ref[:, h * 256 + 128:(h + 1) * 256] = (qr * scale).astype(BF16)
        k_ref[:, h * 256:h * 256 + 128] = kv[:, h * 128:(h + 1) * 128].astype(BF16)
        k_ref[:, h * 256 + 128:(h + 1) * 256] = kr
    v_ref[0] = kv[:, GROUP_W:].T.astype(BF16)


def _mla_prep(pq, gq, gkv, wq, wkv, cos, sin, *, nblk_b):
    rows = pq.shape[0]
    tm = SUB
    kern = functools.partial(_mla_prep_kernel, scale=(MLA_NOPE + MLA_ROPE) ** -0.5 * LOG2E)
    return pl.pallas_call(
        kern,
        out_shape=(jax.ShapeDtypeStruct((rows, HEADS * 256), BF16),
                   jax.ShapeDtypeStruct((rows, HEADS * 256), BF16),
                   jax.ShapeDtypeStruct((rows // tm, GROUP_W, tm), BF16)),
        grid=(rows // tm,),
        in_specs=[
            pl.BlockSpec((tm, MLA_W), lambda i: (i, C_MLA // MLA_W)),
            pl.BlockSpec((1, MLA_Q_RANK), lambda i: (0, 0)),
            pl.BlockSpec((1, MLA_KV_RANK), lambda i: (0, 0)),
            pl.BlockSpec((MLA_Q_RANK, HEADS * 256), lambda i: (0, 0)),
            pl.BlockSpec((MLA_KV_RANK, 2 * GROUP_W), lambda i: (0, 0)),
            pl.BlockSpec((tm, 128), lambda i: (i % nblk_b, 0)),
            pl.BlockSpec((tm, 128), lambda i: (i % nblk_b, 0)),
        ],
        out_specs=(pl.BlockSpec((tm, HEADS * 256), lambda i: (i, 0)),
                   pl.BlockSpec((tm, HEADS * 256), lambda i: (i, 0)),
                   pl.BlockSpec((1, GROUP_W, tm), lambda i: (i, 0, 0))),
        compiler_params=_cparams(("parallel",)),
        name="mla_prep",
    )(pq, gq, gkv, wq, wkv, cos, sin)


def _flash_kernel(*refs, nsub, hp, dq, n_lat_pairs):
    if nsub == 2:
        lam_ref, g_ref, q_ref, k_ref, vt_ref, o_ref, s_scr, p_scr, acc_scr = refs
    else:
        q_ref, k_ref, vt_ref, o_ref, s_scr, p_scr, acc_scr = refs
    qt = pl.program_id(2)
    tq = q_ref.shape[0]
    tk = 2 * SUB
    qs = []
    for hh in range(hp):
        q = q_ref[:, hh * dq:(hh + 1) * dq]
        if nsub == 2:
            lane = lax.broadcasted_iota(jnp.int32, q.shape, 1)
            zero = jnp.zeros_like(q)
            qs.append((hh, jnp.where(lane < 64, q, zero)))
            qs.append((hh, jnp.where(lane >= 64, q, zero)))
        else:
            qs.append((hh, q))
    nch = len(qs)

    def scores_into(slot, rows, nrows):
        maxes = []
        for c, (hh, qi) in enumerate(qs):
            s = _dot_nt(k_ref[rows, hh * dq:(hh + 1) * dq], qi)
            s_scr[slot, c, 0:nrows, :] = s
            maxes.append(jnp.max(s, axis=0, keepdims=True))
        return maxes

    def pv_from(slot, vblocks):
        for c, (hh, _) in enumerate(qs):
            pv = None
            for i, blk in enumerate(vblocks):
                part = _dot(vt_ref[blk, hh * HEAD_DIM:(hh + 1) * HEAD_DIM, :],
                            p_scr[slot, c, i * SUB:(i + 1) * SUB, :])
                pv = part if pv is None else pv + part
            acc_scr[c] += pv

    def softmax(slot, nrows, stats, maxes):
        ch = 64
        new = []
        for c, ((m, l), mb) in enumerate(zip(stats, maxes)):
            m_new = jnp.maximum(m, mb)
            psum = None
            for r in range(0, nrows, ch):
                p = jnp.exp2(s_scr[slot, c, r:r + ch, :] - m_new)
                p_scr[slot, c, r:r + ch, :] = p.astype(BF16)
                psum = p if psum is None else psum + p
            alpha = jnp.exp2(m - m_new)
            acc_scr[c] = acc_scr[c] * alpha
            new.append((m_new, alpha * l + jnp.sum(psum, axis=0, keepdims=True)))
        return new

    def lat_rows(j):
        return pl.ds(pl.multiple_of(SUB + j * tk, SUB), tk)

    acc_scr[...] = jnp.zeros_like(acc_scr)
    stats = [(jnp.full((1, tq), NEG, F32), jnp.zeros((1, tq), F32)) for _ in qs]
    mx_ctx = scores_into(1, slice(0, SUB), SUB)
    mx0 = scores_into(0, lat_rows(0), tk)
    p_scr[1, :, SUB:, :] = jnp.zeros((nch, SUB, tq), BF16)
    stats = softmax(1, SUB, stats, mx_ctx)
    unroll = 2 if n_lat_pairs % 2 == 0 else 1
    n = jnp.where(qt == 0, 0, n_lat_pairs // unroll)
    last = 2 * n_lat_pairs - 1

    def pair(a, stats, mx0):
        bp = jnp.where(a == 0, 0, 2 * a - 1)
        pv_from(1, (bp, bp + 1))
        mx1 = scores_into(1, lat_rows(a + 1), tk)
        stats = softmax(0, tk, stats, mx0)
        pv_from(0, (1 + 2 * a, 2 + 2 * a))
        mx0 = scores_into(0, lat_rows(jnp.minimum(a + 2, last)), tk)
        return softmax(1, tk, stats, mx1), mx0

    def body(jj, carry):
        stats = tuple(zip(carry[0:2 * nch:2], carry[1:2 * nch:2]))
        mx0 = list(carry[2 * nch:])
        for u in range(unroll):
            stats, mx0 = pair(2 * (jj * unroll + u), stats, mx0)
        return tuple(x for st in stats for x in st) + tuple(mx0)

    flat = lax.fori_loop(0, n, body, tuple(x for st in stats for x in st) + tuple(mx0))
    stats = tuple(zip(flat[0:2 * nch:2], flat[1:2 * nch:2]))
    done = 2 * unroll * n
    bl = jnp.where(done == 0, 0, 2 * done - 1)
    pv_from(1, (bl, bl + 1))

    for hh in range(hp):
        if nsub == 2:
            o1 = acc_scr[2 * hh] / stats[2 * hh][1]
            o2 = acc_scr[2 * hh + 1] / stats[2 * hh + 1][1]
            o = (o1 - lam_ref[0] * o2).T
            o = o * lax.rsqrt(jnp.mean(o * o, axis=-1, keepdims=True) + RMS_EPS) * g_ref[...] * lam_ref[1]
        else:
            o = (acc_scr[hh] / stats[hh][1]).T
        o_ref[:, hh * HEAD_DIM:(hh + 1) * HEAD_DIM] = o.astype(o_ref.dtype)


def _flash(q_arr, k_arr, vt_arr, *, nb, rb, dq, qcol, kcol, nsub=1, lam=None, norm_g=None):
    rows = q_arr.shape[0]
    nblk_b = rb // SUB
    lat = rb - SUB
    hp = 2
    assert lat % (2 * SUB) == 0 and qcol % (hp * dq) == 0 and kcol % (hp * dq) == 0
    assert lat % (4 * SUB) == 0
    nch = hp * nsub
    kern = functools.partial(_flash_kernel, nsub=nsub, hp=hp, dq=dq, n_lat_pairs=lat // (4 * SUB))
    in_specs = [
        pl.BlockSpec((SUB, hp * dq), lambda b, h, t: (b * nblk_b + t, qcol // (hp * dq) + h)),
        pl.BlockSpec((rb, hp * dq), lambda b, h, t: (b, kcol // (hp * dq) + h)),
        pl.BlockSpec((nblk_b, hp * HEAD_DIM, SUB), lambda b, h, t: (b, h, 0)),
    ]
    args = [q_arr, k_arr, vt_arr]
    if nsub == 2:
        in_specs = [pl.BlockSpec(memory_space=pltpu.SMEM),
                    pl.BlockSpec((1, HEAD_DIM), lambda b, h, t: (0, 0))] + in_specs
        args = [lam, norm_g] + args
    return pl.pallas_call(
        kern,
        out_shape=jax.ShapeDtypeStruct((rows, GROUP_W), BF16),
        grid=(nb, HEADS // hp, nblk_b),
        in_specs=in_specs,
        out_specs=pl.BlockSpec((SUB, hp * HEAD_DIM), lambda b, h, t: (b * nblk_b + t, h)),
        scratch_shapes=[pltpu.VMEM((2, nch, 2 * SUB, SUB), F32), pltpu.VMEM((2, nch, 2 * SUB, SUB), BF16),
                        pltpu.VMEM((nch, HEAD_DIM, SUB), F32)],
        compiler_params=_cparams(("parallel", "parallel", "arbitrary")),
        name="flash_diff" if nsub == 2 else "flash_mla",
    )(*args)


def _na_kernel(q_ref, k_ref, v_ref, bias_ref, o_ref, *, grid_rows, scale, hp):
    qt = pl.program_id(2)
    qb = jnp.maximum(qt - 1, 0)
    u0 = jnp.clip(qb * NA_BLOCK_ROWS - NA_KR // 2, 0, grid_rows - NA_UNION_ROWS)
    start = pl.multiple_of(SUB + u0 * GRID_W, GRID_W)
    nn = NA_UNION_ROWS * GRID_W
    hs = [slice(h * HEAD_DIM, (h + 1) * HEAD_DIM) for h in range(hp)]
    qs = [(q_ref[:, sl].astype(F32) * scale).astype(BF16) for sl in hs]
    s_cs = [_dot_nt(q, k_ref[0:SUB, sl]) for q, sl in zip(qs, hs)]
    s_ns = [_dot_nt(q, k_ref[pl.ds(start, nn), sl]) + bias_ref[h, 0] for h, (q, sl) in enumerate(zip(qs, hs))]
    ms = [jnp.maximum(jnp.max(s_c, axis=-1, keepdims=True), jnp.max(s_n, axis=-1, keepdims=True))
          for s_c, s_n in zip(s_cs, s_ns)]
    e_cs = [jnp.exp2(s_c - m) for s_c, m in zip(s_cs, ms)]
    e_ns = [jnp.exp2(s_n - m) for s_n, m in zip(s_ns, ms)]
    for sl, e_c, e_n in zip(hs, e_cs, e_ns):
        l = jnp.sum(e_c, axis=-1, keepdims=True) + jnp.sum(e_n, axis=-1, keepdims=True)
        o = (_dot(e_c.astype(BF16), v_ref[0:SUB, sl]) + _dot(e_n.astype(BF16), v_ref[pl.ds(start, nn), sl])) / l
        o_ref[:, sl] = o.astype(o_ref.dtype)


def _na_bias_table(rpb, grid_rows):
    assert grid_rows >= NA_UNION_ROWS + NA_BLOCK_ROWS
    j = np.arange(NA_BLOCK_ROWS)[:, None]
    iu = np.arange(NA_UNION_ROWS)[None, :]
    cq = np.arange(GRID_W)[:, None]
    ck = np.arange(GRID_W)[None, :]
    cs = np.clip(cq - NA_KC // 2, 0, GRID_W - NA_KC)
    col_ok = (ck >= cs) & (ck < cs + NA_KC)
    dx = np.clip(ck - cq + NA_KC - 1, 0, 2 * NA_KC - 2)
    ex = np.eye(2 * NA_KC - 1, dtype=np.float32)[dx]
    eys, oks = [], []
    for r_off, w_off in ((j, 0 * j), (j + NA_KR // 2, j),
                         (j + NA_UNION_ROWS - NA_BLOCK_ROWS, 0 * j + NA_UNION_ROWS - NA_KR)):
        dy = np.clip(iu - r_off + NA_KR - 1, 0, 2 * NA_KR - 2)
        eys.append(np.eye(2 * NA_KR - 1, dtype=np.float32)[dy])
        oks.append((iu - w_off >= 0) & (iu - w_off < NA_KR))
    ey = np.stack(eys)
    ok = np.stack(oks)[:, :, None, :, None] & col_ok[None, None, :, None, :]
    vals = jnp.einsum("tjia,lhab,qkb->lhtjqik", ey, rpb.astype(F32), ex, precision=lax.Precision.HIGHEST)
    nb_part = jnp.where(ok[None, None], vals * LOG2E, NEG)
    nb_part = jnp.concatenate([nb_part, jnp.full_like(nb_part[:, :, :1], NEG)], axis=2)
    return nb_part.reshape(rpb.shape[:2] + (4, NA_BLOCK_ROWS * GRID_W, NA_UNION_ROWS * GRID_W))


def _na(pq, bias, *, nb, rb):
    rows = pq.shape[0]
    nblk_b = rb // SUB
    grid_rows = (rb - SUB) // GRID_W
    nqb = nblk_b - 1
    hp = 2
    hw = hp * HEAD_DIM
    kern = functools.partial(_na_kernel, grid_rows=grid_rows, scale=HEAD_DIM ** -0.5 * LOG2E, hp=hp)

    def bias_map(b, h, t):
        ty = jnp.where(t == 0, 3, jnp.where(t == 1, 0, jnp.where(t == nqb, 2, 1)))
        return (h, ty, 0, 0)

    return pl.pallas_call(
        kern,
        out_shape=jax.ShapeDtypeStruct((rows, GROUP_W), BF16),
        grid=(nb, HEADS // hp, nblk_b),
        in_specs=[
            pl.BlockSpec((SUB, hw), lambda b, h, t: (b * nblk_b + t, C_NQ // hw + h)),
            pl.BlockSpec((rb, hw), lambda b, h, t: (b, C_NK // hw + h)),
            pl.BlockSpec((rb, hw), lambda b, h, t: (b, C_NV // hw + h)),
            pl.BlockSpec((hp, 1, SUB, bias.shape[-1]), bias_map),
        ],
        out_specs=pl.BlockSpec((SUB, hw), lambda b, h, t: (b * nblk_b + t, h)),
        compiler_params=_cparams(("parallel", "parallel", "arbitrary")),
        name="na",
    )(pq, pq, pq, bias)


def _gdn_prep_kernel(x_ref, prev_ref, next_ref, w_ref, g_ref, av_ref, dtb_ref, o_ref, og_ref, *, nblk_b):
    i = pl.program_id(0)
    t = i % nblk_b
    keep_prev = jnp.where((t == 0) | (t == 1), 0.0, 1.0)
    keep_next = jnp.where((t == 0) | (t == nblk_b - 1), 0.0, 1.0)
    x = x_ref[...].astype(F32)
    tm = x.shape[0]
    row = lax.broadcasted_iota(jnp.int32, x.shape, 0)
    halo_p = prev_ref[15:16, :].astype(F32) * keep_prev
    halo_n = next_ref[0:1, :].astype(F32) * keep_next
    xp = jnp.where(row == 0, halo_p, pltpu.roll(x, 1, 0))
    xn = jnp.where(row == tm - 1, halo_n, pltpu.roll(x, tm - 1, 0))
    c = xp * w_ref[0:1, :] + x * w_ref[1:2, :] + xn * w_ref[2:3, :]
    s = c * _sigmoid(c)
    for h in range(2 * HEADS):
        sh = s[:, h * 128:(h + 1) * 128]
        nrm = lax.rsqrt(jnp.sum(sh * sh, axis=-1, keepdims=True) + RMS_EPS)
        if h < HEADS:
            nrm = nrm * HEAD_DIM ** -0.5
        o_ref[:, h * 128:(h + 1) * 128] = (sh * nrm).astype(BF16)
    o_ref[:, 2 * GROUP_W:] = s[:, 2 * GROUP_W:].astype(BF16)
    g = g_ref[...]
    z = g + dtb_ref[...]
    softplus = jnp.maximum(z, 0.0) + jnp.log(1.0 + jnp.exp(-jnp.abs(z)))
    lane = lax.broadcasted_iota(jnp.int32, g.shape, 1)
    og_ref[...] = jnp.where(lane < 2 * HEADS, -av_ref[...] * softplus, _sigmoid(g))


def _gdn_prep(pq, pg, conv_w, a_vec, dtb_vec, *, nblk_b):
    rows = pq.shape[0]
    tm = SUB
    w3 = 3 * GROUP_W
    nhalo = rows // 16
    kern = functools.partial(_gdn_prep_kernel, nblk_b=nblk_b)
    return pl.pallas_call(
        kern,
        out_shape=(jax.ShapeDtypeStruct((rows, w3), BF16), jax.ShapeDtypeStruct((rows, 128), F32)),
        grid=(rows // tm,),
        in_specs=[
            pl.BlockSpec((tm, w3), lambda i: (i, 0)),
            pl.BlockSpec((16, w3), lambda i: (jnp.maximum(i * (tm // 16) - 1, 0), 0)),
            pl.BlockSpec((16, w3), lambda i: (jnp.minimum((i + 1) * (tm // 16), nhalo - 1), 0)),
            pl.BlockSpec((3, w3), lambda i: (0, 0)),
            pl.BlockSpec((tm, 128), lambda i: (i, 0)),
            pl.BlockSpec((1, 128), lambda i: (0, 0)),
            pl.BlockSpec((1, 128), lambda i: (0, 0)),
        ],
        out_specs=(pl.BlockSpec((tm, w3), lambda i: (i, 0)), pl.BlockSpec((tm, 128), lambda i: (i, 0))),
        compiler_params=_cparams(("parallel",)),
        name="gdn_prep",
    )(pq, pq, pq, conv_w, pg, a_vec, dtb_vec)


def _split3(x):
    hi = x.astype(BF16)
    r = x - hi.astype(F32)
    mid = r.astype(BF16)
    lo = (r - mid.astype(F32)).astype(BF16)
    return hi, mid, lo


def _gdn_chains(chains):
    c = chains[0][0].shape[0]
    row = lax.broadcasted_iota(jnp.int32, (c, c), 0)
    col = lax.broadcasted_iota(jnp.int32, (c, c), 1)
    xor = row ^ col
    eye = jnp.where(row == col, 1.0, 0.0)
    a_mats, qks, kbs, kfs, egs = [], [], [], [], []
    for q, k, v, gcol, grow, beta, glast, s_prev, lower in chains:
        incl = (row >= col) if lower else (row <= col)
        strict = (row > col) if lower else (row < col)
        decay = jnp.exp(jnp.where(incl, gcol - grow, NEG))
        kf = k.astype(F32)
        kb = kf * beta
        a_mats.append(jnp.where(strict, _dot_nt(kb.astype(BF16), k) * decay, 0.0))
        qks.append(jnp.where(incl, _dot_nt(q, k) * decay, 0.0).astype(BF16))
        kbs.append(kb)
        kfs.append(kf)
        egs.append(jnp.exp(gcol))
    tinvs = [eye - jnp.where(xor < 2, a, 0.0) for a in a_mats]
    s = 2
    while s < c:
        ys = [_dot(jnp.where((xor >= s) & (xor < 2 * s), a, 0.0).astype(BF16), t.astype(BF16))
              for a, t in zip(a_mats, tinvs)]
        tinvs = [t - _dot(t.astype(BF16), y.astype(BF16)) for t, y in zip(tinvs, ys)]
        s *= 2
    wus = []
    for (q, k, v, gcol, grow, beta, glast, s_prev, lower), kb, eg, t in zip(chains, kbs, egs, tinvs):
        rhs = jnp.concatenate([kb * eg, v.astype(F32) * beta], axis=1).astype(BF16)
        wus.append(_dot(t.astype(BF16), rhs))
    sbs = [ch[7].astype(BF16) for ch in chains]
    vnbs = [(wu[:, HEAD_DIM:] - _dot(wu[:, :HEAD_DIM].astype(BF16), sb)).astype(BF16) for wu, sb in zip(wus, sbs)]
    outs = []
    for (q, k, v, gcol, grow, beta, glast, s_prev, lower), kf, eg, qk, sb, vnb in zip(chains, kfs, egs, qks, sbs, vnbs):
        o = _dot((q.astype(F32) * eg).astype(BF16), sb) + _dot(qk, vnb)
        kd_t = (kf * jnp.exp(glast - gcol)).T.astype(BF16)
        s_new = s_prev * jnp.exp(glast) + _dot(kd_t, vnb)
        outs.append((o, s_new))
    return outs


def _gdn_kernel(xf_ref, xb_ref, gf_ref, gb_ref, of_ref, ob_ref, s_scr):
    st = pl.program_id(1)

    @pl.when(st == 0)
    def _():
        s_scr[...] = jnp.zeros_like(s_scr)

    c = GDN_CHUNK
    row = lax.broadcasted_iota(jnp.int32, (c, c), 0)
    col = lax.broadcasted_iota(jnp.int32, (c, c), 1)
    chains = []
    for d, (x_ref, g_ref) in enumerate(((xf_ref, gf_ref), (xb_ref, gb_ref))):
        lower = d == 0
        gates = g_ref[...]
        tri = jnp.where((row >= col) if lower else (row <= col), 1.0, 0.0).astype(BF16)
        hi, mid, lo = _split3(gates)
        gsum = _dot(tri, hi) + _dot(tri, mid) + _dot(tri, lo)
        gsum_t = gsum.T
        for h in range(HEADS):
            gi = d * HEADS + h
            bi = 2 * HEADS + gi
            glast = gsum[c - 1:c, gi:gi + 1] if lower else gsum[0:1, gi:gi + 1]
            chains.append((x_ref[:, h * 128:(h + 1) * 128],
                           x_ref[:, GROUP_W + h * 128:GROUP_W + (h + 1) * 128],
                           x_ref[:, 2 * GROUP_W + h * 128:2 * GROUP_W + (h + 1) * 128],
                           gsum[:, gi:gi + 1], gsum_t[gi:gi + 1, :], gates[:, bi:bi + 1], glast,
                           s_scr[d, h], lower))
    outs = _gdn_chains(chains)
    for idx, (o, s_new) in enumerate(outs):
        d, h = divmod(idx, HEADS)
        s_scr[d, h] = s_new
        (of_ref, ob_ref)[d][:, h * 128:(h + 1) * 128] = o


def _gdn(gx, gg, *, nb, rb):
    rows = gx.shape[0]
    c = GDN_CHUNK
    nch = rb // c
    nctx = SUB // c
    w3 = 3 * GROUP_W

    def fmap(b, s):
        return (b * nch + s, 0)

    def bmap(b, s):
        return (b * nch + jnp.where(s < nctx, nctx - 1 - s, nch + nctx - 1 - s), 0)

    return pl.pallas_call(
        _gdn_kernel,
        out_shape=(jax.ShapeDtypeStruct((rows, GROUP_W), F32), jax.ShapeDtypeStruct((rows, GROUP_W), F32)),
        grid=(nb, nch),
        in_specs=[
            pl.BlockSpec((c, w3), fmap), pl.BlockSpec((c, w3), bmap),
            pl.BlockSpec((c, 128), fmap), pl.BlockSpec((c, 128), bmap),
        ],
        out_specs=(pl.BlockSpec((c, GROUP_W), fmap), pl.BlockSpec((c, GROUP_W), bmap)),
        scratch_shapes=[pltpu.VMEM((2, HEADS, HEAD_DIM, HEAD_DIM), F32)],
        compiler_params=_cparams(("parallel", "arbitrary")),
        name="gdn",
    )(gx, gx, gg, gg)


def _gdn_finish_kernel(of_ref, ob_ref, z_ref, g_ref, o_ref):
    z = z_ref[...].astype(F32)
    gate = z * _sigmoid(z)
    for h in range(HEADS):
        sl = slice(h * 128, (h + 1) * 128)
        o = of_ref[:, sl] + ob_ref[:, sl]
        o = o * lax.rsqrt(jnp.mean(o * o, axis=-1, keepdims=True) + RMS_EPS) * g_ref[...]
        o_ref[:, sl] = (o * gate[:, sl]).astype(BF16)


def _gdn_finish(o_f, o_b, pq, norm_g):
    rows = o_f.shape[0]
    tm = 2 * SUB
    return pl.pallas_call(
        _gdn_finish_kernel,
        out_shape=jax.ShapeDtypeStruct((rows, GROUP_W), BF16),
        grid=(rows // tm,),
        in_specs=[
            pl.BlockSpec((tm, GROUP_W), lambda i: (i, 0)),
            pl.BlockSpec((tm, GROUP_W), lambda i: (i, 0)),
            pl.BlockSpec((tm, GROUP_W), lambda i: (i, C_GZ // GROUP_W)),
            pl.BlockSpec((1, HEAD_DIM), lambda i: (0, 0)),
        ],
        out_specs=pl.BlockSpec((tm, GROUP_W), lambda i: (i, 0)),
        compiler_params=_cparams(("parallel",)),
        name="gdn_finish",
    )(o_f, o_b, pq, norm_g)


def _layernorm_rows(z, g, b):
    mu = jnp.mean(z, axis=-1, keepdims=True)
    zc = z - mu
    var = jnp.mean(zc * zc, axis=-1, keepdims=True)
    return zc * lax.rsqrt(var + LN_EPS) * g + b


def _outproj_kernel(ya_ref, yb_ref, ym_ref, yn_ref, w_ref, x_ref, mod_ref, g_ref, b_ref, o_ref,
                    *, nsub, nblk_b, nb, d, alpha):
    i = pl.program_id(0)
    y = _dot(ya_ref[...], w_ref[0:GROUP_W, :])
    y = y + _dot(yb_ref[...], w_ref[GROUP_W:2 * GROUP_W, :])
    y = y + _dot(ym_ref[...], w_ref[2 * GROUP_W:3 * GROUP_W, :])
    y = y + _dot(yn_ref[...], w_ref[3 * GROUP_W:4 * GROUP_W, :])
    for s in range(nsub):
        g, _ = _group_of(i * nsub + s, nblk_b, nb)
        gate = mod_ref[0, pl.ds(g, 1), 2 * d:3 * d]
        sl = slice(s * SUB, (s + 1) * SUB)
        z = alpha * x_ref[sl, :] + gate * y[sl, :]
        o_ref[sl, :] = _layernorm_rows(z, g_ref[...], b_ref[...])


def _outproj(ys, w_out, xu, mod_l, ln_g, ln_b, *, nblk_b, nb, alpha):
    rows, d = xu.shape
    tm = 2 * SUB
    kern = functools.partial(_outproj_kernel, nsub=tm // SUB, nblk_b=nblk_b, nb=nb, d=d, alpha=alpha)
    yspec = pl.BlockSpec((tm, GROUP_W), lambda i: (i, 0))
    return pl.pallas_call(
        kern,
        out_shape=jax.ShapeDtypeStruct((rows, d), F32),
        grid=(rows // tm,),
        in_specs=[yspec, yspec, yspec, yspec,
                  pl.BlockSpec((4 * GROUP_W, d), lambda i: (0, 0)),
                  pl.BlockSpec((tm, d), lambda i: (i, 0)),
                  pl.BlockSpec((1, MOD_ROWS, 6 * d), lambda i: (0, 0, 0)),
                  pl.BlockSpec((1, d), lambda i: (0, 0)),
                  pl.BlockSpec((1, d), lambda i: (0, 0))],
        out_specs=pl.BlockSpec((tm, d), lambda i: (i, 0)),
        compiler_params=_cparams(("parallel",)),
        name="outproj",
    )(*ys, w_out, xu, mod_l, ln_g, ln_b)


def _ffn_kernel(x_ref, prev_ref, next_ref, mod_ref, wg_ref, wv_ref, cg_ref, cv_ref, wd_ref, g_ref, b_ref,
                o_ref, h_scr, hb_scr, acc_scr, *, nsub, nblk_b, nb, d, nj, alpha):
    i = pl.program_id(0)
    j = pl.program_id(1)
    tm = nsub * SUB

    @pl.when(j == 0)
    def _():
        acc_scr[...] = jnp.zeros_like(acc_scr)
        for s in range(nsub):
            g, _ = _group_of(i * nsub + s, nblk_b, nb)
            shift = mod_ref[0, pl.ds(g, 1), 3 * d:4 * d]
            scale = mod_ref[0, pl.ds(g, 1), 4 * d:5 * d]
            h_scr[8 + s * SUB:8 + (s + 1) * SUB, :] = x_ref[s * SUB:(s + 1) * SUB, :] * (1.0 + scale) + shift
            if s == 0:
                h_scr[0:8, :] = prev_ref[...] * (1.0 + scale) + shift
            if s == nsub - 1:
                h_scr[tm + 8:tm + 16, :] = next_ref[...] * (1.0 + scale) + shift
        hb_scr[...] = h_scr[...].astype(BF16)

    row = lax.broadcasted_iota(jnp.int32, (tm, 1), 0)
    keep_prev = jnp.ones((tm, 1), F32)
    keep_next = jnp.ones((tm, 1), F32)
    for s in range(nsub):
        _, t = _group_of(i * nsub + s, nblk_b, nb)
        first = (t == 0) | (t == 1)
        last = (t == 0) | (t == nblk_b - 1)
        keep_prev = jnp.where((row == s * SUB) & first, 0.0, keep_prev)
        keep_next = jnp.where((row == s * SUB + SUB - 1) & last, 0.0, keep_next)

    hb = hb_scr[...]
    ext = tm + 16

    def conv(u, cw_ref):
        up = pltpu.roll(u, 1, 0)[8:tm + 8, :] * keep_prev
        un = pltpu.roll(u, ext - 1, 0)[8:tm + 8, :] * keep_next
        return up * cw_ref[0:1, :] + u[8:tm + 8, :] * cw_ref[1:2, :] + un * cw_ref[2:3, :]

    gate = conv(_dot(hb, wg_ref[...]), cg_ref)
    val = conv(_dot(hb, wv_ref[...]), cv_ref)
    act = (gate * _sigmoid(gate) * val).astype(BF16)
    acc_scr[...] += _dot(act, wd_ref[...])

    @pl.when(j == nj - 1)
    def _():
        for s in range(nsub):
            g, _ = _group_of(i * nsub + s, nblk_b, nb)
            gt = mod_ref[0, pl.ds(g, 1), 5 * d:6 * d]
            sl = slice(s * SUB, (s + 1) * SUB)
            z = alpha * x_ref[sl, :] + gt * acc_scr[sl, :]
            o_ref[sl, :] = _layernorm_rows(z, g_ref[...], b_ref[...])


def _ffn(xu, mod_l, w_up, conv_w, w_down, ln_g, ln_b, *, nblk_b, nb, alpha):
    rows, d = xu.shape
    dff = w_down.shape[0]
    tm, tc = 2 * SUB, 512
    nj = dff // tc
    nhalo = rows // 8
    kern = functools.partial(_ffn_kernel, nsub=tm // SUB, nblk_b=nblk_b, nb=nb, d=d, nj=nj, alpha=alpha)
    return pl.pallas_call(
        kern,
        out_shape=jax.ShapeDtypeStruct((rows, d), F32),
        grid=(rows // tm, nj),
        in_specs=[
            pl.BlockSpec((tm, d), lambda i, j: (i, 0)),
            pl.BlockSpec((8, d), lambda i, j: (jnp.maximum(i * (tm // 8) - 1, 0), 0)),
            pl.BlockSpec((8, d), lambda i, j: (jnp.minimum((i + 1) * (tm // 8), nhalo - 1), 0)),
            pl.BlockSpec((1, MOD_ROWS, 6 * d), lambda i, j: (0, 0, 0)),
            pl.BlockSpec((d, tc), lambda i, j: (0, j)),
            pl.BlockSpec((d, tc), lambda i, j: (0, nj + j)),
            pl.BlockSpec((3, tc), lambda i, j: (0, j)),
            pl.BlockSpec((3, tc), lambda i, j: (0, nj + j)),
            pl.BlockSpec((tc, d), lambda i, j: (j, 0)),
            pl.BlockSpec((1, d), lambda i, j: (0, 0)),
            pl.BlockSpec((1, d), lambda i, j: (0, 0)),
        ],
        out_specs=pl.BlockSpec((tm, d), lambda i, j: (i, 0)),
        scratch_shapes=[pltpu.VMEM((tm + 16, d), F32), pltpu.VMEM((tm + 16, d), BF16), pltpu.VMEM((tm, d), F32)],
        compiler_params=_cparams(("parallel", "arbitrary")),
        name="ffn",
    )(xu, xu, xu, mod_l, w_up, w_up, conv_w, conv_w, w_down, ln_g, ln_b)


def _rope_tables(n_lat):
    pos = jnp.arange(n_lat)
    row = (pos // GRID_W).astype(F32)
    col = (pos % GRID_W).astype(F32)
    half = MLA_ROPE // 2
    inv = ROPE_THETA ** (-jnp.arange(0, half, 2, dtype=F32) / half)
    ar = row[:, None] * inv[None, :]
    ac = col[:, None] * inv[None, :]
    cos64 = jnp.concatenate([jnp.cos(ar), jnp.cos(ar), jnp.cos(ac), jnp.cos(ac)], axis=-1)
    sin64 = jnp.concatenate([-jnp.sin(ar), jnp.sin(ar), -jnp.sin(ac), jnp.sin(ac)], axis=-1)
    one = jnp.ones((n_lat, 64), F32)
    zero = jnp.zeros((n_lat, 64), F32)

    def with_ctx(c, s):
        c = jnp.concatenate([jnp.ones((SUB, 128), F32), c], axis=0)
        s = jnp.concatenate([jnp.zeros((SUB, 128), F32), s], axis=0)
        return c, s

    diff = with_ctx(jnp.concatenate([cos64, cos64], -1), jnp.concatenate([sin64, sin64], -1))
    mla = with_ctx(jnp.concatenate([cos64, one], -1), jnp.concatenate([sin64, zero], -1))
    return diff, mla


def _prep_w_in(w_in):
    depth, d, _ = w_in.shape
    z = lambda n: jnp.zeros((depth, d, n), w_in.dtype)
    o_diff = 4 * GROUP_W + 4 * HEADS
    o_mla = o_diff + 3 * GROUP_W
    o_na = o_mla + MLA_Q_RANK + MLA_KV_RANK + MLA_ROPE
    w = jnp.concatenate([w_in[..., :4 * GROUP_W], w_in[..., o_diff:o_mla], w_in[..., o_na:],
                         w_in[..., o_mla:o_na], z(MLA_W - (o_na - o_mla)),
                         w_in[..., 4 * GROUP_W:o_diff], z(128 - 4 * HEADS)], axis=-1)
    assert w.shape[-1] == N_PROJ
    return w.astype(BF16)


def _prep_mla_weights(w_uq, w_ukv):
    depth = w_uq.shape[0]
    dq = MLA_NOPE + MLA_ROPE
    wq = w_uq.reshape(depth, MLA_Q_RANK, HEADS, dq)
    wq = jnp.concatenate([wq, jnp.zeros((depth, MLA_Q_RANK, HEADS, 256 - dq), w_uq.dtype)], axis=-1)
    wq = wq.reshape(depth, MLA_Q_RANK, HEADS * 256)
    wkv = w_ukv.reshape(depth, MLA_KV_RANK, HEADS, MLA_NOPE + HEAD_DIM)
    wkv = jnp.concatenate([wkv[..., :MLA_NOPE].reshape(depth, MLA_KV_RANK, GROUP_W),
                           wkv[..., MLA_NOPE:].reshape(depth, MLA_KV_RANK, GROUP_W)], axis=-1)
    return wq.astype(BF16), wkv.astype(BF16)


def _pad_lanes(v, n=128):
    return jnp.concatenate([v, jnp.zeros(v.shape[:-1] + (n - v.shape[-1],), v.dtype)], axis=-1)


def _mixers(pq, pg, lw, l, tabs, *, nb, rb):
    nblk_b = rb // SUB
    (dcos, dsin), (mcos, msin) = tabs
    gx, gg = _gdn_prep(pq, pg, lw["gdn_conv"], lw["gdn_a"], lw["gdn_dtb"], nblk_b=nblk_b)
    o_f, o_b = _gdn(gx, gg, nb=nb, rb=rb)
    ya = _gdn_finish(o_f, o_b, pq, lw["gdn_norm_g"])
    qk_d, vt_d = _diff_prep(pq, dcos, dsin, nblk_b=nblk_b)
    yb = _flash(qk_d, qk_d, vt_d, nb=nb, rb=rb, dq=HEAD_DIM, qcol=0, kcol=GROUP_W, nsub=2,
                lam=lw["diff_lam"], norm_g=lw["diff_norm_g"])
    qm, km, vm = _mla_prep(pq, lw["mla_gq"], lw["mla_gkv"], lw["mla_wq"], lw["mla_wkv"], mcos, msin,
                           nblk_b=nblk_b)
    ym = _flash(qm, km, vm, nb=nb, rb=rb, dq=256, qcol=0, kcol=0)
    yn = _na(pq, lw["na_bias"], nb=nb, rb=rb)
    return ya, yb, ym, yn


def _layer(xu, lw, l, tabs, *, nb, rb, alpha):
    nblk_b = rb // SUB
    mod_l = lw["mod"]
    pq, pg = _inproj(xu, mod_l, lw["w_in"], nblk_b=nblk_b, nb=nb)
    ys = _mixers(pq, pg, lw, l, tabs, nb=nb, rb=rb)
    x1 = _outproj(ys, lw["w_out"], xu, mod_l, lw["ln_g0"], lw["ln_b0"], nblk_b=nblk_b, nb=nb, alpha=alpha)
    return _ffn(x1, mod_l, lw["ffn_w_up"], lw["ffn_conv"], lw["ffn_w_down"], lw["ln_g1"], lw["ln_b1"],
                nblk_b=nblk_b, nb=nb, alpha=alpha)


def _layer_weights(l, mod, w_in_p, gdn_conv, gdn_a_log, gdn_dt_bias, gdn_norm_g, diff_lambda, diff_norm_g,
                   mla_q_norm_g, mla_kv_norm_g, wq, wkv, na_bias, w_out_b, ln_g, ln_b, w_up_b, ffn_conv,
                   w_down_b):
    lf = diff_lambda[l].astype(F32)
    lam_init = 0.8 - 0.6 * math.exp(-0.3 * l)
    lam = jnp.exp(jnp.sum(lf[0] * lf[1])) - jnp.exp(jnp.sum(lf[2] * lf[3])) + lam_init
    return {
        "mod": mod[l:l + 1],
        "w_in": w_in_p[l],
        "gdn_conv": gdn_conv[l],
        "gdn_a": _pad_lanes(jnp.exp(gdn_a_log[l].astype(F32)).reshape(1, 2 * HEADS)),
        "gdn_dtb": _pad_lanes(gdn_dt_bias[l].astype(F32).reshape(1, 2 * HEADS)),
        "gdn_norm_g": gdn_norm_g[l].reshape(1, HEAD_DIM),
        "diff_lam": jnp.stack([lam, jnp.asarray(1.0 - lam_init, F32)]),
        "diff_norm_g": diff_norm_g[l].reshape(1, HEAD_DIM),
        "mla_gq": mla_q_norm_g[l].reshape(1, MLA_Q_RANK),
        "mla_gkv": mla_kv_norm_g[l].reshape(1, MLA_KV_RANK),
        "mla_wq": wq[l],
        "mla_wkv": wkv[l],
        "na_bias": na_bias[l],
        "w_out": w_out_b[l],
        "ln_g0": ln_g[l, 0:1], "ln_b0": ln_b[l, 0:1], "ln_g1": ln_g[l, 1:2], "ln_b1": ln_b[l, 1:2],
        "ffn_w_up": w_up_b[l],
        "ffn_conv": ffn_conv[l],
        "ffn_w_down": w_down_b[l],
    }


def kernel(x, c, ctx, c_ctx, w_mod, b_mod, w_in, gdn_conv, gdn_a_log, gdn_dt_bias, gdn_norm_g, diff_lambda,
           diff_norm_g, mla_q_norm_g, mla_kv_norm_g, mla_w_uq, mla_w_ukv, na_rpb, w_out, ln_g, ln_b,
           ffn_w_up, ffn_conv, ffn_w_down):
    nb, n_lat, d = x.shape
    depth = w_mod.shape[0]
    assert ctx.shape[1] == SUB and n_lat % (2 * SUB) == 0 and n_lat % GRID_W == 0
    assert (nb * (SUB + n_lat)) % (2 * SUB) == 0 and nb + 1 <= MOD_ROWS
    rb = SUB + n_lat
    alpha = (2 * depth) ** 0.25

    cvec = jnp.concatenate([c, c_ctx[None, :], jnp.zeros((MOD_ROWS - nb - 1, d), c.dtype)], axis=0)
    mod = _modulation(cvec, w_mod, b_mod)
    tabs = _rope_tables(n_lat)
    w_in_p = _prep_w_in(w_in)
    wq, wkv = _prep_mla_weights(mla_w_uq, mla_w_ukv)
    na_bias = _na_bias_table(na_rpb, n_lat // GRID_W)
    w_out_b = w_out.astype(BF16)
    w_up_b = ffn_w_up.astype(BF16)
    w_down_b = ffn_w_down.astype(BF16)

    xu = jnp.concatenate([ctx, x], axis=1).reshape(nb * rb, d)
    for l in range(depth):
        lw = _layer_weights(l, mod, w_in_p, gdn_conv, gdn_a_log, gdn_dt_bias, gdn_norm_g, diff_lambda,
                            diff_norm_g, mla_q_norm_g, mla_kv_norm_g, wq, wkv, na_bias, w_out_b, ln_g, ln_b,
                            w_up_b, ffn_conv, w_down_b)
        xu = _layer(xu, lw, l, tabs, nb=nb, rb=rb, alpha=alpha)
    return xu.reshape(nb, rb, d)[:, SUB:, :]
```

```python
import functools
import math

import numpy as np
import jax
import jax.numpy as jnp
from jax import lax
from jax.experimental import pallas as pl
from jax.experimental.pallas import tpu as pltpu

F32 = jnp.float32
BF16 = jnp.bfloat16

HEADS = 4
HEAD_DIM = 128
GROUP_W = HEADS * HEAD_DIM
GRID_W = 64
NA_KR = 8
NA_KC = 16
NA_BLOCK_ROWS = 4
NA_UNION_ROWS = NA_BLOCK_ROWS + NA_KR - 1
MLA_Q_RANK = 384
MLA_KV_RANK = 128
MLA_NOPE = 128
MLA_ROPE = 64
ROPE_THETA = 10000.0
LN_EPS = 1e-5
RMS_EPS = 1e-6
NEG = -1e30
LOG2E = math.log2(math.e)

SUB = 256
GDN_CHUNK = 128
MOD_ROWS = 16
VMEM_LIMIT = 56 * 1024 * 1024

C_GQ, C_GK, C_GV, C_GZ = 0, 512, 1024, 1536
C_DQ, C_DK, C_DV = 2048, 2560, 3072
C_NQ, C_NK, C_NV = 3584, 4096, 4608
C_MLA = 5120
MLA_W = 640
C_AB = 5760
N_PROJ = 5888


def _cparams(sem):
    return pltpu.CompilerParams(dimension_semantics=sem, vmem_limit_bytes=VMEM_LIMIT)


def _sigmoid(x):
    return 1.0 / (1.0 + jnp.exp(-x))


def _dot(a, b):
    return jnp.dot(a, b, preferred_element_type=F32)


def _dot_nt(a, b):
    return lax.dot_general(a, b, (((1,), (1,)), ((), ())), preferred_element_type=F32)


def _group_of(blk, nblk_b, nb):
    t = blk % nblk_b
    return jnp.where(t == 0, nb, blk // nblk_b), t


def _mod_kernel(c_ref, w_ref, b_ref, o_ref):
    c = c_ref[...]
    s = (c * _sigmoid(c)).astype(BF16)
    o_ref[0] = _dot(s, w_ref[0].astype(BF16)) + b_ref[0]


def _modulation(cvec, w_mod, b_mod):
    depth, d, n = w_mod.shape
    tn = 1536
    return pl.pallas_call(
        _mod_kernel,
        out_shape=jax.ShapeDtypeStruct((depth, MOD_ROWS, n), F32),
        grid=(depth, n // tn),
        in_specs=[
            pl.BlockSpec((MOD_ROWS, d), lambda l, j: (0, 0)),
            pl.BlockSpec((1, d, tn), lambda l, j: (l, 0, j)),
            pl.BlockSpec((1, 1, tn), lambda l, j: (l, 0, j)),
        ],
        out_specs=pl.BlockSpec((1, MOD_ROWS, tn), lambda l, j: (l, 0, j)),
        compiler_params=_cparams(("parallel", "parallel")),
        name="modulation",
    )(cvec, w_mod, b_mod.reshape(depth, 1, n))


def _inproj_kernel(x_ref, mod_ref, w_ref, o_ref, og_ref, h_scr, *, nsub, nblk_b, nb, d, nj):
    i = pl.program_id(0)
    j = pl.program_id(1)

    @pl.when(j == 0)
    def _():
        for s in range(nsub):
            g, _ = _group_of(i * nsub + s, nblk_b, nb)
            shift = mod_ref[0, pl.ds(g, 1), 0:d]
            scale = mod_ref[0, pl.ds(g, 1), d:2 * d]
            xs = x_ref[s * SUB:(s + 1) * SUB, :]
            h_scr[s * SUB:(s + 1) * SUB, :] = (xs * (1.0 + scale) + shift).astype(BF16)

    acc = _dot(h_scr[...], w_ref[...])
    o_ref[...] = acc.astype(BF16)

    @pl.when(j == nj - 1)
    def _():
        og_ref[...] = acc[:, acc.shape[1] - 128:]


def _inproj(xu, mod, w_in_p, l, *, nblk_b, nb):
    rows, d = xu.shape
    n = w_in_p.shape[2]
    tm, tn = 2 * SUB, n // 2
    nj = n // tn
    kern = functools.partial(_inproj_kernel, nsub=tm // SUB, nblk_b=nblk_b, nb=nb, d=d, nj=nj)
    return pl.pallas_call(
        kern,
        out_shape=(jax.ShapeDtypeStruct((rows, n), BF16), jax.ShapeDtypeStruct((rows, 128), F32)),
        grid=(rows // tm, nj),
        in_specs=[
            pl.BlockSpec((tm, d), lambda i, j: (i, 0)),
            pl.BlockSpec((1, MOD_ROWS, 6 * d), lambda i, j: (l, 0, 0)),
            pl.BlockSpec((None, d, tn), lambda i, j: (l, 0, j)),
        ],
        out_specs=(pl.BlockSpec((tm, tn), lambda i, j: (i, j)),
                   pl.BlockSpec((tm, 128), lambda i, j: (i, 0))),
        scratch_shapes=[pltpu.VMEM((tm, d), BF16)],
        compiler_params=_cparams(("parallel", "arbitrary")),
        name="inproj",
    )(xu, mod, w_in_p)


def _rope_slab(x, cos, sin):
    lane = lax.broadcasted_iota(jnp.int32, x.shape, 1)
    odd = (lane & 16) != 0
    partner = jnp.where(odd, pltpu.roll(x, 16, 1), pltpu.roll(x, 128 - 16, 1))
    return x * cos + partner * sin


def _diff_prep_kernel(p_ref, v_ref, cos_ref, sin_ref, o_ref, vt_ref, *, scale):
    vt_ref[0] = v_ref[...].astype(F32).T.astype(BF16)
    cos = cos_ref[...]
    sin = sin_ref[...]
    for s in range(2 * HEADS):
        x = p_ref[:, s * 128:(s + 1) * 128].astype(F32)
        r = _rope_slab(x, cos, sin)
        if s < HEADS:
            r = r * scale
        o_ref[:, s * 128:(s + 1) * 128] = r.astype(BF16)


def _diff_prep(pq, cos, sin, *, nblk_b):
    rows = pq.shape[0]
    tm = SUB
    kern = functools.partial(_diff_prep_kernel, scale=(HEAD_DIM // 2) ** -0.5 * LOG2E)
    return pl.pallas_call(
        kern,
        out_shape=(jax.ShapeDtypeStruct((rows, 2 * GROUP_W), BF16),
                   jax.ShapeDtypeStruct((rows // tm, GROUP_W, tm), BF16)),
        grid=(rows // tm,),
        in_specs=[
            pl.BlockSpec((tm, 2 * GROUP_W), lambda i: (i, C_DQ // (2 * GROUP_W))),
            pl.BlockSpec((tm, GROUP_W), lambda i: (i, C_DV // GROUP_W)),
            pl.BlockSpec((tm, 128), lambda i: (i % nblk_b, 0)),
            pl.BlockSpec((tm, 128), lambda i: (i % nblk_b, 0)),
        ],
        out_specs=(pl.BlockSpec((tm, 2 * GROUP_W), lambda i: (i, 0)),
                   pl.BlockSpec((1, GROUP_W, tm), lambda i: (i, 0, 0))),
        compiler_params=_cparams(("parallel",)),
        name="diff_prep",
    )(pq, pq, cos, sin)


def _mla_prep_kernel(p_ref, gq_ref, gkv_ref, wq_ref, wkv_ref, cos_ref, sin_ref,
                     q_ref, k_ref, v_ref, *, scale):
    cos = cos_ref[...]
    sin = sin_ref[...]
    cq = p_ref[:, 0:MLA_Q_RANK].astype(F32)
    cq = cq * lax.rsqrt(jnp.mean(cq * cq, axis=-1, keepdims=True) + RMS_EPS) * gq_ref[...]
    q = _dot(cq.astype(BF16), wq_ref[...])
    ckv = p_ref[:, MLA_Q_RANK:MLA_Q_RANK + MLA_KV_RANK].astype(F32)
    ckv = ckv * lax.rsqrt(jnp.mean(ckv * ckv, axis=-1, keepdims=True) + RMS_EPS) * gkv_ref[...]
    kv = _dot(ckv.astype(BF16), wkv_ref[...])
    kr = _rope_slab(p_ref[:, MLA_Q_RANK + MLA_KV_RANK:MLA_W].astype(F32), cos, sin).astype(BF16)
    for h in range(HEADS):
        q_ref[:, h * 256:h * 256 + 128] = (q[:, h * 256:h * 256 + 128] * scale).astype(BF16)
        qr = _rope_slab(q[:, h * 256 + 128:(h + 1) * 256], cos, sin)
        q_ref[:, h * 256 + 128:(h + 1) * 256] = (qr * scale).astype(BF16)
        k_ref[:, h * 256:h * 256 + 128] = kv[:, h * 128:(h + 1) * 128].astype(BF16)
        k_ref[:, h * 256 + 128:(h + 1) * 256] = kr
    v_ref[0] = kv[:, GROUP_W:].T.astype(BF16)


def _mla_prep(pq, gq, gkv, wq, wkv, cos, sin, *, nblk_b):
    rows = pq.shape[0]
    tm = SUB
    kern = functools.partial(_mla_prep_kernel, scale=(MLA_NOPE + MLA_ROPE) ** -0.5 * LOG2E)
    return pl.pallas_call(
        kern,
        out_shape=(jax.ShapeDtypeStruct((rows, HEADS * 256), BF16),
                   jax.ShapeDtypeStruct((rows, HEADS * 256), BF16),
                   jax.ShapeDtypeStruct((rows // tm, GROUP_W, tm), BF16)),
        grid=(rows // tm,),
        in_specs=[
            pl.BlockSpec((tm, MLA_W), lambda i: (i, C_MLA // MLA_W)),
            pl.BlockSpec((1, MLA_Q_RANK), lambda i: (0, 0)),
            pl.BlockSpec((1, MLA_KV_RANK), lambda i: (0, 0)),
            pl.BlockSpec((MLA_Q_RANK, HEADS * 256), lambda i: (0, 0)),
            pl.BlockSpec((MLA_KV_RANK, 2 * GROUP_W), lambda i: (0, 0)),
            pl.BlockSpec((tm, 128), lambda i: (i % nblk_b, 0)),
            pl.BlockSpec((tm, 128), lambda i: (i % nblk_b, 0)),
        ],
        out_specs=(pl.BlockSpec((tm, HEADS * 256), lambda i: (i, 0)),
                   pl.BlockSpec((tm, HEADS * 256), lambda i: (i, 0)),
                   pl.BlockSpec((1, GROUP_W, tm), lambda i: (i, 0, 0))),
        compiler_params=_cparams(("parallel",)),
        name="mla_prep",
    )(pq, gq, gkv, wq, wkv, cos, sin)


def _flash_kernel(*refs, nsub, hp, dq, n_lat_pairs):
    if nsub == 2:
        lam_ref, g_ref, q_ref, k_ref, vt_ref, o_ref, s_scr, p_scr, acc_scr = refs
    else:
        q_ref, k_ref, vt_ref, o_ref, s_scr, p_scr, acc_scr = refs
    qt = pl.program_id(2)
    tq = q_ref.shape[0]
    tk = 2 * SUB
    qs = []
    for hh in range(hp):
        q = q_ref[:, hh * dq:(hh + 1) * dq]
        if nsub == 2:
            lane = lax.broadcasted_iota(jnp.int32, q.shape, 1)
            zero = jnp.zeros_like(q)
            qs.append((hh, jnp.where(lane < 64, q, zero)))
            qs.append((hh, jnp.where(lane >= 64, q, zero)))
        else:
            qs.append((hh, q))
    nch = len(qs)

    def scores_into(slot, rows, nrows):
        maxes = []
        for c, (hh, qi) in enumerate(qs):
            s = _dot_nt(k_ref[rows, hh * dq:(hh + 1) * dq], qi)
            s_scr[slot, c, 0:nrows, :] = s
            maxes.append(jnp.max(s, axis=0, keepdims=True))
        return maxes

    def pv_from(slot, vblocks):
        for c, (hh, _) in enumerate(qs):
            pv = None
            for i, blk in enumerate(vblocks):
                part = _dot(vt_ref[blk, hh * HEAD_DIM:(hh + 1) * HEAD_DIM, :],
                            p_scr[slot, c, i * SUB:(i + 1) * SUB, :])
                pv = part if pv is None else pv + part
            acc_scr[c] += pv

    def softmax(slot, nrows, stats, maxes):
        ch = 64
        new = []
        for c, ((m, l), mb) in enumerate(zip(stats, maxes)):
            m_new = jnp.maximum(m, mb)
            psum = None
            for r in range(0, nrows, ch):
                p = jnp.exp2(s_scr[slot, c, r:r + ch, :] - m_new)
                p_scr[slot, c, r:r + ch, :] = p.astype(BF16)
                psum = p if psum is None else psum + p
            alpha = jnp.exp2(m - m_new)
            acc_scr[c] = acc_scr[c] * alpha
            new.append((m_new, alpha * l + jnp.sum(psum, axis=0, keepdims=True)))
        return new

    def lat_rows(j):
        return pl.ds(pl.multiple_of(SUB + j * tk, SUB), tk)

    acc_scr[...] = jnp.zeros_like(acc_scr)
    stats = [(jnp.full((1, tq), NEG, F32), jnp.zeros((1, tq), F32)) for _ in qs]
    mx_ctx = scores_into(1, slice(0, SUB), SUB)
    mx0 = scores_into(0, lat_rows(0), tk)
    p_scr[1, :, SUB:, :] = jnp.zeros((nch, SUB, tq), BF16)
    stats = softmax(1, SUB, stats, mx_ctx)
    unroll = 2 if n_lat_pairs % 2 == 0 else 1
    n = jnp.where(qt == 0, 0, n_lat_pairs // unroll)
    last = 2 * n_lat_pairs - 1

    def pair(a, stats, mx0):
        bp = jnp.where(a == 0, 0, 2 * a - 1)
        pv_from(1, (bp, bp + 1))
        mx1 = scores_into(1, lat_rows(a + 1), tk)
        stats = softmax(0, tk, stats, mx0)
        pv_from(0, (1 + 2 * a, 2 + 2 * a))
        mx0 = scores_into(0, lat_rows(jnp.minimum(a + 2, last)), tk)
        return softmax(1, tk, stats, mx1), mx0

    def body(jj, carry):
        stats = tuple(zip(carry[0:2 * nch:2], carry[1:2 * nch:2]))
        mx0 = list(carry[2 * nch:])
        for u in range(unroll):
            stats, mx0 = pair(2 * (jj * unroll + u), stats, mx0)
        return tuple(x for st in stats for x in st) + tuple(mx0)

    flat = lax.fori_loop(0, n, body, tuple(x for st in stats for x in st) + tuple(mx0))
    stats = tuple(zip(flat[0:2 * nch:2], flat[1:2 * nch:2]))
    done = 2 * unroll * n
    bl = jnp.where(done == 0, 0, 2 * done - 1)
    pv_from(1, (bl, bl + 1))

    for hh in range(hp):
        if nsub == 2:
            o1 = acc_scr[2 * hh] / stats[2 * hh][1]
            o2 = acc_scr[2 * hh + 1] / stats[2 * hh + 1][1]
            o = (o1 - lam_ref[0] * o2).T
            o = o * lax.rsqrt(jnp.mean(o * o, axis=-1, keepdims=True) + RMS_EPS) * g_ref[...] * lam_ref[1]
        else:
            o = (acc_scr[hh] / stats[hh][1]).T
        o_ref[:, hh * HEAD_DIM:(hh + 1) * HEAD_DIM] = o.astype(o_ref.dtype)


def _flash(q_arr, k_arr, vt_arr, *, nb, rb, dq, qcol, kcol, nsub=1, lam=None, norm_g=None):
    rows = q_arr.shape[0]
    nblk_b = rb // SUB
    lat = rb - SUB
    hp = 2
    assert lat % (2 * SUB) == 0 and qcol % (hp * dq) == 0 and kcol % (hp * dq) == 0
    assert lat % (4 * SUB) == 0
    nch = hp * nsub
    kern = functools.partial(_flash_kernel, nsub=nsub, hp=hp, dq=dq, n_lat_pairs=lat // (4 * SUB))
    in_specs = [
        pl.BlockSpec((SUB, hp * dq), lambda b, h, t: (b * nblk_b + t, qcol // (hp * dq) + h)),
        pl.BlockSpec((rb, hp * dq), lambda b, h, t: (b, kcol // (hp * dq) + h)),
        pl.BlockSpec((nblk_b, hp * HEAD_DIM, SUB), lambda b, h, t: (b, h, 0)),
    ]
    args = [q_arr, k_arr, vt_arr]
    if nsub == 2:
        in_specs = [pl.BlockSpec(memory_space=pltpu.SMEM),
                    pl.BlockSpec((1, HEAD_DIM), lambda b, h, t: (0, 0))] + in_specs
        args = [lam, norm_g] + args
    return pl.pallas_call(
        kern,
        out_shape=jax.ShapeDtypeStruct((rows, GROUP_W), BF16),
        grid=(nb, HEADS // hp, nblk_b),
        in_specs=in_specs,
        out_specs=pl.BlockSpec((SUB, hp * HEAD_DIM), lambda b, h, t: (b * nblk_b + t, h)),
        scratch_shapes=[pltpu.VMEM((2, nch, 2 * SUB, SUB), F32), pltpu.VMEM((2, nch, 2 * SUB, SUB), BF16),
                        pltpu.VMEM((nch, HEAD_DIM, SUB), F32)],
        compiler_params=_cparams(("parallel", "parallel", "arbitrary")),
        name="flash_diff" if nsub == 2 else "flash_mla",
    )(*args)


def _na_kernel(q_ref, k_ref, v_ref, bias_ref, o_ref, *, grid_rows, scale, hp):
    qt = pl.program_id(2)
    qb = jnp.maximum(qt - 1, 0)
    u0 = jnp.clip(qb * NA_BLOCK_ROWS - NA_KR // 2, 0, grid_rows - NA_UNION_ROWS)
    start = pl.multiple_of(SUB + u0 * GRID_W, GRID_W)
    nn = NA_UNION_ROWS * GRID_W
    hs = [slice(h * HEAD_DIM, (h + 1) * HEAD_DIM) for h in range(hp)]
    qs = [(q_ref[:, sl].astype(F32) * scale).astype(BF16) for sl in hs]
    s_cs = [_dot_nt(q, k_ref[0:SUB, sl]) for q, sl in zip(qs, hs)]
    s_ns = [_dot_nt(q, k_ref[pl.ds(start, nn), sl]) + bias_ref[h, 0] for h, (q, sl) in enumerate(zip(qs, hs))]
    ms = [jnp.maximum(jnp.max(s_c, axis=-1, keepdims=True), jnp.max(s_n, axis=-1, keepdims=True))
          for s_c, s_n in zip(s_cs, s_ns)]
    e_cs = [jnp.exp2(s_c - m) for s_c, m in zip(s_cs, ms)]
    e_ns = [jnp.exp2(s_n - m) for s_n, m in zip(s_ns, ms)]
    for sl, e_c, e_n in zip(hs, e_cs, e_ns):
        l = jnp.sum(e_c, axis=-1, keepdims=True) + jnp.sum(e_n, axis=-1, keepdims=True)
        o = (_dot(e_c.astype(BF16), v_ref[0:SUB, sl]) + _dot(e_n.astype(BF16), v_ref[pl.ds(start, nn), sl])) / l
        o_ref[:, sl] = o.astype(o_ref.dtype)


def _na_bias_table(rpb, grid_rows):
    assert grid_rows >= NA_UNION_ROWS + NA_BLOCK_ROWS
    j = np.arange(NA_BLOCK_ROWS)[:, None]
    iu = np.arange(NA_UNION_ROWS)[None, :]
    cq = np.arange(GRID_W)[:, None]
    ck = np.arange(GRID_W)[None, :]
    cs = np.clip(cq - NA_KC // 2, 0, GRID_W - NA_KC)
    col_ok = (ck >= cs) & (ck < cs + NA_KC)
    dx = np.clip(ck - cq + NA_KC - 1, 0, 2 * NA_KC - 2)
    ex = np.eye(2 * NA_KC - 1, dtype=np.float32)[dx]
    eys, oks = [], []
    for r_off, w_off in ((j, 0 * j), (j + NA_KR // 2, j),
                         (j + NA_UNION_ROWS - NA_BLOCK_ROWS, 0 * j + NA_UNION_ROWS - NA_KR)):
        dy = np.clip(iu - r_off + NA_KR - 1, 0, 2 * NA_KR - 2)
        eys.append(np.eye(2 * NA_KR - 1, dtype=np.float32)[dy])
        oks.append((iu - w_off >= 0) & (iu - w_off < NA_KR))
    ey = np.stack(eys)
    ok = np.stack(oks)[:, :, None, :, None] & col_ok[None, None, :, None, :]
    vals = jnp.einsum("tjia,lhab,qkb->lhtjqik", ey, rpb.astype(F32), ex, precision=lax.Precision.HIGHEST)
    nb_part = jnp.where(ok[None, None], vals * LOG2E, NEG)
    nb_part = jnp.concatenate([nb_part, jnp.full_like(nb_part[:, :, :1], NEG)], axis=2)
    return nb_part.reshape(rpb.shape[:2] + (4, NA_BLOCK_ROWS * GRID_W, NA_UNION_ROWS * GRID_W))


def _na(pq, bias, l, *, nb, rb):
    rows = pq.shape[0]
    nblk_b = rb // SUB
    grid_rows = (rb - SUB) // GRID_W
    nqb = nblk_b - 1
    hp = 2
    hw = hp * HEAD_DIM
    kern = functools.partial(_na_kernel, grid_rows=grid_rows, scale=HEAD_DIM ** -0.5 * LOG2E, hp=hp)

    def bias_map(b, h, t):
        ty = jnp.where(t == 0, 3, jnp.where(t == 1, 0, jnp.where(t == nqb, 2, 1)))
        return (l, h, ty, 0, 0)

    return pl.pallas_call(
        kern,
        out_shape=jax.ShapeDtypeStruct((rows, GROUP_W), BF16),
        grid=(nb, HEADS // hp, nblk_b),
        in_specs=[
            pl.BlockSpec((SUB, hw), lambda b, h, t: (b * nblk_b + t, C_NQ // hw + h)),
            pl.BlockSpec((rb, hw), lambda b, h, t: (b, C_NK // hw + h)),
            pl.BlockSpec((rb, hw), lambda b, h, t: (b, C_NV // hw + h)),
            pl.BlockSpec((None, hp, 1, SUB, bias.shape[-1]), bias_map),
        ],
        out_specs=pl.BlockSpec((SUB, hw), lambda b, h, t: (b * nblk_b + t, h)),
        compiler_params=_cparams(("parallel", "parallel", "arbitrary")),
        name="na",
    )(pq, pq, pq, bias)


def _gdn_prep_kernel(x_ref, prev_ref, next_ref, w_ref, g_ref, av_ref, dtb_ref, o_ref, og_ref, *, nblk_b):
    i = pl.program_id(0)
    t = i % nblk_b
    keep_prev = jnp.where((t == 0) | (t == 1), 0.0, 1.0)
    keep_next = jnp.where((t == 0) | (t == nblk_b - 1), 0.0, 1.0)
    x = x_ref[...].astype(F32)
    tm = x.shape[0]
    row = lax.broadcasted_iota(jnp.int32, x.shape, 0)
    halo_p = prev_ref[15:16, :].astype(F32) * keep_prev
    halo_n = next_ref[0:1, :].astype(F32) * keep_next
    xp = jnp.where(row == 0, halo_p, pltpu.roll(x, 1, 0))
    xn = jnp.where(row == tm - 1, halo_n, pltpu.roll(x, tm - 1, 0))
    c = xp * w_ref[0:1, :] + x * w_ref[1:2, :] + xn * w_ref[2:3, :]
    s = c * _sigmoid(c)
    for h in range(2 * HEADS):
        sh = s[:, h * 128:(h + 1) * 128]
        nrm = lax.rsqrt(jnp.sum(sh * sh, axis=-1, keepdims=True) + RMS_EPS)
        if h < HEADS:
            nrm = nrm * HEAD_DIM ** -0.5
        o_ref[:, h * 128:(h + 1) * 128] = (sh * nrm).astype(BF16)
    o_ref[:, 2 * GROUP_W:] = s[:, 2 * GROUP_W:].astype(BF16)
    g = g_ref[...]
    z = g + dtb_ref[...]
    softplus = jnp.maximum(z, 0.0) + jnp.log(1.0 + jnp.exp(-jnp.abs(z)))
    lane = lax.broadcasted_iota(jnp.int32, g.shape, 1)
    og_ref[...] = jnp.where(lane < 2 * HEADS, -av_ref[...] * softplus, _sigmoid(g))


def _gdn_prep(pq, pg, conv_w, a_vec, dtb_vec, *, nblk_b):
    rows = pq.shape[0]
    tm = SUB
    w3 = 3 * GROUP_W
    nhalo = rows // 16
    kern = functools.partial(_gdn_prep_kernel, nblk_b=nblk_b)
    return pl.pallas_call(
        kern,
        out_shape=(jax.ShapeDtypeStruct((rows, w3), BF16), jax.ShapeDtypeStruct((rows, 128), F32)),
        grid=(rows // tm,),
        in_specs=[
            pl.BlockSpec((tm, w3), lambda i: (i, 0)),
            pl.BlockSpec((16, w3), lambda i: (jnp.maximum(i * (tm // 16) - 1, 0), 0)),
            pl.BlockSpec((16, w3), lambda i: (jnp.minimum((i + 1) * (tm // 16), nhalo - 1), 0)),
            pl.BlockSpec((3, w3), lambda i: (0, 0)),
            pl.BlockSpec((tm, 128), lambda i: (i, 0)),
            pl.BlockSpec((1, 128), lambda i: (0, 0)),
            pl.BlockSpec((1, 128), lambda i: (0, 0)),
        ],
        out_specs=(pl.BlockSpec((tm, w3), lambda i: (i, 0)), pl.BlockSpec((tm, 128), lambda i: (i, 0))),
        compiler_params=_cparams(("parallel",)),
        name="gdn_prep",
    )(pq, pq, pq, conv_w, pg, a_vec, dtb_vec)


def _split3(x):
    hi = x.astype(BF16)
    r = x - hi.astype(F32)
    mid = r.astype(BF16)
    lo = (r - mid.astype(F32)).astype(BF16)
    return hi, mid, lo


def _gdn_chains(chains):
    c = chains[0][0].shape[0]
    row = lax.broadcasted_iota(jnp.int32, (c, c), 0)
    col = lax.broadcasted_iota(jnp.int32, (c, c), 1)
    xor = row ^ col
    eye = jnp.where(row == col, 1.0, 0.0)
    a_mats, qks, kbs, kfs, egs = [], [], [], [], []
    for q, k, v, gcol, grow, beta, glast, s_prev, lower in chains:
        incl = (row >= col) if lower else (row <= col)
        strict = (row > col) if lower else (row < col)
        decay = jnp.exp(jnp.where(incl, gcol - grow, NEG))
        kf = k.astype(F32)
        kb = kf * beta
        a_mats.append(jnp.where(strict, _dot_nt(kb.astype(BF16), k) * decay, 0.0))
        qks.append(jnp.where(incl, _dot_nt(q, k) * decay, 0.0).astype(BF16))
        kbs.append(kb)
        kfs.append(kf)
        egs.append(jnp.exp(gcol))
    tinvs = [eye - jnp.where(xor < 2, a, 0.0) for a in a_mats]
    s = 2
    while s < c:
        ys = [_dot(jnp.where((xor >= s) & (xor < 2 * s), a, 0.0).astype(BF16), t.astype(BF16))
              for a, t in zip(a_mats, tinvs)]
        tinvs = [t - _dot(t.astype(BF16), y.astype(BF16)) for t, y in zip(tinvs, ys)]
        s *= 2
    wus = []
    for (q, k, v, gcol, grow, beta, glast, s_prev, lower), kb, eg, t in zip(chains, kbs, egs, tinvs):
        rhs = jnp.concatenate([kb * eg, v.astype(F32) * beta], axis=1).astype(BF16)
        wus.append(_dot(t.astype(BF16), rhs))
    sbs = [ch[7].astype(BF16) for ch in chains]
    vnbs = [(wu[:, HEAD_DIM:] - _dot(wu[:, :HEAD_DIM].astype(BF16), sb)).astype(BF16) for wu, sb in zip(wus, sbs)]
    outs = []
    for (q, k, v, gcol, grow, beta, glast, s_prev, lower), kf, eg, qk, sb, vnb in zip(chains, kfs, egs, qks, sbs, vnbs):
        o = _dot((q.astype(F32) * eg).astype(BF16), sb) + _dot(qk, vnb)
        kd_t = (kf * jnp.exp(glast - gcol)).T.astype(BF16)
        s_new = s_prev * jnp.exp(glast) + _dot(kd_t, vnb)
        outs.append((o, s_new))
    return outs


def _gdn_kernel(xf_ref, xb_ref, gf_ref, gb_ref, of_ref, ob_ref, s_scr):
    st = pl.program_id(1)

    @pl.when(st == 0)
    def _():
        s_scr[...] = jnp.zeros_like(s_scr)

    c = GDN_CHUNK
    row = lax.broadcasted_iota(jnp.int32, (c, c), 0)
    col = lax.broadcasted_iota(jnp.int32, (c, c), 1)
    chains = []
    for d, (x_ref, g_ref) in enumerate(((xf_ref, gf_ref), (xb_ref, gb_ref))):
        lower = d == 0
        gates = g_ref[...]
        tri = jnp.where((row >= col) if lower else (row <= col), 1.0, 0.0).astype(BF16)
        hi, mid, lo = _split3(gates)
        gsum = _dot(tri, hi) + _dot(tri, mid) + _dot(tri, lo)
        gsum_t = gsum.T
        for h in range(HEADS):
            gi = d * HEADS + h
            bi = 2 * HEADS + gi
            glast = gsum[c - 1:c, gi:gi + 1] if lower else gsum[0:1, gi:gi + 1]
            chains.append((x_ref[:, h * 128:(h + 1) * 128],
                           x_ref[:, GROUP_W + h * 128:GROUP_W + (h + 1) * 128],
                           x_ref[:, 2 * GROUP_W + h * 128:2 * GROUP_W + (h + 1) * 128],
                           gsum[:, gi:gi + 1], gsum_t[gi:gi + 1, :], gates[:, bi:bi + 1], glast,
                           s_scr[d, h], lower))
    outs = _gdn_chains(chains)
    for idx, (o, s_new) in enumerate(outs):
        d, h = divmod(idx, HEADS)
        s_scr[d, h] = s_new
        (of_ref, ob_ref)[d][:, h * 128:(h + 1) * 128] = o


def _gdn(gx, gg, *, nb, rb):
    rows = gx.shape[0]
    c = GDN_CHUNK
    nch = rb // c
    nctx = SUB // c
    w3 = 3 * GROUP_W

    def fmap(b, s):
        return (b * nch + s, 0)

    def bmap(b, s):
        return (b * nch + jnp.where(s < nctx, nctx - 1 - s, nch + nctx - 1 - s), 0)

    return pl.pallas_call(
        _gdn_kernel,
        out_shape=(jax.ShapeDtypeStruct((rows, GROUP_W), F32), jax.ShapeDtypeStruct((rows, GROUP_W), F32)),
        grid=(nb, nch),
        in_specs=[
            pl.BlockSpec((c, w3), fmap), pl.BlockSpec((c, w3), bmap),
            pl.BlockSpec((c, 128), fmap), pl.BlockSpec((c, 128), bmap),
        ],
        out_specs=(pl.BlockSpec((c, GROUP_W), fmap), pl.BlockSpec((c, GROUP_W), bmap)),
        scratch_shapes=[pltpu.VMEM((2, HEADS, HEAD_DIM, HEAD_DIM), F32)],
        compiler_params=_cparams(("parallel", "arbitrary")),
        name="gdn",
    )(gx, gx, gg, gg)


def _gdn_finish_kernel(of_ref, ob_ref, z_ref, g_ref, o_ref):
    z = z_ref[...].astype(F32)
    gate = z * _sigmoid(z)
    for h in range(HEADS):
        sl = slice(h * 128, (h + 1) * 128)
        o = of_ref[:, sl] + ob_ref[:, sl]
        o = o * lax.rsqrt(jnp.mean(o * o, axis=-1, keepdims=True) + RMS_EPS) * g_ref[...]
        o_ref[:, sl] = (o * gate[:, sl]).astype(BF16)


def _gdn_finish(o_f, o_b, pq, norm_g):
    rows = o_f.shape[0]
    tm = 2 * SUB
    return pl.pallas_call(
        _gdn_finish_kernel,
        out_shape=jax.ShapeDtypeStruct((rows, GROUP_W), BF16),
        grid=(rows // tm,),
        in_specs=[
            pl.BlockSpec((tm, GROUP_W), lambda i: (i, 0)),
            pl.BlockSpec((tm, GROUP_W), lambda i: (i, 0)),
            pl.BlockSpec((tm, GROUP_W), lambda i: (i, C_GZ // GROUP_W)),
            pl.BlockSpec((1, HEAD_DIM), lambda i: (0, 0)),
        ],
        out_specs=pl.BlockSpec((tm, GROUP_W), lambda i: (i, 0)),
        compiler_params=_cparams(("parallel",)),
        name="gdn_finish",
    )(o_f, o_b, pq, norm_g)


def _layernorm_rows(z, g, b):
    mu = jnp.mean(z, axis=-1, keepdims=True)
    zc = z - mu
    var = jnp.mean(zc * zc, axis=-1, keepdims=True)
    return zc * lax.rsqrt(var + LN_EPS) * g + b


def _outproj_kernel(ya_ref, yb_ref, ym_ref, yn_ref, w_ref, x_ref, mod_ref, g_ref, b_ref, o_ref,
                    *, nsub, nblk_b, nb, d, alpha):
    i = pl.program_id(0)
    y = _dot(ya_ref[...], w_ref[0:GROUP_W, :])
    y = y + _dot(yb_ref[...], w_ref[GROUP_W:2 * GROUP_W, :])
    y = y + _dot(ym_ref[...], w_ref[2 * GROUP_W:3 * GROUP_W, :])
    y = y + _dot(yn_ref[...], w_ref[3 * GROUP_W:4 * GROUP_W, :])
    for s in range(nsub):
        g, _ = _group_of(i * nsub + s, nblk_b, nb)
        gate = mod_ref[0, pl.ds(g, 1), 2 * d:3 * d]
        sl = slice(s * SUB, (s + 1) * SUB)
        z = alpha * x_ref[sl, :] + gate * y[sl, :]
        o_ref[sl, :] = _layernorm_rows(z, g_ref[...], b_ref[...])


def _outproj(ys, w_out, xu, mod, ln_g, ln_b, l, *, nblk_b, nb, alpha):
    rows, d = xu.shape
    tm = 2 * SUB
    kern = functools.partial(_outproj_kernel, nsub=tm // SUB, nblk_b=nblk_b, nb=nb, d=d, alpha=alpha)
    yspec = pl.BlockSpec((tm, GROUP_W), lambda i: (i, 0))
    return pl.pallas_call(
        kern,
        out_shape=jax.ShapeDtypeStruct((rows, d), F32),
        grid=(rows // tm,),
        in_specs=[yspec, yspec, yspec, yspec,
                  pl.BlockSpec((None, 4 * GROUP_W, d), lambda i: (l, 0, 0)),
                  pl.BlockSpec((tm, d), lambda i: (i, 0)),
                  pl.BlockSpec((1, MOD_ROWS, 6 * d), lambda i: (l, 0, 0)),
                  pl.BlockSpec((1, d), lambda i: (0, 0)),
                  pl.BlockSpec((1, d), lambda i: (0, 0))],
        out_specs=pl.BlockSpec((tm, d), lambda i: (i, 0)),
        compiler_params=_cparams(("parallel",)),
        name="outproj",
    )(*ys, w_out, xu, mod, ln_g, ln_b)


def _ffn_kernel(x_ref, prev_ref, next_ref, mod_ref, wg_ref, wv_ref, cg_ref, cv_ref, wd_ref, g_ref, b_ref,
                o_ref, h_scr, hb_scr, acc_scr, *, nsub, nblk_b, nb, d, nj, alpha):
    i = pl.program_id(0)
    j = pl.program_id(1)
    tm = nsub * SUB

    @pl.when(j == 0)
    def _():
        acc_scr[...] = jnp.zeros_like(acc_scr)
        for s in range(nsub):
            g, _ = _group_of(i * nsub + s, nblk_b, nb)
            shift = mod_ref[0, pl.ds(g, 1), 3 * d:4 * d]
            scale = mod_ref[0, pl.ds(g, 1), 4 * d:5 * d]
            h_scr[8 + s * SUB:8 + (s + 1) * SUB, :] = x_ref[s * SUB:(s + 1) * SUB, :] * (1.0 + scale) + shift
            if s == 0:
                h_scr[0:8, :] = prev_ref[...] * (1.0 + scale) + shift
            if s == nsub - 1:
                h_scr[tm + 8:tm + 16, :] = next_ref[...] * (1.0 + scale) + shift
        hb_scr[...] = h_scr[...].astype(BF16)

    row = lax.broadcasted_iota(jnp.int32, (tm, 1), 0)
    keep_prev = jnp.ones((tm, 1), F32)
    keep_next = jnp.ones((tm, 1), F32)
    for s in range(nsub):
        _, t = _group_of(i * nsub + s, nblk_b, nb)
        first = (t == 0) | (t == 1)
        last = (t == 0) | (t == nblk_b - 1)
        keep_prev = jnp.where((row == s * SUB) & first, 0.0, keep_prev)
        keep_next = jnp.where((row == s * SUB + SUB - 1) & last, 0.0, keep_next)

    hb = hb_scr[...]
    ext = tm + 16

    def conv(u, cw_ref):
        up = pltpu.roll(u, 1, 0)[8:tm + 8, :] * keep_prev
        un = pltpu.roll(u, ext - 1, 0)[8:tm + 8, :] * keep_next
        return up * cw_ref[0:1, :] + u[8:tm + 8, :] * cw_ref[1:2, :] + un * cw_ref[2:3, :]

    gate = conv(_dot(hb, wg_ref[...]), cg_ref)
    val = conv(_dot(hb, wv_ref[...]), cv_ref)
    act = (gate * _sigmoid(gate) * val).astype(BF16)
    acc_scr[...] += _dot(act, wd_ref[...])

    @pl.when(j == nj - 1)
    def _():
        for s in range(nsub):
            g, _ = _group_of(i * nsub + s, nblk_b, nb)
            gt = mod_ref[0, pl.ds(g, 1), 5 * d:6 * d]
            sl = slice(s * SUB, (s + 1) * SUB)
            z = alpha * x_ref[sl, :] + gt * acc_scr[sl, :]
            o_ref[sl, :] = _layernorm_rows(z, g_ref[...], b_ref[...])


def _ffn(xu, mod, w_up, conv_w, w_down, ln_g, ln_b, l, *, nblk_b, nb, alpha):
    rows, d = xu.shape
    dff = w_down.shape[1]
    tm, tc = 2 * SUB, 512
    nj = dff // tc
    nhalo = rows // 8
    kern = functools.partial(_ffn_kernel, nsub=tm // SUB, nblk_b=nblk_b, nb=nb, d=d, nj=nj, alpha=alpha)
    return pl.pallas_call(
        kern,
        out_shape=jax.ShapeDtypeStruct((rows, d), F32),
        grid=(rows // tm, nj),
        in_specs=[
            pl.BlockSpec((tm, d), lambda i, j: (i, 0)),
            pl.BlockSpec((8, d), lambda i, j: (jnp.maximum(i * (tm // 8) - 1, 0), 0)),
            pl.BlockSpec((8, d), lambda i, j: (jnp.minimum((i + 1) * (tm // 8), nhalo - 1), 0)),
            pl.BlockSpec((1, MOD_ROWS, 6 * d), lambda i, j: (l, 0, 0)),
            pl.BlockSpec((None, d, tc), lambda i, j: (l, 0, j)),
            pl.BlockSpec((None, d, tc), lambda i, j: (l, 0, nj + j)),
            pl.BlockSpec((None, 3, tc), lambda i, j: (l, 0, j)),
            pl.BlockSpec((None, 3, tc), lambda i, j: (l, 0, nj + j)),
            pl.BlockSpec((None, tc, d), lambda i, j: (l, j, 0)),
            pl.BlockSpec((1, d), lambda i, j: (0, 0)),
            pl.BlockSpec((1, d), lambda i, j: (0, 0)),
        ],
        out_specs=pl.BlockSpec((tm, d), lambda i, j: (i, 0)),
        scratch_shapes=[pltpu.VMEM((tm + 16, d), F32), pltpu.VMEM((tm + 16, d), BF16), pltpu.VMEM((tm, d), F32)],
        compiler_params=_cparams(("parallel", "arbitrary")),
        name="ffn",
    )(xu, xu, xu, mod, w_up, w_up, conv_w, conv_w, w_down, ln_g, ln_b)


def _rope_tables(n_lat):
    pos = jnp.arange(n_lat)
    row = (pos // GRID_W).astype(F32)
    col = (pos % GRID_W).astype(F32)
    half = MLA_ROPE // 2
    inv = ROPE_THETA ** (-jnp.arange(0, half, 2, dtype=F32) / half)
    ar = row[:, None] * inv[None, :]
    ac = col[:, None] * inv[None, :]
    cos64 = jnp.concatenate([jnp.cos(ar), jnp.cos(ar), jnp.cos(ac), jnp.cos(ac)], axis=-1)
    sin64 = jnp.concatenate([-jnp.sin(ar), jnp.sin(ar), -jnp.sin(ac), jnp.sin(ac)], axis=-1)
    one = jnp.ones((n_lat, 64), F32)
    zero = jnp.zeros((n_lat, 64), F32)

    def with_ctx(c, s):
        c = jnp.concatenate([jnp.ones((SUB, 128), F32), c], axis=0)
        s = jnp.concatenate([jnp.zeros((SUB, 128), F32), s], axis=0)
        return c, s

    diff = with_ctx(jnp.concatenate([cos64, cos64], -1), jnp.concatenate([sin64, sin64], -1))
    mla = with_ctx(jnp.concatenate([cos64, one], -1), jnp.concatenate([sin64, zero], -1))
    return diff, mla


def _prep_w_in(w_in):
    depth, d, _ = w_in.shape
    z = lambda n: jnp.zeros((depth, d, n), w_in.dtype)
    o_diff = 4 * GROUP_W + 4 * HEADS
    o_mla = o_diff + 3 * GROUP_W
    o_na = o_mla + MLA_Q_RANK + MLA_KV_RANK + MLA_ROPE
    w = jnp.concatenate([w_in[..., :4 * GROUP_W], w_in[..., o_diff:o_mla], w_in[..., o_na:],
                         w_in[..., o_mla:o_na], z(MLA_W - (o_na - o_mla)),
                         w_in[..., 4 * GROUP_W:o_diff], z(128 - 4 * HEADS)], axis=-1)
    assert w.shape[-1] == N_PROJ
    return w.astype(BF16)


def _prep_mla_weights(w_uq, w_ukv):
    depth = w_uq.shape[0]
    dq = MLA_NOPE + MLA_ROPE
    wq = w_uq.reshape(depth, MLA_Q_RANK, HEADS, dq)
    wq = jnp.concatenate([wq, jnp.zeros((depth, MLA_Q_RANK, HEADS, 256 - dq), w_uq.dtype)], axis=-1)
    wq = wq.reshape(depth, MLA_Q_RANK, HEADS * 256)
    wkv = w_ukv.reshape(depth, MLA_KV_RANK, HEADS, MLA_NOPE + HEAD_DIM)
    wkv = jnp.concatenate([wkv[..., :MLA_NOPE].reshape(depth, MLA_KV_RANK, GROUP_W),
                           wkv[..., MLA_NOPE:].reshape(depth, MLA_KV_RANK, GROUP_W)], axis=-1)
    return wq.astype(BF16), wkv.astype(BF16)


def _pad_lanes(v, n=128):
    return jnp.concatenate([v, jnp.zeros(v.shape[:-1] + (n - v.shape[-1],), v.dtype)], axis=-1)


def _mixers(pq, pg, lw, l, tabs, *, nb, rb):
    nblk_b = rb // SUB
    (dcos, dsin), (mcos, msin) = tabs
    gx, gg = _gdn_prep(pq, pg, lw["gdn_conv"], lw["gdn_a"], lw["gdn_dtb"], nblk_b=nblk_b)
    o_f, o_b = _gdn(gx, gg, nb=nb, rb=rb)
    ya = _gdn_finish(o_f, o_b, pq, lw["gdn_norm_g"])
    qk_d, vt_d = _diff_prep(pq, dcos, dsin, nblk_b=nblk_b)
    yb = _flash(qk_d, qk_d, vt_d, nb=nb, rb=rb, dq=HEAD_DIM, qcol=0, kcol=GROUP_W, nsub=2,
                lam=lw["diff_lam"], norm_g=lw["diff_norm_g"])
    qm, km, vm = _mla_prep(pq, lw["mla_gq"], lw["mla_gkv"], lw["mla_wq"], lw["mla_wkv"], mcos, msin,
                           nblk_b=nblk_b)
    ym = _flash(qm, km, vm, nb=nb, rb=rb, dq=256, qcol=0, kcol=0)
    yn = _na(pq, lw["na_bias"], l, nb=nb, rb=rb)
    return ya, yb, ym, yn


def _layer(xu, lw, l, tabs, *, nb, rb, alpha):
    nblk_b = rb // SUB
    mod = lw["mod"]
    pq, pg = _inproj(xu, mod, lw["w_in"], l, nblk_b=nblk_b, nb=nb)
    ys = _mixers(pq, pg, lw, l, tabs, nb=nb, rb=rb)
    x1 = _outproj(ys, lw["w_out"], xu, mod, lw["ln_g0"], lw["ln_b0"], l, nblk_b=nblk_b, nb=nb, alpha=alpha)
    return _ffn(x1, mod, lw["ffn_w_up"], lw["ffn_conv"], lw["ffn_w_down"], lw["ln_g1"], lw["ln_b1"], l,
                nblk_b=nblk_b, nb=nb, alpha=alpha)


def _layer_weights(l, mod, w_in_p, gdn_conv, gdn_a_log, gdn_dt_bias, gdn_norm_g, diff_lambda, diff_norm_g,
                   mla_q_norm_g, mla_kv_norm_g, wq, wkv, na_bias, w_out_b, ln_g, ln_b, w_up_b, ffn_conv,
                   w_down_b):
    lf = diff_lambda[l].astype(F32)
    lam_init = 0.8 - 0.6 * math.exp(-0.3 * l)
    lam = jnp.exp(jnp.sum(lf[0] * lf[1])) - jnp.exp(jnp.sum(lf[2] * lf[3])) + lam_init
    return {
        "mod": mod,
        "w_in": w_in_p,
        "gdn_conv": gdn_conv[l],
        "gdn_a": _pad_lanes(jnp.exp(gdn_a_log[l].astype(F32)).reshape(1, 2 * HEADS)),
        "gdn_dtb": _pad_lanes(gdn_dt_bias[l].astype(F32).reshape(1, 2 * HEADS)),
        "gdn_norm_g": gdn_norm_g[l].reshape(1, HEAD_DIM),
        "diff_lam": jnp.stack([lam, jnp.asarray(1.0 - lam_init, F32)]),
        "diff_norm_g": diff_norm_g[l].reshape(1, HEAD_DIM),
        "mla_gq": mla_q_norm_g[l].reshape(1, MLA_Q_RANK),
        "mla_gkv": mla_kv_norm_g[l].reshape(1, MLA_KV_RANK),
        "mla_wq": wq[l],
        "mla_wkv": wkv[l],
        "na_bias": na_bias,
        "w_out": w_out_b,
        "ln_g0": ln_g[l, 0:1], "ln_b0": ln_b[l, 0:1], "ln_g1": ln_g[l, 1:2], "ln_b1": ln_b[l, 1:2],
        "ffn_w_up": w_up_b,
        "ffn_conv": ffn_conv,
        "ffn_w_down": w_down_b,
    }


def kernel(x, c, ctx, c_ctx, w_mod, b_mod, w_in, gdn_conv, gdn_a_log, gdn_dt_bias, gdn_norm_g, diff_lambda,
           diff_norm_g, mla_q_norm_g, mla_kv_norm_g, mla_w_uq, mla_w_ukv, na_rpb, w_out, ln_g, ln_b,
           ffn_w_up, ffn_conv, ffn_w_down):
    nb, n_lat, d = x.shape
    depth = w_mod.shape[0]
    assert ctx.shape[1] == SUB and n_lat % (2 * SUB) == 0 and n_lat % GRID_W == 0
    assert (nb * (SUB + n_lat)) % (2 * SUB) == 0 and nb + 1 <= MOD_ROWS
    rb = SUB + n_lat
    alpha = (2 * depth) ** 0.25

    cvec = jnp.concatenate([c, c_ctx[None, :], jnp.zeros((MOD_ROWS - nb - 1, d), c.dtype)], axis=0)
    mod = _modulation(cvec, w_mod, b_mod)
    tabs = _rope_tables(n_lat)
    w_in_p = _prep_w_in(w_in)
    wq, wkv = _prep_mla_weights(mla_w_uq, mla_w_ukv)
    na_bias = _na_bias_table(na_rpb, n_lat // GRID_W)
    w_out_b = w_out.astype(BF16)
    w_up_b = ffn_w_up.astype(BF16)
    w_down_b = ffn_w_down.astype(BF16)

    xu = jnp.concatenate([ctx, x], axis=1).reshape(nb * rb, d)
    for l in range(depth):
        lw = _layer_weights(l, mod, w_in_p, gdn_conv, gdn_a_log, gdn_dt_bias, gdn_norm_g, diff_lambda,
                            diff_norm_g, mla_q_norm_g, mla_kv_norm_g, wq, wkv, na_bias, w_out_b, ln_g, ln_b,
                            w_up_b, ffn_conv, w_down_b)
        xu = _layer(xu, lw, l, tabs, nb=nb, rb=rb, alpha=alpha)
    return xu.reshape(nb, rb, d)[:, SUB:, :]
```

```python
import functools
import math

import numpy as np
import jax
import jax.numpy as jnp
from jax import lax
from jax.experimental import pallas as pl
from jax.experimental.pallas import tpu as pltpu

F32 = jnp.float32
BF16 = jnp.bfloat16

HEADS = 4
HEAD_DIM = 128
GROUP_W = HEADS * HEAD_DIM
GRID_W = 64
NA_KR = 8
NA_KC = 16
NA_BLOCK_ROWS = 4
NA_UNION_ROWS = NA_BLOCK_ROWS + NA_KR - 1
MLA_Q_RANK = 384
MLA_KV_RANK = 128
MLA_NOPE = 128
MLA_ROPE = 64
ROPE_THETA = 10000.0
LN_EPS = 1e-5
RMS_EPS = 1e-6
NEG = -1e30
LOG2E = math.log2(math.e)

SUB = 256
GDN_CHUNK = 128
MOD_ROWS = 16
VMEM_LIMIT = 56 * 1024 * 1024

C_GQ, C_GK, C_GV, C_GZ = 0, 512, 1024, 1536
C_DQ, C_DK, C_DV = 2048, 2560, 3072
C_NQ, C_NK, C_NV = 3584, 4096, 4608
C_MLA = 5120
MLA_W = 640
C_AB = 5760
N_PROJ = 5888


def _cparams(sem):
    return pltpu.CompilerParams(dimension_semantics=sem, vmem_limit_bytes=VMEM_LIMIT)


def _sigmoid(x):
    return 1.0 / (1.0 + jnp.exp(-x))


def _dot(a, b):
    return jnp.dot(a, b, preferred_element_type=F32)


def _dot_nt(a, b):
    return lax.dot_general(a, b, (((1,), (1,)), ((), ())), preferred_element_type=F32)


def _group_of(blk, nblk_b, nb):
    t = blk % nblk_b
    return jnp.where(t == 0, nb, blk // nblk_b), t


def _mod_kernel(c_ref, w_ref, b_ref, o_ref):
    c = c_ref[...]
    s = (c * _sigmoid(c)).astype(BF16)
    o_ref[0] = _dot(s, w_ref[0].astype(BF16)) + b_ref[0]


def _modulation(cvec, w_mod, b_mod):
    depth, d, n = w_mod.shape
    tn = 1536
    return pl.pallas_call(
        _mod_kernel,
        out_shape=jax.ShapeDtypeStruct((depth, MOD_ROWS, n), F32),
        grid=(depth, n // tn),
        in_specs=[
            pl.BlockSpec((MOD_ROWS, d), lambda l, j: (0, 0)),
            pl.BlockSpec((1, d, tn), lambda l, j: (l, 0, j)),
            pl.BlockSpec((1, 1, tn), lambda l, j: (l, 0, j)),
        ],
        out_specs=pl.BlockSpec((1, MOD_ROWS, tn), lambda l, j: (l, 0, j)),
        compiler_params=_cparams(("parallel", "parallel")),
        name="modulation",
    )(cvec, w_mod, b_mod.reshape(depth, 1, n))


def _inproj_kernel(x_ref, mod_ref, w_ref, o_ref, og_ref, h_scr, *, nsub, nblk_b, nb, d, nj):
    i = pl.program_id(0)
    j = pl.program_id(1)

    @pl.when(j == 0)
    def _():
        for s in range(nsub):
            g, _ = _group_of(i * nsub + s, nblk_b, nb)
            shift = mod_ref[0, pl.ds(g, 1), 0:d]
            scale = mod_ref[0, pl.ds(g, 1), d:2 * d]
            xs = x_ref[s * SUB:(s + 1) * SUB, :]
            h_scr[s * SUB:(s + 1) * SUB, :] = (xs * (1.0 + scale) + shift).astype(BF16)

    acc = _dot(h_scr[...], w_ref[...])
    o_ref[...] = acc.astype(BF16)

    @pl.when(j == nj - 1)
    def _():
        og_ref[...] = acc[:, acc.shape[1] - 128:]


def _inproj(xu, mod, w_in_p, l, *, nblk_b, nb):
    rows, d = xu.shape
    n = w_in_p.shape[2]
    tm, tn = 2 * SUB, n // 2
    nj = n // tn
    kern = functools.partial(_inproj_kernel, nsub=tm // SUB, nblk_b=nblk_b, nb=nb, d=d, nj=nj)
    return pl.pallas_call(
        kern,
        out_shape=(jax.ShapeDtypeStruct((rows, n), BF16), jax.ShapeDtypeStruct((rows, 128), F32)),
        grid=(rows // tm, nj),
        in_specs=[
            pl.BlockSpec((tm, d), lambda i, j: (i, 0)),
            pl.BlockSpec((1, MOD_ROWS, 6 * d), lambda i, j: (l, 0, 0)),
            pl.BlockSpec((None, d, tn), lambda i, j: (l, 0, j)),
        ],
        out_specs=(pl.BlockSpec((tm, tn), lambda i, j: (i, j)),
                   pl.BlockSpec((tm, 128), lambda i, j: (i, 0))),
        scratch_shapes=[pltpu.VMEM((tm, d), BF16)],
        compiler_params=_cparams(("parallel", "arbitrary")),
        name="inproj",
    )(xu, mod, w_in_p)


def _rope_slab(x, cos, sin):
    lane = lax.broadcasted_iota(jnp.int32, x.shape, 1)
    odd = (lane & 16) != 0
    partner = jnp.where(odd, pltpu.roll(x, 16, 1), pltpu.roll(x, 128 - 16, 1))
    return x * cos + partner * sin


def _diff_prep_kernel(p_ref, v_ref, cos_ref, sin_ref, o_ref, vt_ref, *, scale):
    vt_ref[0] = v_ref[...].astype(F32).T.astype(BF16)
    cos = cos_ref[...]
    sin = sin_ref[...]
    for s in range(2 * HEADS):
        x = p_ref[:, s * 128:(s + 1) * 128].astype(F32)
        r = _rope_slab(x, cos, sin)
        if s < HEADS:
            r = r * scale
        o_ref[:, s * 128:(s + 1) * 128] = r.astype(BF16)


def _diff_prep(pq, cos, sin, *, nblk_b):
    rows = pq.shape[0]
    tm = SUB
    kern = functools.partial(_diff_prep_kernel, scale=(HEAD_DIM // 2) ** -0.5 * LOG2E)
    return pl.pallas_call(
        kern,
        out_shape=(jax.ShapeDtypeStruct((rows, 2 * GROUP_W), BF16),
                   jax.ShapeDtypeStruct((rows // tm, GROUP_W, tm), BF16)),
        grid=(rows // tm,),
        in_specs=[
            pl.BlockSpec((tm, 2 * GROUP_W), lambda i: (i, C_DQ // (2 * GROUP_W))),
            pl.BlockSpec((tm, GROUP_W), lambda i: (i, C_DV // GROUP_W)),
            pl.BlockSpec((tm, 128), lambda i: (i % nblk_b, 0)),
            pl.BlockSpec((tm, 128), lambda i: (i % nblk_b, 0)),
        ],
        out_specs=(pl.BlockSpec((tm, 2 * GROUP_W), lambda i: (i, 0)),
                   pl.BlockSpec((1, GROUP_W, tm), lambda i: (i, 0, 0))),
        compiler_params=_cparams(("parallel",)),
        name="diff_prep",
    )(pq, pq, cos, sin)


def _mla_prep_kernel(p_ref, gq_ref, gkv_ref, wq_ref, wkv_ref, cos_ref, sin_ref,
                     q_ref, k_ref, v_ref, *, scale):
    cos = cos_ref[...]
    sin = sin_ref[...]
    cq = p_ref[:, 0:MLA_Q_RANK].astype(F32)
    cq = cq * lax.rsqrt(jnp.mean(cq * cq, axis=-1, keepdims=True) + RMS_EPS) * gq_ref[...]
    q = _dot(cq.astype(BF16), wq_ref[...])
    ckv = p_ref[:, MLA_Q_RANK:MLA_Q_RANK + MLA_KV_RANK].astype(F32)
    ckv = ckv * lax.rsqrt(jnp.mean(ckv * ckv, axis=-1, keepdims=True) + RMS_EPS) * gkv_ref[...]
    kv = _dot(ckv.astype(BF16), wkv_ref[...])
    kr = _rope_slab(p_ref[:, MLA_Q_RANK + MLA_KV_RANK:MLA_W].astype(F32), cos, sin).astype(BF16)
    for h in range(HEADS):
        q_ref[:, h * 256:h * 256 + 128] = (q[:, h * 256:h * 256 + 128] * scale).astype(BF16)
        qr = _rope_slab(q[:, h * 256 + 128:(h + 1) * 256], cos, sin)
        q_ref[:, h * 256 + 128:(h + 1) * 256] = (qr * scale).astype(BF16)
        k_ref[:, h * 256:h * 256 + 128] = kv[:, h * 128:(h + 1) * 128].astype(BF16)
        k_ref[:, h * 256 + 128:(h + 1) * 256] = kr
    v_ref[0] = kv[:, GROUP_W:].T.astype(BF16)


def _mla_prep(pq, gq, gkv, wq, wkv, cos, sin, *, nblk_b):
    rows = pq.shape[0]
    tm = SUB
    kern = functools.partial(_mla_prep_kernel, scale=(MLA_NOPE + MLA_ROPE) ** -0.5 * LOG2E)
    return pl.pallas_call(
        kern,
        out_shape=(jax.ShapeDtypeStruct((rows, HEADS * 256), BF16),
                   jax.ShapeDtypeStruct((rows, HEADS * 256), BF16),
                   jax.ShapeDtypeStruct((rows // tm, GROUP_W, tm), BF16)),
        grid=(rows // tm,),
        in_specs=[
            pl.BlockSpec((tm, MLA_W), lambda i: (i, C_MLA // MLA_W)),
            pl.BlockSpec((1, MLA_Q_RANK), lambda i: (0, 0)),
            pl.BlockSpec((1, MLA_KV_RANK), lambda i: (0, 0)),
            pl.BlockSpec((MLA_Q_RANK, HEADS * 256), lambda i: (0, 0)),
            pl.BlockSpec((MLA_KV_RANK, 2 * GROUP_W), lambda i: (0, 0)),
            pl.BlockSpec((tm, 128), lambda i: (i % nblk_b, 0)),
            pl.BlockSpec((tm, 128), lambda i: (i % nblk_b, 0)),
        ],
        out_specs=(pl.BlockSpec((tm, HEADS * 256), lambda i: (i, 0)),
                   pl.BlockSpec((tm, HEADS * 256), lambda i: (i, 0)),
                   pl.BlockSpec((1, GROUP_W, tm), lambda i: (i, 0, 0))),
        compiler_params=_cparams(("parallel",)),
        name="mla_prep",
    )(pq, gq, gkv, wq, wkv, cos, sin)


def _flash_kernel(*refs, nsub, hp, dq, n_lat_pairs):
    if nsub == 2:
        lam_ref, g_ref, q_ref, k_ref, vt_ref, o_ref, s_scr, p_scr, acc_scr = refs
    else:
        q_ref, k_ref, vt_ref, o_ref, s_scr, p_scr, acc_scr = refs
    qt = pl.program_id(2)
    tq = q_ref.shape[0]
    tk = 2 * SUB
    qs = []
    for hh in range(hp):
        q = q_ref[:, hh * dq:(hh + 1) * dq]
        if nsub == 2:
            lane = lax.broadcasted_iota(jnp.int32, q.shape, 1)
            zero = jnp.zeros_like(q)
            qs.append((hh, jnp.where(lane < 64, q, zero)))
            qs.append((hh, jnp.where(lane >= 64, q, zero)))
        else:
            qs.append((hh, q))
    nch = len(qs)

    def scores_into(slot, rows, nrows):
        maxes = []
        for c, (hh, qi) in enumerate(qs):
            s = _dot_nt(k_ref[rows, hh * dq:(hh + 1) * dq], qi)
            s_scr[slot, c, 0:nrows, :] = s
            maxes.append(jnp.max(s, axis=0, keepdims=True))
        return maxes

    def pv_from(slot, vblocks):
        for c, (hh, _) in enumerate(qs):
            pv = None
            for i, blk in enumerate(vblocks):
                part = _dot(vt_ref[blk, hh * HEAD_DIM:(hh + 1) * HEAD_DIM, :],
                            p_scr[slot, c, i * SUB:(i + 1) * SUB, :])
                pv = part if pv is None else pv + part
            acc_scr[c] += pv

    def softmax(slot, nrows, stats, maxes):
        ch = 64
        new = []
        for c, ((m, l), mb) in enumerate(zip(stats, maxes)):
            m_new = jnp.maximum(m, mb)
            psum = None
            for r in range(0, nrows, ch):
                p = jnp.exp2(s_scr[slot, c, r:r + ch, :] - m_new)
                p_scr[slot, c, r:r + ch, :] = p.astype(BF16)
                psum = p if psum is None else psum + p
            alpha = jnp.exp2(m - m_new)
            acc_scr[c] = acc_scr[c] * alpha
            new.append((m_new, alpha * l + jnp.sum(psum, axis=0, keepdims=True)))
        return new

    def lat_rows(j):
        return pl.ds(pl.multiple_of(SUB + j * tk, SUB), tk)

    acc_scr[...] = jnp.zeros_like(acc_scr)
    stats = [(jnp.full((1, tq), NEG, F32), jnp.zeros((1, tq), F32)) for _ in qs]
    mx_ctx = scores_into(1, slice(0, SUB), SUB)
    mx0 = scores_into(0, lat_rows(0), tk)
    p_scr[1, :, SUB:, :] = jnp.zeros((nch, SUB, tq), BF16)
    stats = softmax(1, SUB, stats, mx_ctx)
    unroll = 2 if n_lat_pairs % 2 == 0 else 1
    n = jnp.where(qt == 0, 0, n_lat_pairs // unroll)
    last = 2 * n_lat_pairs - 1

    def pair(a, stats, mx0):
        bp = jnp.where(a == 0, 0, 2 * a - 1)
        pv_from(1, (bp, bp + 1))
        mx1 = scores_into(1, lat_rows(a + 1), tk)
        stats = softmax(0, tk, stats, mx0)
        pv_from(0, (1 + 2 * a, 2 + 2 * a))
        mx0 = scores_into(0, lat_rows(jnp.minimum(a + 2, last)), tk)
        return softmax(1, tk, stats, mx1), mx0

    def body(jj, carry):
        stats = tuple(zip(carry[0:2 * nch:2], carry[1:2 * nch:2]))
        mx0 = list(carry[2 * nch:])
        for u in range(unroll):
            stats, mx0 = pair(2 * (jj * unroll + u), stats, mx0)
        return tuple(x for st in stats for x in st) + tuple(mx0)

    flat = lax.fori_loop(0, n, body, tuple(x for st in stats for x in st) + tuple(mx0))
    stats = tuple(zip(flat[0:2 * nch:2], flat[1:2 * nch:2]))
    done = 2 * unroll * n
    bl = jnp.where(done == 0, 0, 2 * done - 1)
    pv_from(1, (bl, bl + 1))

    for hh in range(hp):
        if nsub == 2:
            o1 = acc_scr[2 * hh] / stats[2 * hh][1]
            o2 = acc_scr[2 * hh + 1] / stats[2 * hh + 1][1]
            o = (o1 - lam_ref[0] * o2).T
            o = o * lax.rsqrt(jnp.mean(o * o, axis=-1, keepdims=True) + RMS_EPS) * g_ref[...] * lam_ref[1]
        else:
            o = (acc_scr[hh] / stats[hh][1]).T
        o_ref[:, hh * HEAD_DIM:(hh + 1) * HEAD_DIM] = o.astype(o_ref.dtype)


def _flash(q_arr, k_arr, vt_arr, *, nb, rb, dq, qcol, kcol, nsub=1, lam=None, norm_g=None):
    rows = q_arr.shape[0]
    nblk_b = rb // SUB
    lat = rb - SUB
    hp = 2
    assert lat % (2 * SUB) == 0 and qcol % (hp * dq) == 0 and kcol % (hp * dq) == 0
    assert lat % (4 * SUB) == 0
    nch = hp * nsub
    kern = functools.partial(_flash_kernel, nsub=nsub, hp=hp, dq=dq, n_lat_pairs=lat // (4 * SUB))
    in_specs = [
        pl.BlockSpec((SUB, hp * dq), lambda b, h, t: (b * nblk_b + t, qcol // (hp * dq) + h)),
        pl.BlockSpec((rb, hp * dq), lambda b, h, t: (b, kcol // (hp * dq) + h)),
        pl.BlockSpec((nblk_b, hp * HEAD_DIM, SUB), lambda b, h, t: (b, h, 0)),
    ]
    args = [q_arr, k_arr, vt_arr]
    if nsub == 2:
        in_specs = [pl.BlockSpec(memory_space=pltpu.SMEM),
                    pl.BlockSpec((1, HEAD_DIM), lambda b, h, t: (0, 0))] + in_specs
        args = [lam, norm_g] + args
    return pl.pallas_call(
        kern,
        out_shape=jax.ShapeDtypeStruct((rows, GROUP_W), BF16),
        grid=(nb, HEADS // hp, nblk_b),
        in_specs=in_specs,
        out_specs=pl.BlockSpec((SUB, hp * HEAD_DIM), lambda b, h, t: (b * nblk_b + t, h)),
        scratch_shapes=[pltpu.VMEM((2, nch, 2 * SUB, SUB), F32), pltpu.VMEM((2, nch, 2 * SUB, SUB), BF16),
                        pltpu.VMEM((nch, HEAD_DIM, SUB), F32)],
        compiler_params=_cparams(("parallel", "parallel", "arbitrary")),
        name="flash_diff" if nsub == 2 else "flash_mla",
    )(*args)


def _na_kernel(q_ref, k_ref, v_ref, bias_ref, o_ref, *, grid_rows, scale, hp):
    qt = pl.program_id(2)
    qb = jnp.maximum(qt - 1, 0)
    u0 = jnp.clip(qb * NA_BLOCK_ROWS - NA_KR // 2, 0, grid_rows - NA_UNION_ROWS)
    start = pl.multiple_of(SUB + u0 * GRID_W, GRID_W)
    nn = NA_UNION_ROWS * GRID_W
    hs = [slice(h * HEAD_DIM, (h + 1) * HEAD_DIM) for h in range(hp)]
    qs = [(q_ref[:, sl].astype(F32) * scale).astype(BF16) for sl in hs]
    s_cs = [_dot_nt(q, k_ref[0:SUB, sl]) for q, sl in zip(qs, hs)]
    s_ns = [_dot_nt(q, k_ref[pl.ds(start, nn), sl]) + bias_ref[h, 0] for h, (q, sl) in enumerate(zip(qs, hs))]
    ms = [jnp.maximum(jnp.max(s_c, axis=-1, keepdims=True), jnp.max(s_n, axis=-1, keepdims=True))
          for s_c, s_n in zip(s_cs, s_ns)]
    e_cs = [jnp.exp2(s_c - m) for s_c, m in zip(s_cs, ms)]
    e_ns = [jnp.exp2(s_n - m) for s_n, m in zip(s_ns, ms)]
    for sl, e_c, e_n in zip(hs, e_cs, e_ns):
        l = jnp.sum(e_c, axis=-1, keepdims=True) + jnp.sum(e_n, axis=-1, keepdims=True)
        o = (_dot(e_c.astype(BF16), v_ref[0:SUB, sl]) + _dot(e_n.astype(BF16), v_ref[pl.ds(start, nn), sl])) / l
        o_ref[:, sl] = o.astype(o_ref.dtype)


def _na_bias_table(rpb, grid_rows):
    assert grid_rows >= NA_UNION_ROWS + NA_BLOCK_ROWS
    j = np.arange(NA_BLOCK_ROWS)[:, None]
    iu = np.arange(NA_UNION_ROWS)[None, :]
    cq = np.arange(GRID_W)[:, None]
    ck = np.arange(GRID_W)[None, :]
    cs = np.clip(cq - NA_KC // 2, 0, GRID_W - NA_KC)
    col_ok = (ck >= cs) & (ck < cs + NA_KC)
    dx = np.clip(ck - cq + NA_KC - 1, 0, 2 * NA_KC - 2)
    ex = np.eye(2 * NA_KC - 1, dtype=np.float32)[dx]
    eys, oks = [], []
    for r_off, w_off in ((j, 0 * j), (j + NA_KR // 2, j),
                         (j + NA_UNION_ROWS - NA_BLOCK_ROWS, 0 * j + NA_UNION_ROWS - NA_KR)):
        dy = np.clip(iu - r_off + NA_KR - 1, 0, 2 * NA_KR - 2)
        eys.append(np.eye(2 * NA_KR - 1, dtype=np.float32)[dy])
        oks.append((iu - w_off >= 0) & (iu - w_off < NA_KR))
    ey = np.stack(eys)
    ok = np.stack(oks)[:, :, None, :, None] & col_ok[None, None, :, None, :]
    vals = jnp.einsum("tjia,lhab,qkb->lhtjqik", ey, rpb.astype(F32), ex, precision=lax.Precision.HIGHEST)
    nb_part = jnp.where(ok[None, None], vals * LOG2E, NEG)
    nb_part = jnp.concatenate([nb_part, jnp.full_like(nb_part[:, :, :1], NEG)], axis=2)
    return nb_part.reshape(rpb.shape[:2] + (4, NA_BLOCK_ROWS * GRID_W, NA_UNION_ROWS * GRID_W))


def _na(pq, bias, l, *, nb, rb):
    rows = pq.shape[0]
    nblk_b = rb // SUB
    grid_rows = (rb - SUB) // GRID_W
    nqb = nblk_b - 1
    hp = 2
    hw = hp * HEAD_DIM
    kern = functools.partial(_na_kernel, grid_rows=grid_rows, scale=HEAD_DIM ** -0.5 * LOG2E, hp=hp)

    def bias_map(b, h, t):
        ty = jnp.where(t == 0, 3, jnp.where(t == 1, 0, jnp.where(t == nqb, 2, 1)))
        return (l, h, ty, 0, 0)

    return pl.pallas_call(
        kern,
        out_shape=jax.ShapeDtypeStruct((rows, GROUP_W), BF16),
        grid=(nb, HEADS // hp, nblk_b),
        in_specs=[
            pl.BlockSpec((SUB, hw), lambda b, h, t: (b * nblk_b + t, C_NQ // hw + h)),
            pl.BlockSpec((rb, hw), lambda b, h, t: (b, C_NK // hw + h)),
            pl.BlockSpec((rb, hw), lambda b, h, t: (b, C_NV // hw + h)),
            pl.BlockSpec((None, hp, 1, SUB, bias.shape[-1]), bias_map),
        ],
        out_specs=pl.BlockSpec((SUB, hw), lambda b, h, t: (b * nblk_b + t, h)),
        compiler_params=_cparams(("parallel", "parallel", "arbitrary")),
        name="na",
    )(pq, pq, pq, bias)


def _gdn_prep_kernel(x_ref, prev_ref, next_ref, w_ref, g_ref, av_ref, dtb_ref, o_ref, og_ref, *, nblk_b):
    i = pl.program_id(0)
    t = i % nblk_b
    keep_prev = jnp.where((t == 0) | (t == 1), 0.0, 1.0)
    keep_next = jnp.where((t == 0) | (t == nblk_b - 1), 0.0, 1.0)
    x = x_ref[...].astype(F32)
    tm = x.shape[0]
    row = lax.broadcasted_iota(jnp.int32, x.shape, 0)
    halo_p = prev_ref[15:16, :].astype(F32) * keep_prev
    halo_n = next_ref[0:1, :].astype(F32) * keep_next
    xp = jnp.where(row == 0, halo_p, pltpu.roll(x, 1, 0))
    xn = jnp.where(row == tm - 1, halo_n, pltpu.roll(x, tm - 1, 0))
    c = xp * w_ref[0:1, :] + x * w_ref[1:2, :] + xn * w_ref[2:3, :]
    s = c * _sigmoid(c)
    for h in range(2 * HEADS):
        sh = s[:, h * 128:(h + 1) * 128]
        nrm = lax.rsqrt(jnp.sum(sh * sh, axis=-1, keepdims=True) + RMS_EPS)
        if h < HEADS:
            nrm = nrm * HEAD_DIM ** -0.5
        o_ref[:, h * 128:(h + 1) * 128] = (sh * nrm).astype(BF16)
    o_ref[:, 2 * GROUP_W:] = s[:, 2 * GROUP_W:].astype(BF16)
    g = g_ref[...]
    z = g + dtb_ref[...]
    softplus = jnp.maximum(z, 0.0) + jnp.log(1.0 + jnp.exp(-jnp.abs(z)))
    lane = lax.broadcasted_iota(jnp.int32, g.shape, 1)
    og_ref[...] = jnp.where(lane < 2 * HEADS, -av_ref[...] * softplus, _sigmoid(g))


def _gdn_prep(pq, pg, conv_w, a_vec, dtb_vec, *, nblk_b):
    rows = pq.shape[0]
    tm = SUB
    w3 = 3 * GROUP_W
    nhalo = rows // 16
    kern = functools.partial(_gdn_prep_kernel, nblk_b=nblk_b)
    return pl.pallas_call(
        kern,
        out_shape=(jax.ShapeDtypeStruct((rows, w3), BF16), jax.ShapeDtypeStruct((rows, 128), F32)),
        grid=(rows // tm,),
        in_specs=[
            pl.BlockSpec((tm, w3), lambda i: (i, 0)),
            pl.BlockSpec((16, w3), lambda i: (jnp.maximum(i * (tm // 16) - 1, 0), 0)),
            pl.BlockSpec((16, w3), lambda i: (jnp.minimum((i + 1) * (tm // 16), nhalo - 1), 0)),
            pl.BlockSpec((3, w3), lambda i: (0, 0)),
            pl.BlockSpec((tm, 128), lambda i: (i, 0)),
            pl.BlockSpec((1, 128), lambda i: (0, 0)),
            pl.BlockSpec((1, 128), lambda i: (0, 0)),
        ],
        out_specs=(pl.BlockSpec((tm, w3), lambda i: (i, 0)), pl.BlockSpec((tm, 128), lambda i: (i, 0))),
        compiler_params=_cparams(("parallel",)),
        name="gdn_prep",
    )(pq, pq, pq, conv_w, pg, a_vec, dtb_vec)


def _split3(x):
    hi = x.astype(BF16)
    r = x - hi.astype(F32)
    mid = r.astype(BF16)
    lo = (r - mid.astype(F32)).astype(BF16)
    return hi, mid, lo


def _gdn_chains(chains):
    c = chains[0][0].shape[0]
    row = lax.broadcasted_iota(jnp.int32, (c, c), 0)
    col = lax.broadcasted_iota(jnp.int32, (c, c), 1)
    xor = row ^ col
    eye = jnp.where(row == col, 1.0, 0.0)
    a_mats, qks, kbs, kfs, egs = [], [], [], [], []
    for q, k, v, gcol, grow, beta, glast, s_prev, lower in chains:
        incl = (row >= col) if lower else (row <= col)
        strict = (row > col) if lower else (row < col)
        decay = jnp.exp(jnp.where(incl, gcol - grow, NEG))
        kf = k.astype(F32)
        kb = kf * beta
        a_mats.append(jnp.where(strict, _dot_nt(kb.astype(BF16), k) * decay, 0.0))
        qks.append(jnp.where(incl, _dot_nt(q, k) * decay, 0.0).astype(BF16))
        kbs.append(kb)
        kfs.append(kf)
        egs.append(jnp.exp(gcol))
    tinvs = [eye - jnp.where(xor < 2, a, 0.0) for a in a_mats]
    s = 2
    while s < c:
        ys = [_dot(jnp.where((xor >= s) & (xor < 2 * s), a, 0.0).astype(BF16), t.astype(BF16))
              for a, t in zip(a_mats, tinvs)]
        tinvs = [t - _dot(t.astype(BF16), y.astype(BF16)) for t, y in zip(tinvs, ys)]
        s *= 2
    wus = []
    for (q, k, v, gcol, grow, beta, glast, s_prev, lower), kb, eg, t in zip(chains, kbs, egs, tinvs):
        rhs = jnp.concatenate([kb * eg, v.astype(F32) * beta], axis=1).astype(BF16)
        wus.append(_dot(t.astype(BF16), rhs))
    sbs = [ch[7].astype(BF16) for ch in chains]
    vnbs = [(wu[:, HEAD_DIM:] - _dot(wu[:, :HEAD_DIM].astype(BF16), sb)).astype(BF16) for wu, sb in zip(wus, sbs)]
    outs = []
    for (q, k, v, gcol, grow, beta, glast, s_prev, lower), kf, eg, qk, sb, vnb in zip(chains, kfs, egs, qks, sbs, vnbs):
        o = _dot((q.astype(F32) * eg).astype(BF16), sb) + _dot(qk, vnb)
        kd_t = (kf * jnp.exp(glast - gcol)).T.astype(BF16)
        s_new = s_prev * jnp.exp(glast) + _dot(kd_t, vnb)
        outs.append((o, s_new))
    return outs


def _gdn_kernel(xf_ref, xb_ref, gf_ref, gb_ref, of_ref, ob_ref, s_scr):
    st = pl.program_id(1)

    @pl.when(st == 0)
    def _():
        s_scr[...] = jnp.zeros_like(s_scr)

    c = GDN_CHUNK
    row = lax.broadcasted_iota(jnp.int32, (c, c), 0)
    col = lax.broadcasted_iota(jnp.int32, (c, c), 1)
    chains = []
    for d, (x_ref, g_ref) in enumerate(((xf_ref, gf_ref), (xb_ref, gb_ref))):
        lower = d == 0
        gates = g_ref[...]
        tri = jnp.where((row >= col) if lower else (row <= col), 1.0, 0.0).astype(BF16)
        hi, mid, lo = _split3(gates)
        gsum = _dot(tri, hi) + _dot(tri, mid) + _dot(tri, lo)
        gsum_t = gsum.T
        for h in range(HEADS):
            gi = d * HEADS + h
            bi = 2 * HEADS + gi
            glast = gsum[c - 1:c, gi:gi + 1] if lower else gsum[0:1, gi:gi + 1]
            chains.append((x_ref[:, h * 128:(h + 1) * 128],
                           x_ref[:, GROUP_W + h * 128:GROUP_W + (h + 1) * 128],
                           x_ref[:, 2 * GROUP_W + h * 128:2 * GROUP_W + (h + 1) * 128],
                           gsum[:, gi:gi + 1], gsum_t[gi:gi + 1, :], gates[:, bi:bi + 1], glast,
                           s_scr[d, h], lower))
    outs = _gdn_chains(chains)
    for idx, (o, s_new) in enumerate(outs):
        d, h = divmod(idx, HEADS)
        s_scr[d, h] = s_new
        (of_ref, ob_ref)[d][:, h * 128:(h + 1) * 128] = o


def _gdn(gx, gg, *, nb, rb):
    rows = gx.shape[0]
    c = GDN_CHUNK
    nch = rb // c
    nctx = SUB // c
    w3 = 3 * GROUP_W

    def fmap(b, s):
        return (b * nch + s, 0)

    def bmap(b, s):
        return (b * nch + jnp.where(s < nctx, nctx - 1 - s, nch + nctx - 1 - s), 0)

    return pl.pallas_call(
        _gdn_kernel,
        out_shape=(jax.ShapeDtypeStruct((rows, GROUP_W), F32), jax.ShapeDtypeStruct((rows, GROUP_W), F32)),
        grid=(nb, nch),
        in_specs=[
            pl.BlockSpec((c, w3), fmap), pl.BlockSpec((c, w3), bmap),
            pl.BlockSpec((c, 128), fmap), pl.BlockSpec((c, 128), bmap),
        ],
        out_specs=(pl.BlockSpec((c, GROUP_W), fmap), pl.BlockSpec((c, GROUP_W), bmap)),
        scratch_shapes=[pltpu.VMEM((2, HEADS, HEAD_DIM, HEAD_DIM), F32)],
        compiler_params=_cparams(("parallel", "arbitrary")),
        name="gdn",
    )(gx, gx, gg, gg)


def _layernorm_rows(z, g, b):
    mu = jnp.mean(z, axis=-1, keepdims=True)
    zc = z - mu
    var = jnp.mean(zc * zc, axis=-1, keepdims=True)
    return zc * lax.rsqrt(var + LN_EPS) * g + b


def _outproj_kernel(of_ref, ob_ref, z_ref, gn_ref, yb_ref, ym_ref, yn_ref, w_ref, x_ref, mod_ref, g_ref, b_ref,
                    o_ref, *, nsub, nblk_b, nb, d, alpha):
    i = pl.program_id(0)
    z = z_ref[...].astype(F32)
    gate = z * _sigmoid(z)
    ya = []
    for h in range(HEADS):
        sl = slice(h * HEAD_DIM, (h + 1) * HEAD_DIM)
        o = of_ref[:, sl] + ob_ref[:, sl]
        o = o * lax.rsqrt(jnp.mean(o * o, axis=-1, keepdims=True) + RMS_EPS) * gn_ref[...]
        ya.append((o * gate[:, sl]).astype(BF16))
    y = _dot(jnp.concatenate(ya, axis=1), w_ref[0:GROUP_W, :])
    y = y + _dot(yb_ref[...], w_ref[GROUP_W:2 * GROUP_W, :])
    y = y + _dot(ym_ref[...], w_ref[2 * GROUP_W:3 * GROUP_W, :])
    y = y + _dot(yn_ref[...], w_ref[3 * GROUP_W:4 * GROUP_W, :])
    for s in range(nsub):
        g, _ = _group_of(i * nsub + s, nblk_b, nb)
        gate = mod_ref[0, pl.ds(g, 1), 2 * d:3 * d]
        sl = slice(s * SUB, (s + 1) * SUB)
        z = alpha * x_ref[sl, :] + gate * y[sl, :]
        o_ref[sl, :] = _layernorm_rows(z, g_ref[...], b_ref[...])


def _outproj(gdn, ys, w_out, xu, mod, ln_g, ln_b, l, *, nblk_b, nb, alpha):
    o_f, o_b, pq, norm_g = gdn
    rows, d = xu.shape
    tm = 2 * SUB
    kern = functools.partial(_outproj_kernel, nsub=tm // SUB, nblk_b=nblk_b, nb=nb, d=d, alpha=alpha)
    yspec = pl.BlockSpec((tm, GROUP_W), lambda i: (i, 0))
    return pl.pallas_call(
        kern,
        out_shape=jax.ShapeDtypeStruct((rows, d), F32),
        grid=(rows // tm,),
        in_specs=[yspec, yspec,
                  pl.BlockSpec((tm, GROUP_W), lambda i: (i, C_GZ // GROUP_W)),
                  pl.BlockSpec((1, HEAD_DIM), lambda i: (0, 0)),
                  yspec, yspec, yspec,
                  pl.BlockSpec((None, 4 * GROUP_W, d), lambda i: (l, 0, 0)),
                  pl.BlockSpec((tm, d), lambda i: (i, 0)),
                  pl.BlockSpec((1, MOD_ROWS, 6 * d), lambda i: (l, 0, 0)),
                  pl.BlockSpec((1, d), lambda i: (0, 0)),
                  pl.BlockSpec((1, d), lambda i: (0, 0))],
        out_specs=pl.BlockSpec((tm, d), lambda i: (i, 0)),
        compiler_params=_cparams(("parallel",)),
        name="outproj",
    )(o_f, o_b, pq, norm_g, *ys, w_out, xu, mod, ln_g, ln_b)


def _ffn_kernel(x_ref, prev_ref, next_ref, mod_ref, wg_ref, wv_ref, cg_ref, cv_ref, wd_ref, g_ref, b_ref,
                o_ref, h_scr, hb_scr, acc_scr, *, nsub, nblk_b, nb, d, nj, alpha):
    i = pl.program_id(0)
    j = pl.program_id(1)
    tm = nsub * SUB

    @pl.when(j == 0)
    def _():
        acc_scr[...] = jnp.zeros_like(acc_scr)
        for s in range(nsub):
            g, _ = _group_of(i * nsub + s, nblk_b, nb)
            shift = mod_ref[0, pl.ds(g, 1), 3 * d:4 * d]
            scale = mod_ref[0, pl.ds(g, 1), 4 * d:5 * d]
            h_scr[8 + s * SUB:8 + (s + 1) * SUB, :] = x_ref[s * SUB:(s + 1) * SUB, :] * (1.0 + scale) + shift
            if s == 0:
                h_scr[0:8, :] = prev_ref[...] * (1.0 + scale) + shift
            if s == nsub - 1:
                h_scr[tm + 8:tm + 16, :] = next_ref[...] * (1.0 + scale) + shift
        hb_scr[...] = h_scr[...].astype(BF16)

    row = lax.broadcasted_iota(jnp.int32, (tm, 1), 0)
    keep_prev = jnp.ones((tm, 1), F32)
    keep_next = jnp.ones((tm, 1), F32)
    for s in range(nsub):
        _, t = _group_of(i * nsub + s, nblk_b, nb)
        first = (t == 0) | (t == 1)
        last = (t == 0) | (t == nblk_b - 1)
        keep_prev = jnp.where((row == s * SUB) & first, 0.0, keep_prev)
        keep_next = jnp.where((row == s * SUB + SUB - 1) & last, 0.0, keep_next)

    hb = hb_scr[...]
    ext = tm + 16

    def conv(u, cw_ref):
        up = pltpu.roll(u, 1, 0)[8:tm + 8, :] * keep_prev
        un = pltpu.roll(u, ext - 1, 0)[8:tm + 8, :] * keep_next
        return up * cw_ref[0:1, :] + u[8:tm + 8, :] * cw_ref[1:2, :] + un * cw_ref[2:3, :]

    gate = conv(_dot(hb, wg_ref[...]), cg_ref)
    val = conv(_dot(hb, wv_ref[...]), cv_ref)
    act = (gate * _sigmoid(gate) * val).astype(BF16)
    acc_scr[...] += _dot(act, wd_ref[...])

    @pl.when(j == nj - 1)
    def _():
        for s in range(nsub):
            g, _ = _group_of(i * nsub + s, nblk_b, nb)
            gt = mod_ref[0, pl.ds(g, 1), 5 * d:6 * d]
            sl = slice(s * SUB, (s + 1) * SUB)
            z = alpha * x_ref[sl, :] + gt * acc_scr[sl, :]
            o_ref[sl, :] = _layernorm_rows(z, g_ref[...], b_ref[...])


def _ffn(xu, mod, w_up, conv_w, w_down, ln_g, ln_b, l, *, nblk_b, nb, alpha):
    rows, d = xu.shape
    dff = w_down.shape[1]
    tm, tc = 2 * SUB, 512
    nj = dff // tc
    nhalo = rows // 8
    kern = functools.partial(_ffn_kernel, nsub=tm // SUB, nblk_b=nblk_b, nb=nb, d=d, nj=nj, alpha=alpha)
    return pl.pallas_call(
        kern,
        out_shape=jax.ShapeDtypeStruct((rows, d), F32),
        grid=(rows // tm, nj),
        in_specs=[
            pl.BlockSpec((tm, d), lambda i, j: (i, 0)),
            pl.BlockSpec((8, d), lambda i, j: (jnp.maximum(i * (tm // 8) - 1, 0), 0)),
            pl.BlockSpec((8, d), lambda i, j: (jnp.minimum((i + 1) * (tm // 8), nhalo - 1), 0)),
            pl.BlockSpec((1, MOD_ROWS, 6 * d), lambda i, j: (l, 0, 0)),
            pl.BlockSpec((None, d, tc), lambda i, j: (l, 0, j)),
            pl.BlockSpec((None, d, tc), lambda i, j: (l, 0, nj + j)),
            pl.BlockSpec((None, 3, tc), lambda i, j: (l, 0, j)),
            pl.BlockSpec((None, 3, tc), lambda i, j: (l, 0, nj + j)),
            pl.BlockSpec((None, tc, d), lambda i, j: (l, j, 0)),
            pl.BlockSpec((1, d), lambda i, j: (0, 0)),
            pl.BlockSpec((1, d), lambda i, j: (0, 0)),
        ],
        out_specs=pl.BlockSpec((tm, d), lambda i, j: (i, 0)),
        scratch_shapes=[pltpu.VMEM((tm + 16, d), F32), pltpu.VMEM((tm + 16, d), BF16), pltpu.VMEM((tm, d), F32)],
        compiler_params=_cparams(("parallel", "arbitrary")),
        name="ffn",
    )(xu, xu, xu, mod, w_up, w_up, conv_w, conv_w, w_down, ln_g, ln_b)


def _rope_tables(n_lat):
    pos = jnp.arange(n_lat)
    row = (pos // GRID_W).astype(F32)
    col = (pos % GRID_W).astype(F32)
    half = MLA_ROPE // 2
    inv = ROPE_THETA ** (-jnp.arange(0, half, 2, dtype=F32) / half)
    ar = row[:, None] * inv[None, :]
    ac = col[:, None] * inv[None, :]
    cos64 = jnp.concatenate([jnp.cos(ar), jnp.cos(ar), jnp.cos(ac), jnp.cos(ac)], axis=-1)
    sin64 = jnp.concatenate([-jnp.sin(ar), jnp.sin(ar), -jnp.sin(ac), jnp.sin(ac)], axis=-1)
    one = jnp.ones((n_lat, 64), F32)
    zero = jnp.zeros((n_lat, 64), F32)

    def with_ctx(c, s):
        c = jnp.concatenate([jnp.ones((SUB, 128), F32), c], axis=0)
        s = jnp.concatenate([jnp.zeros((SUB, 128), F32), s], axis=0)
        return c, s

    diff = with_ctx(jnp.concatenate([cos64, cos64], -1), jnp.concatenate([sin64, sin64], -1))
    mla = with_ctx(jnp.concatenate([cos64, one], -1), jnp.concatenate([sin64, zero], -1))
    return diff, mla


def _prep_w_in(w_in):
    depth, d, _ = w_in.shape
    z = lambda n: jnp.zeros((depth, d, n), w_in.dtype)
    o_diff = 4 * GROUP_W + 4 * HEADS
    o_mla = o_diff + 3 * GROUP_W
    o_na = o_mla + MLA_Q_RANK + MLA_KV_RANK + MLA_ROPE
    w = jnp.concatenate([w_in[..., :4 * GROUP_W], w_in[..., o_diff:o_mla], w_in[..., o_na:],
                         w_in[..., o_mla:o_na], z(MLA_W - (o_na - o_mla)),
                         w_in[..., 4 * GROUP_W:o_diff], z(128 - 4 * HEADS)], axis=-1)
    assert w.shape[-1] == N_PROJ
    return lax.optimization_barrier(w).astype(BF16)


def _prep_mla_weights(w_uq, w_ukv):
    depth = w_uq.shape[0]
    dq = MLA_NOPE + MLA_ROPE
    wq = w_uq.reshape(depth, MLA_Q_RANK, HEADS, dq)
    wq = jnp.concatenate([wq, jnp.zeros((depth, MLA_Q_RANK, HEADS, 256 - dq), w_uq.dtype)], axis=-1)
    wq = wq.reshape(depth, MLA_Q_RANK, HEADS * 256)
    wkv = w_ukv.reshape(depth, MLA_KV_RANK, HEADS, MLA_NOPE + HEAD_DIM)
    wkv = jnp.concatenate([wkv[..., :MLA_NOPE].reshape(depth, MLA_KV_RANK, GROUP_W),
                           wkv[..., MLA_NOPE:].reshape(depth, MLA_KV_RANK, GROUP_W)], axis=-1)
    return wq.astype(BF16), wkv.astype(BF16)


def _pad_lanes(v, n=128):
    return jnp.concatenate([v, jnp.zeros(v.shape[:-1] + (n - v.shape[-1],), v.dtype)], axis=-1)


def _mixers(pq, pg, lw, l, tabs, *, nb, rb):
    nblk_b = rb // SUB
    (dcos, dsin), (mcos, msin) = tabs
    gx, gg = _gdn_prep(pq, pg, lw["gdn_conv"], lw["gdn_a"], lw["gdn_dtb"], nblk_b=nblk_b)
    o_f, o_b = _gdn(gx, gg, nb=nb, rb=rb)
    ya = (o_f, o_b, pq, lw["gdn_norm_g"])
    qk_d, vt_d = _diff_prep(pq, dcos, dsin, nblk_b=nblk_b)
    yb = _flash(qk_d, qk_d, vt_d, nb=nb, rb=rb, dq=HEAD_DIM, qcol=0, kcol=GROUP_W, nsub=2,
                lam=lw["diff_lam"], norm_g=lw["diff_norm_g"])
    qm, km, vm = _mla_prep(pq, lw["mla_gq"], lw["mla_gkv"], lw["mla_wq"], lw["mla_wkv"], mcos, msin,
                           nblk_b=nblk_b)
    ym = _flash(qm, km, vm, nb=nb, rb=rb, dq=256, qcol=0, kcol=0)
    yn = _na(pq, lw["na_bias"], l, nb=nb, rb=rb)
    return ya, yb, ym, yn


def _layer(xu, lw, l, tabs, *, nb, rb, alpha):
    nblk_b = rb // SUB
    mod = lw["mod"]
    pq, pg = _inproj(xu, mod, lw["w_in"], l, nblk_b=nblk_b, nb=nb)
    ya, yb, ym, yn = _mixers(pq, pg, lw, l, tabs, nb=nb, rb=rb)
    x1 = _outproj(ya, (yb, ym, yn), lw["w_out"], xu, mod, lw["ln_g0"], lw["ln_b0"], l, nblk_b=nblk_b, nb=nb, alpha=alpha)
    return _ffn(x1, mod, lw["ffn_w_up"], lw["ffn_conv"], lw["ffn_w_down"], lw["ln_g1"], lw["ln_b1"], l,
                nblk_b=nblk_b, nb=nb, alpha=alpha)


def _layer_weights(l, mod, w_in_p, gdn_conv, gdn_a_log, gdn_dt_bias, gdn_norm_g, diff_lambda, diff_norm_g,
                   mla_q_norm_g, mla_kv_norm_g, wq, wkv, na_bias, w_out_b, ln_g, ln_b, w_up_b, ffn_conv,
                   w_down_b):
    lf = diff_lambda[l].astype(F32)
    lam_init = 0.8 - 0.6 * math.exp(-0.3 * l)
    lam = jnp.exp(jnp.sum(lf[0] * lf[1])) - jnp.exp(jnp.sum(lf[2] * lf[3])) + lam_init
    return {
        "mod": mod,
        "w_in": w_in_p,
        "gdn_conv": gdn_conv[l],
        "gdn_a": _pad_lanes(jnp.exp(gdn_a_log[l].astype(F32)).reshape(1, 2 * HEADS)),
        "gdn_dtb": _pad_lanes(gdn_dt_bias[l].astype(F32).reshape(1, 2 * HEADS)),
        "gdn_norm_g": gdn_norm_g[l].reshape(1, HEAD_DIM),
        "diff_lam": jnp.stack([lam, jnp.asarray(1.0 - lam_init, F32)]),
        "diff_norm_g": diff_norm_g[l].reshape(1, HEAD_DIM),
        "mla_gq": mla_q_norm_g[l].reshape(1, MLA_Q_RANK),
        "mla_gkv": mla_kv_norm_g[l].reshape(1, MLA_KV_RANK),
        "mla_wq": wq[l],
        "mla_wkv": wkv[l],
        "na_bias": na_bias,
        "w_out": w_out_b,
        "ln_g0": ln_g[l, 0:1], "ln_b0": ln_b[l, 0:1], "ln_g1": ln_g[l, 1:2], "ln_b1": ln_b[l, 1:2],
        "ffn_w_up": w_up_b,
        "ffn_conv": ffn_conv,
        "ffn_w_down": w_down_b,
    }


def kernel(x, c, ctx, c_ctx, w_mod, b_mod, w_in, gdn_conv, gdn_a_log, gdn_dt_bias, gdn_norm_g, diff_lambda,
           diff_norm_g, mla_q_norm_g, mla_kv_norm_g, mla_w_uq, mla_w_ukv, na_rpb, w_out, ln_g, ln_b,
           ffn_w_up, ffn_conv, ffn_w_down):
    nb, n_lat, d = x.shape
    depth = w_mod.shape[0]
    assert ctx.shape[1] == SUB and n_lat % (2 * SUB) == 0 and n_lat % GRID_W == 0
    assert (nb * (SUB + n_lat)) % (2 * SUB) == 0 and nb + 1 <= MOD_ROWS
    rb = SUB + n_lat
    alpha = (2 * depth) ** 0.25

    cvec = jnp.concatenate([c, c_ctx[None, :], jnp.zeros((MOD_ROWS - nb - 1, d), c.dtype)], axis=0)
    mod = _modulation(cvec, w_mod, b_mod)
    tabs = _rope_tables(n_lat)
    w_in_p = _prep_w_in(w_in)
    wq, wkv = _prep_mla_weights(mla_w_uq, mla_w_ukv)
    na_bias = _na_bias_table(na_rpb, n_lat // GRID_W)
    w_out_b = w_out.astype(BF16)
    w_up_b = ffn_w_up.astype(BF16)
    w_down_b = ffn_w_down.astype(BF16)

    xu = jnp.concatenate([ctx, x], axis=1).reshape(nb * rb, d)
    for l in range(depth):
        lw = _layer_weights(l, mod, w_in_p, gdn_conv, gdn_a_log, gdn_dt_bias, gdn_norm_g, diff_lambda,
                            diff_norm_g, mla_q_norm_g, mla_kv_norm_g, wq, wkv, na_bias, w_out_b, ln_g, ln_b,
                            w_up_b, ffn_conv, w_down_b)
        xu = _layer(xu, lw, l, tabs, nb=nb, rb=rb, alpha=alpha)
    return xu.reshape(nb, rb, d)[:, SUB:, :]
```

```python
import functools
import math

import numpy as np
import jax
import jax.numpy as jnp
from jax import lax
from jax.experimental import pallas as pl
from jax.experimental.pallas import tpu as pltpu

F32 = jnp.float32
BF16 = jnp.bfloat16

HEADS = 4
HEAD_DIM = 128
GROUP_W = HEADS * HEAD_DIM
GRID_W = 64
NA_KR = 8
NA_KC = 16
NA_BLOCK_ROWS = 4
NA_UNION_ROWS = NA_BLOCK_ROWS + NA_KR - 1
MLA_Q_RANK = 384
MLA_KV_RANK = 128
MLA_NOPE = 128
MLA_ROPE = 64
ROPE_THETA = 10000.0
LN_EPS = 1e-5
RMS_EPS = 1e-6
NEG = -1e30
LOG2E = math.log2(math.e)

SUB = 256
VT_ROWS = HEAD_DIM + 16
GDN_CHUNK = 128
MOD_ROWS = 16
VMEM_LIMIT = 56 * 1024 * 1024

C_GQ, C_GK, C_GV, C_GZ = 0, 512, 1024, 1536
C_DQ, C_DK, C_DV = 2048, 2560, 3072
C_NQ, C_NK, C_NV = 3584, 4096, 4608
C_MLA = 5120
MLA_W = 640
C_AB = 5760
N_PROJ = 5888


def _cparams(sem):
    return pltpu.CompilerParams(dimension_semantics=sem, vmem_limit_bytes=VMEM_LIMIT)


def _sigmoid(x):
    return 1.0 / (1.0 + jnp.exp(-x))


def _dot(a, b):
    return jnp.dot(a, b, preferred_element_type=F32)


def _dot_nt(a, b):
    return lax.dot_general(a, b, (((1,), (1,)), ((), ())), preferred_element_type=F32)


def _group_of(blk, nblk_b, nb):
    t = blk % nblk_b
    return jnp.where(t == 0, nb, blk // nblk_b), t


def _mod_kernel(c_ref, w_ref, b_ref, o_ref):
    c = c_ref[...]
    s = (c * _sigmoid(c)).astype(BF16)
    o_ref[0] = _dot(s, w_ref[0].astype(BF16)) + b_ref[0]


def _modulation(cvec, w_mod, b_mod):
    depth, d, n = w_mod.shape
    tn = 1536
    return pl.pallas_call(
        _mod_kernel,
        out_shape=jax.ShapeDtypeStruct((depth, MOD_ROWS, n), F32),
        grid=(depth, n // tn),
        in_specs=[
            pl.BlockSpec((MOD_ROWS, d), lambda l, j: (0, 0)),
            pl.BlockSpec((1, d, tn), lambda l, j: (l, 0, j)),
            pl.BlockSpec((1, 1, tn), lambda l, j: (l, 0, j)),
        ],
        out_specs=pl.BlockSpec((1, MOD_ROWS, tn), lambda l, j: (l, 0, j)),
        compiler_params=_cparams(("parallel", "parallel")),
        name="modulation",
    )(cvec, w_mod, b_mod.reshape(depth, 1, n))


def _inproj_kernel(x_ref, mod_ref, w_ref, o_ref, og_ref, h_scr, *, nsub, nblk_b, nb, d, nj):
    i = pl.program_id(0)
    j = pl.program_id(1)

    @pl.when(j == 0)
    def _():
        for s in range(nsub):
            g, _ = _group_of(i * nsub + s, nblk_b, nb)
            shift = mod_ref[0, pl.ds(g, 1), 0:d]
            scale = mod_ref[0, pl.ds(g, 1), d:2 * d]
            xs = x_ref[s * SUB:(s + 1) * SUB, :]
            h_scr[s * SUB:(s + 1) * SUB, :] = (xs * (1.0 + scale) + shift).astype(BF16)

    acc = _dot(h_scr[...], w_ref[...])
    o_ref[...] = acc.astype(BF16)

    @pl.when(j == nj - 1)
    def _():
        og_ref[...] = acc[:, acc.shape[1] - 128:]


def _inproj(xu, mod, w_in_p, l, *, nblk_b, nb):
    rows, d = xu.shape
    n = w_in_p.shape[2]
    tm, tn = 2 * SUB, n // 2
    nj = n // tn
    kern = functools.partial(_inproj_kernel, nsub=tm // SUB, nblk_b=nblk_b, nb=nb, d=d, nj=nj)
    return pl.pallas_call(
        kern,
        out_shape=(jax.ShapeDtypeStruct((rows, n), BF16), jax.ShapeDtypeStruct((rows, 128), F32)),
        grid=(rows // tm, nj),
        in_specs=[
            pl.BlockSpec((tm, d), lambda i, j: (i, 0)),
            pl.BlockSpec((1, MOD_ROWS, 6 * d), lambda i, j: (l, 0, 0)),
            pl.BlockSpec((None, d, tn), lambda i, j: (l, 0, j)),
        ],
        out_specs=(pl.BlockSpec((tm, tn), lambda i, j: (i, j)),
                   pl.BlockSpec((tm, 128), lambda i, j: (i, 0))),
        scratch_shapes=[pltpu.VMEM((tm, d), BF16)],
        compiler_params=_cparams(("parallel", "arbitrary")),
        name="inproj",
    )(xu, mod, w_in_p)


def _rope_slab(x, cos, sin):
    lane = lax.broadcasted_iota(jnp.int32, x.shape, 1)
    odd = (lane & 16) != 0
    partner = jnp.where(odd, pltpu.roll(x, 16, 1), pltpu.roll(x, 128 - 16, 1))
    return x * cos + partner * sin


def _store_vt(vt_ref, v):
    vt = v.T.astype(BF16)
    rows = v.shape[0]
    ext = jnp.where(lax.broadcasted_iota(jnp.int32, (VT_ROWS - HEAD_DIM, rows), 0) == 0, 1.0, 0.0).astype(BF16)
    for h in range(HEADS):
        vt_ref[0, h * VT_ROWS:h * VT_ROWS + HEAD_DIM, :] = vt[h * HEAD_DIM:(h + 1) * HEAD_DIM, :]
        vt_ref[0, h * VT_ROWS + HEAD_DIM:(h + 1) * VT_ROWS, :] = ext


def _diff_prep_kernel(p_ref, v_ref, cos_ref, sin_ref, o_ref, vt_ref, *, scale):
    _store_vt(vt_ref, v_ref[...].astype(F32))
    cos = cos_ref[...]
    sin = sin_ref[...]
    for s in range(2 * HEADS):
        x = p_ref[:, s * 128:(s + 1) * 128].astype(F32)
        r = _rope_slab(x, cos, sin)
        if s < HEADS:
            r = r * scale
        o_ref[:, s * 128:(s + 1) * 128] = r.astype(BF16)


def _diff_prep(pq, cos, sin, *, nblk_b):
    rows = pq.shape[0]
    tm = SUB
    kern = functools.partial(_diff_prep_kernel, scale=(HEAD_DIM // 2) ** -0.5 * LOG2E)
    return pl.pallas_call(
        kern,
        out_shape=(jax.ShapeDtypeStruct((rows, 2 * GROUP_W), BF16),
                   jax.ShapeDtypeStruct((rows // tm, HEADS * VT_ROWS, tm), BF16)),
        grid=(rows // tm,),
        in_specs=[
            pl.BlockSpec((tm, 2 * GROUP_W), lambda i: (i, C_DQ // (2 * GROUP_W))),
            pl.BlockSpec((tm, GROUP_W), lambda i: (i, C_DV // GROUP_W)),
            pl.BlockSpec((tm, 128), lambda i: (i % nblk_b, 0)),
            pl.BlockSpec((tm, 128), lambda i: (i % nblk_b, 0)),
        ],
        out_specs=(pl.BlockSpec((tm, 2 * GROUP_W), lambda i: (i, 0)),
                   pl.BlockSpec((1, HEADS * VT_ROWS, tm), lambda i: (i, 0, 0))),
        compiler_params=_cparams(("parallel",)),
        name="diff_prep",
    )(pq, pq, cos, sin)


def _mla_prep_kernel(p_ref, gq_ref, gkv_ref, wq_ref, wkv_ref, cos_ref, sin_ref,
                     q_ref, k_ref, v_ref, *, scale):
    cos = cos_ref[...]
    sin = sin_ref[...]
    cq = p_ref[:, 0:MLA_Q_RANK].astype(F32)
    cq = cq * lax.rsqrt(jnp.mean(cq * cq, axis=-1, keepdims=True) + RMS_EPS) * gq_ref[...]
    q = _dot(cq.astype(BF16), wq_ref[...])
    ckv = p_ref[:, MLA_Q_RANK:MLA_Q_RANK + MLA_KV_RANK].astype(F32)
    ckv = ckv * lax.rsqrt(jnp.mean(ckv * ckv, axis=-1, keepdims=True) + RMS_EPS) * gkv_ref[...]
    kv = _dot(ckv.astype(BF16), wkv_ref[...])
    kr = _rope_slab(p_ref[:, MLA_Q_RANK + MLA_KV_RANK:MLA_W].astype(F32), cos, sin).astype(BF16)
    for h in range(HEADS):
        q_ref[:, h * 256:h * 256 + 128] = (q[:, h * 256:h * 256 + 128] * scale).astype(BF16)
        qr = _rope_slab(q[:, h * 256 + 128:(h + 1) * 256], cos, sin)
        q_ref[:, h * 256 + 128:(h + 1) * 256] = (qr * scale).astype(BF16)
        k_ref[:, h * 256:h * 256 + 128] = kv[:, h * 128:(h + 1) * 128].astype(BF16)
        k_ref[:, h * 256 + 128:(h + 1) * 256] = kr
    _store_vt(v_ref, kv[:, GROUP_W:])


def _mla_prep(pq, gq, gkv, wq, wkv, cos, sin, *, nblk_b):
    rows = pq.shape[0]
    tm = SUB
    kern = functools.partial(_mla_prep_kernel, scale=(MLA_NOPE + MLA_ROPE) ** -0.5 * LOG2E)
    return pl.pallas_call(
        kern,
        out_shape=(jax.ShapeDtypeStruct((rows, HEADS * 256), BF16),
                   jax.ShapeDtypeStruct((rows, HEADS * 256), BF16),
                   jax.ShapeDtypeStruct((rows // tm, HEADS * VT_ROWS, tm), BF16)),
        grid=(rows // tm,),
        in_specs=[
            pl.BlockSpec((tm, MLA_W), lambda i: (i, C_MLA // MLA_W)),
            pl.BlockSpec((1, MLA_Q_RANK), lambda i: (0, 0)),
            pl.BlockSpec((1, MLA_KV_RANK), lambda i: (0, 0)),
            pl.BlockSpec((MLA_Q_RANK, HEADS * 256), lambda i: (0, 0)),
            pl.BlockSpec((MLA_KV_RANK, 2 * GROUP_W), lambda i: (0, 0)),
            pl.BlockSpec((tm, 128), lambda i: (i % nblk_b, 0)),
            pl.BlockSpec((tm, 128), lambda i: (i % nblk_b, 0)),
        ],
        out_specs=(pl.BlockSpec((tm, HEADS * 256), lambda i: (i, 0)),
                   pl.BlockSpec((tm, HEADS * 256), lambda i: (i, 0)),
                   pl.BlockSpec((1, HEADS * VT_ROWS, tm), lambda i: (i, 0, 0))),
        compiler_params=_cparams(("parallel",)),
        name="mla_prep",
    )(pq, gq, gkv, wq, wkv, cos, sin)


def _flash_kernel(*refs, nsub, hp, dq, n_lat_pairs):
    if nsub == 2:
        lam_ref, g_ref, q_ref, k_ref, vt_ref, o_ref, s_scr, p_scr, acc_scr = refs
    else:
        q_ref, k_ref, vt_ref, o_ref, s_scr, p_scr, acc_scr = refs
    qt = pl.program_id(2)
    tq = q_ref.shape[0]
    tk = 2 * SUB
    qs = []
    for hh in range(hp):
        q = q_ref[:, hh * dq:(hh + 1) * dq]
        if nsub == 2:
            lane = lax.broadcasted_iota(jnp.int32, q.shape, 1)
            zero = jnp.zeros_like(q)
            qs.append((hh, jnp.where(lane < 64, q, zero)))
            qs.append((hh, jnp.where(lane >= 64, q, zero)))
        else:
            qs.append((hh, q))
    nch = len(qs)

    def scores_into(slot, rows, nrows):
        maxes = []
        for c, (hh, qi) in enumerate(qs):
            s = _dot_nt(k_ref[rows, hh * dq:(hh + 1) * dq], qi)
            s_scr[slot, c, 0:nrows, :] = s
            maxes.append(jnp.max(s, axis=0, keepdims=True))
        return maxes

    def pv_from(slot, vblocks):
        pvs = []
        for c, (hh, _) in enumerate(qs):
            pv = None
            for i, blk in enumerate(vblocks):
                part = _dot(vt_ref[blk, hh * VT_ROWS:(hh + 1) * VT_ROWS, :],
                            p_scr[slot, c, i * SUB:(i + 1) * SUB, :])
                pv = part if pv is None else pv + part
            pvs.append(pv)
        return pvs

    def softmax(slot, nrows, stats, maxes, pvs=None):
        ch = 64
        new = []
        for c, (m, mb) in enumerate(zip(stats, maxes)):
            m_new = jnp.maximum(m, mb)
            for r in range(0, nrows, ch):
                p_scr[slot, c, r:r + ch, :] = jnp.exp2(s_scr[slot, c, r:r + ch, :] - m_new).astype(BF16)
            acc_scr[c] = (acc_scr[c] if pvs is None else acc_scr[c] + pvs[c]) * jnp.exp2(m - m_new)
            new.append(m_new)
        return new

    def lat_rows(j):
        return pl.ds(pl.multiple_of(SUB + j * tk, SUB), tk)

    acc_scr[...] = jnp.zeros_like(acc_scr)
    stats = [jnp.full((1, tq), NEG, F32) for _ in qs]
    mx_ctx = scores_into(1, slice(0, SUB), SUB)
    mx0 = scores_into(0, lat_rows(0), tk)
    p_scr[1, :, SUB:, :] = jnp.zeros((nch, SUB, tq), BF16)
    stats = softmax(1, SUB, stats, mx_ctx)
    unroll = 2 if n_lat_pairs % 2 == 0 else 1
    n = jnp.where(qt == 0, 0, n_lat_pairs // unroll)
    last = 2 * n_lat_pairs - 1

    def pair(a, stats, mx0):
        bp = jnp.where(a == 0, 0, 2 * a - 1)
        pvs = pv_from(1, (bp, bp + 1))
        mx1 = scores_into(1, lat_rows(a + 1), tk)
        stats = softmax(0, tk, stats, mx0, pvs)
        pvs = pv_from(0, (1 + 2 * a, 2 + 2 * a))
        mx0 = scores_into(0, lat_rows(jnp.minimum(a + 2, last)), tk)
        return softmax(1, tk, stats, mx1, pvs), mx0

    def body(jj, carry):
        stats, mx0 = list(carry[:nch]), list(carry[nch:])
        for u in range(unroll):
            stats, mx0 = pair(2 * (jj * unroll + u), stats, mx0)
        return tuple(stats) + tuple(mx0)

    lax.fori_loop(0, n, body, tuple(stats) + tuple(mx0))
    done = 2 * unroll * n
    bl = jnp.where(done == 0, 0, 2 * done - 1)
    for c, pv in enumerate(pv_from(1, (bl, bl + 1))):
        acc_scr[c] += pv

    for hh in range(hp):
        def normalised(c):
            return acc_scr[c, 0:HEAD_DIM, :] / acc_scr[c, HEAD_DIM:HEAD_DIM + 1, :]

        if nsub == 2:
            o = (normalised(2 * hh) - lam_ref[0] * normalised(2 * hh + 1)).T
            o = o * lax.rsqrt(jnp.mean(o * o, axis=-1, keepdims=True) + RMS_EPS) * g_ref[...] * lam_ref[1]
        else:
            o = normalised(hh).T
        o_ref[:, hh * HEAD_DIM:(hh + 1) * HEAD_DIM] = o.astype(o_ref.dtype)


def _flash(q_arr, k_arr, vt_arr, *, nb, rb, dq, qcol, kcol, nsub=1, lam=None, norm_g=None):
    rows = q_arr.shape[0]
    nblk_b = rb // SUB
    lat = rb - SUB
    hp = 2
    assert lat % (2 * SUB) == 0 and qcol % (hp * dq) == 0 and kcol % (hp * dq) == 0
    assert lat % (4 * SUB) == 0
    nch = hp * nsub
    kern = functools.partial(_flash_kernel, nsub=nsub, hp=hp, dq=dq, n_lat_pairs=lat // (4 * SUB))
    in_specs = [
        pl.BlockSpec((SUB, hp * dq), lambda b, h, t: (b * nblk_b + t, qcol // (hp * dq) + h)),
        pl.BlockSpec((rb, hp * dq), lambda b, h, t: (b, kcol // (hp * dq) + h)),
        pl.BlockSpec((nblk_b, hp * VT_ROWS, SUB), lambda b, h, t: (b, h, 0)),
    ]
    args = [q_arr, k_arr, vt_arr]
    if nsub == 2:
        in_specs = [pl.BlockSpec(memory_space=pltpu.SMEM),
                    pl.BlockSpec((1, HEAD_DIM), lambda b, h, t: (0, 0))] + in_specs
        args = [lam, norm_g] + args
    return pl.pallas_call(
        kern,
        out_shape=jax.ShapeDtypeStruct((rows, GROUP_W), BF16),
        grid=(nb, HEADS // hp, nblk_b),
        in_specs=in_specs,
        out_specs=pl.BlockSpec((SUB, hp * HEAD_DIM), lambda b, h, t: (b * nblk_b + t, h)),
        scratch_shapes=[pltpu.VMEM((2, nch, 2 * SUB, SUB), F32), pltpu.VMEM((2, nch, 2 * SUB, SUB), BF16),
                        pltpu.VMEM((nch, VT_ROWS, SUB), F32)],
        compiler_params=_cparams(("parallel", "parallel", "arbitrary")),
        name="flash_diff" if nsub == 2 else "flash_mla",
    )(*args)


def _na_kernel(q_ref, k_ref, v_ref, bias_ref, o_ref, *, grid_rows, scale, hp):
    qt = pl.program_id(2)
    qb = jnp.maximum(qt - 1, 0)
    u0 = jnp.clip(qb * NA_BLOCK_ROWS - NA_KR // 2, 0, grid_rows - NA_UNION_ROWS)
    start = pl.multiple_of(SUB + u0 * GRID_W, GRID_W)
    nn = NA_UNION_ROWS * GRID_W
    hs = [slice(h * HEAD_DIM, (h + 1) * HEAD_DIM) for h in range(hp)]
    qs = [(q_ref[:, sl].astype(F32) * scale).astype(BF16) for sl in hs]
    s_cs = [_dot_nt(q, k_ref[0:SUB, sl]) for q, sl in zip(qs, hs)]
    s_ns = [_dot_nt(q, k_ref[pl.ds(start, nn), sl]) + bias_ref[h, 0] for h, (q, sl) in enumerate(zip(qs, hs))]
    ms = [jnp.maximum(jnp.max(s_c, axis=-1, keepdims=True), jnp.max(s_n, axis=-1, keepdims=True))
          for s_c, s_n in zip(s_cs, s_ns)]
    e_cs = [jnp.exp2(s_c - m) for s_c, m in zip(s_cs, ms)]
    e_ns = [jnp.exp2(s_n - m) for s_n, m in zip(s_ns, ms)]
    for sl, e_c, e_n in zip(hs, e_cs, e_ns):
        l = jnp.sum(e_c, axis=-1, keepdims=True) + jnp.sum(e_n, axis=-1, keepdims=True)
        o = (_dot(e_c.astype(BF16), v_ref[0:SUB, sl]) + _dot(e_n.astype(BF16), v_ref[pl.ds(start, nn), sl])) / l
        o_ref[:, sl] = o.astype(o_ref.dtype)


def _na_bias_table(rpb, grid_rows):
    assert grid_rows >= NA_UNION_ROWS + NA_BLOCK_ROWS
    j = np.arange(NA_BLOCK_ROWS)[:, None]
    iu = np.arange(NA_UNION_ROWS)[None, :]
    cq = np.arange(GRID_W)[:, None]
    ck = np.arange(GRID_W)[None, :]
    cs = np.clip(cq - NA_KC // 2, 0, GRID_W - NA_KC)
    col_ok = (ck >= cs) & (ck < cs + NA_KC)
    dx = np.clip(ck - cq + NA_KC - 1, 0, 2 * NA_KC - 2)
    ex = np.eye(2 * NA_KC - 1, dtype=np.float32)[dx]
    eys, oks = [], []
    for r_off, w_off in ((j, 0 * j), (j + NA_KR // 2, j),
                         (j + NA_UNION_ROWS - NA_BLOCK_ROWS, 0 * j + NA_UNION_ROWS - NA_KR)):
        dy = np.clip(iu - r_off + NA_KR - 1, 0, 2 * NA_KR - 2)
        eys.append(np.eye(2 * NA_KR - 1, dtype=np.float32)[dy])
        oks.append((iu - w_off >= 0) & (iu - w_off < NA_KR))
    ey = np.stack(eys)
    ok = np.stack(oks)[:, :, None, :, None] & col_ok[None, None, :, None, :]
    vals = jnp.einsum("tjia,lhab,qkb->lhtjqik", ey, rpb.astype(F32), ex, precision=lax.Precision.HIGHEST)
    nb_part = jnp.where(ok[None, None], vals * LOG2E, NEG)
    nb_part = jnp.concatenate([nb_part, jnp.full_like(nb_part[:, :, :1], NEG)], axis=2)
    return nb_part.reshape(rpb.shape[:2] + (4, NA_BLOCK_ROWS * GRID_W, NA_UNION_ROWS * GRID_W))


def _na(pq, bias, l, *, nb, rb):
    rows = pq.shape[0]
    nblk_b = rb // SUB
    grid_rows = (rb - SUB) // GRID_W
    nqb = nblk_b - 1
    hp = 2
    hw = hp * HEAD_DIM
    kern = functools.partial(_na_kernel, grid_rows=grid_rows, scale=HEAD_DIM ** -0.5 * LOG2E, hp=hp)

    def bias_map(b, h, t):
        ty = jnp.where(t == 0, 3, jnp.where(t == 1, 0, jnp.where(t == nqb, 2, 1)))
        return (l, h, ty, 0, 0)

    return pl.pallas_call(
        kern,
        out_shape=jax.ShapeDtypeStruct((rows, GROUP_W), BF16),
        grid=(nb, HEADS // hp, nblk_b),
        in_specs=[
            pl.BlockSpec((SUB, hw), lambda b, h, t: (b * nblk_b + t, C_NQ // hw + h)),
            pl.BlockSpec((rb, hw), lambda b, h, t: (b, C_NK // hw + h)),
            pl.BlockSpec((rb, hw), lambda b, h, t: (b, C_NV // hw + h)),
            pl.BlockSpec((None, hp, 1, SUB, bias.shape[-1]), bias_map),
        ],
        out_specs=pl.BlockSpec((SUB, hw), lambda b, h, t: (b * nblk_b + t, h)),
        compiler_params=_cparams(("parallel", "parallel", "arbitrary")),
        name="na",
    )(pq, pq, pq, bias)


def _gdn_prep_kernel(x_ref, prev_ref, next_ref, w_ref, g_ref, av_ref, dtb_ref, o_ref, og_ref, *, nblk_b):
    i = pl.program_id(0)
    t = i % nblk_b
    keep_prev = jnp.where((t == 0) | (t == 1), 0.0, 1.0)
    keep_next = jnp.where((t == 0) | (t == nblk_b - 1), 0.0, 1.0)
    x = x_ref[...].astype(F32)
    tm = x.shape[0]
    row = lax.broadcasted_iota(jnp.int32, x.shape, 0)
    halo_p = prev_ref[15:16, :].astype(F32) * keep_prev
    halo_n = next_ref[0:1, :].astype(F32) * keep_next
    xp = jnp.where(row == 0, halo_p, pltpu.roll(x, 1, 0))
    xn = jnp.where(row == tm - 1, halo_n, pltpu.roll(x, tm - 1, 0))
    c = xp * w_ref[0:1, :] + x * w_ref[1:2, :] + xn * w_ref[2:3, :]
    s = c * _sigmoid(c)
    for h in range(2 * HEADS):
        sh = s[:, h * 128:(h + 1) * 128]
        nrm = lax.rsqrt(jnp.sum(sh * sh, axis=-1, keepdims=True) + RMS_EPS)
        if h < HEADS:
            nrm = nrm * HEAD_DIM ** -0.5
        o_ref[:, h * 128:(h + 1) * 128] = (sh * nrm).astype(BF16)
    o_ref[:, 2 * GROUP_W:] = s[:, 2 * GROUP_W:].astype(BF16)
    g = g_ref[...]
    z = g + dtb_ref[...]
    softplus = jnp.maximum(z, 0.0) + jnp.log(1.0 + jnp.exp(-jnp.abs(z)))
    lane = lax.broadcasted_iota(jnp.int32, g.shape, 1)
    og_ref[...] = jnp.where(lane < 2 * HEADS, -av_ref[...] * softplus, _sigmoid(g))


def _gdn_prep(pq, pg, conv_w, a_vec, dtb_vec, *, nblk_b):
    rows = pq.shape[0]
    tm = SUB
    w3 = 3 * GROUP_W
    nhalo = rows // 16
    kern = functools.partial(_gdn_prep_kernel, nblk_b=nblk_b)
    return pl.pallas_call(
        kern,
        out_shape=(jax.ShapeDtypeStruct((rows, w3), BF16), jax.ShapeDtypeStruct((rows, 128), F32)),
        grid=(rows // tm,),
        in_specs=[
            pl.BlockSpec((tm, w3), lambda i: (i, 0)),
            pl.BlockSpec((16, w3), lambda i: (jnp.maximum(i * (tm // 16) - 1, 0), 0)),
            pl.BlockSpec((16, w3), lambda i: (jnp.minimum((i + 1) * (tm // 16), nhalo - 1), 0)),
            pl.BlockSpec((3, w3), lambda i: (0, 0)),
            pl.BlockSpec((tm, 128), lambda i: (i, 0)),
            pl.BlockSpec((1, 128), lambda i: (0, 0)),
            pl.BlockSpec((1, 128), lambda i: (0, 0)),
        ],
        out_specs=(pl.BlockSpec((tm, w3), lambda i: (i, 0)), pl.BlockSpec((tm, 128), lambda i: (i, 0))),
        compiler_params=_cparams(("parallel",)),
        name="gdn_prep",
    )(pq, pq, pq, conv_w, pg, a_vec, dtb_vec)


def _split3(x):
    hi = x.astype(BF16)
    r = x - hi.astype(F32)
    mid = r.astype(BF16)
    lo = (r - mid.astype(F32)).astype(BF16)
    return hi, mid, lo


def _gdn_chains(chains):
    c = chains[0][0].shape[0]
    row = lax.broadcasted_iota(jnp.int32, (c, c), 0)
    col = lax.broadcasted_iota(jnp.int32, (c, c), 1)
    xor = row ^ col
    eye = jnp.where(row == col, 1.0, 0.0)
    a_mats, qks, kbs, kfs, egs = [], [], [], [], []
    for q, k, v, gcol, grow, beta, glast, s_prev, lower in chains:
        incl = (row >= col) if lower else (row <= col)
        strict = (row > col) if lower else (row < col)
        decay = jnp.exp(jnp.where(incl, gcol - grow, NEG))
        kf = k.astype(F32)
        kb = kf * beta
        a_mats.append(jnp.where(strict, _dot_nt(kb.astype(BF16), k) * decay, 0.0))
        qks.append(jnp.where(incl, _dot_nt(q, k) * decay, 0.0).astype(BF16))
        kbs.append(kb)
        kfs.append(kf)
        egs.append(jnp.exp(gcol))
    tinvs = [eye - jnp.where(xor < 2, a, 0.0) for a in a_mats]
    s = 2
    while s < c:
        ys = [_dot(jnp.where((xor >= s) & (xor < 2 * s), a, 0.0).astype(BF16), t.astype(BF16))
              for a, t in zip(a_mats, tinvs)]
        tinvs = [t - _dot(t.astype(BF16), y.astype(BF16)) for t, y in zip(tinvs, ys)]
        s *= 2
    wus = []
    for (q, k, v, gcol, grow, beta, glast, s_prev, lower), kb, eg, t in zip(chains, kbs, egs, tinvs):
        rhs = jnp.concatenate([kb * eg, v.astype(F32) * beta], axis=1).astype(BF16)
        wus.append(_dot(t.astype(BF16), rhs))
    sbs = [ch[7].astype(BF16) for ch in chains]
    vnbs = [(wu[:, HEAD_DIM:] - _dot(wu[:, :HEAD_DIM].astype(BF16), sb)).astype(BF16) for wu, sb in zip(wus, sbs)]
    outs = []
    for (q, k, v, gcol, grow, beta, glast, s_prev, lower), kf, eg, qk, sb, vnb in zip(chains, kfs, egs, qks, sbs, vnbs):
        o = _dot((q.astype(F32) * eg).astype(BF16), sb) + _dot(qk, vnb)
        kd_t = (kf * jnp.exp(glast - gcol)).T.astype(BF16)
        s_new = s_prev * jnp.exp(glast) + _dot(kd_t, vnb)
        outs.append((o, s_new))
    return outs


def _gdn_kernel(xf_ref, xb_ref, gf_ref, gb_ref, of_ref, ob_ref, s_scr):
    st = pl.program_id(1)

    @pl.when(st == 0)
    def _():
        s_scr[...] = jnp.zeros_like(s_scr)

    c = GDN_CHUNK
    row = lax.broadcasted_iota(jnp.int32, (c, c), 0)
    col = lax.broadcasted_iota(jnp.int32, (c, c), 1)
    chains = []
    for d, (x_ref, g_ref) in enumerate(((xf_ref, gf_ref), (xb_ref, gb_ref))):
        lower = d == 0
        gates = g_ref[...]
        tri = jnp.where((row >= col) if lower else (row <= col), 1.0, 0.0).astype(BF16)
        hi, mid, lo = _split3(gates)
        gsum = _dot(tri, hi) + _dot(tri, mid) + _dot(tri, lo)
        gsum_t = gsum.T
        for h in range(HEADS):
            gi = d * HEADS + h
            bi = 2 * HEADS + gi
            glast = gsum[c - 1:c, gi:gi + 1] if lower else gsum[0:1, gi:gi + 1]
            chains.append((x_ref[:, h * 128:(h + 1) * 128],
                           x_ref[:, GROUP_W + h * 128:GROUP_W + (h + 1) * 128],
                           x_ref[:, 2 * GROUP_W + h * 128:2 * GROUP_W + (h + 1) * 128],
                           gsum[:, gi:gi + 1], gsum_t[gi:gi + 1, :], gates[:, bi:bi + 1], glast,
                           s_scr[d, h], lower))
    outs = _gdn_chains(chains)
    for idx, (o, s_new) in enumerate(outs):
        d, h = divmod(idx, HEADS)
        s_scr[d, h] = s_new
        (of_ref, ob_ref)[d][:, h * 128:(h + 1) * 128] = o


def _gdn(gx, gg, *, nb, rb):
    rows = gx.shape[0]
    c = GDN_CHUNK
    nch = rb // c
    nctx = SUB // c
    w3 = 3 * GROUP_W

    def fmap(b, s):
        return (b * nch + s, 0)

    def bmap(b, s):
        return (b * nch + jnp.where(s < nctx, nctx - 1 - s, nch + nctx - 1 - s), 0)

    return pl.pallas_call(
        _gdn_kernel,
        out_shape=(jax.ShapeDtypeStruct((rows, GROUP_W), F32), jax.ShapeDtypeStruct((rows, GROUP_W), F32)),
        grid=(nb, nch),
        in_specs=[
            pl.BlockSpec((c, w3), fmap), pl.BlockSpec((c, w3), bmap),
            pl.BlockSpec((c, 128), fmap), pl.BlockSpec((c, 128), bmap),
        ],
        out_specs=(pl.BlockSpec((c, GROUP_W), fmap), pl.BlockSpec((c, GROUP_W), bmap)),
        scratch_shapes=[pltpu.VMEM((2, HEADS, HEAD_DIM, HEAD_DIM), F32)],
        compiler_params=_cparams(("parallel", "arbitrary")),
        name="gdn",
    )(gx, gx, gg, gg)


def _layernorm_rows(z, g, b):
    mu = jnp.mean(z, axis=-1, keepdims=True)
    zc = z - mu
    var = jnp.mean(zc * zc, axis=-1, keepdims=True)
    return zc * lax.rsqrt(var + LN_EPS) * g + b


def _outproj_kernel(of_ref, ob_ref, z_ref, gn_ref, yb_ref, ym_ref, yn_ref, w_ref, x_ref, mod_ref, g_ref, b_ref,
                    o_ref, *, nsub, nblk_b, nb, d, alpha):
    i = pl.program_id(0)
    z = z_ref[...].astype(F32)
    gate = z * _sigmoid(z)
    ya = []
    for h in range(HEADS):
        sl = slice(h * HEAD_DIM, (h + 1) * HEAD_DIM)
        o = of_ref[:, sl] + ob_ref[:, sl]
        o = o * lax.rsqrt(jnp.mean(o * o, axis=-1, keepdims=True) + RMS_EPS) * gn_ref[...]
        ya.append((o * gate[:, sl]).astype(BF16))
    y = _dot(jnp.concatenate(ya, axis=1), w_ref[0:GROUP_W, :])
    y = y + _dot(yb_ref[...], w_ref[GROUP_W:2 * GROUP_W, :])
    y = y + _dot(ym_ref[...], w_ref[2 * GROUP_W:3 * GROUP_W, :])
    y = y + _dot(yn_ref[...], w_ref[3 * GROUP_W:4 * GROUP_W, :])
    for s in range(nsub):
        g, _ = _group_of(i * nsub + s, nblk_b, nb)
        gate = mod_ref[0, pl.ds(g, 1), 2 * d:3 * d]
        sl = slice(s * SUB, (s + 1) * SUB)
        z = alpha * x_ref[sl, :] + gate * y[sl, :]
        o_ref[sl, :] = _layernorm_rows(z, g_ref[...], b_ref[...])


def _outproj(gdn, ys, w_out, xu, mod, ln_g, ln_b, l, *, nblk_b, nb, alpha):
    o_f, o_b, pq, norm_g = gdn
    rows, d = xu.shape
    tm = 2 * SUB
    kern = functools.partial(_outproj_kernel, nsub=tm // SUB, nblk_b=nblk_b, nb=nb, d=d, alpha=alpha)
    yspec = pl.BlockSpec((tm, GROUP_W), lambda i: (i, 0))
    return pl.pallas_call(
        kern,
        out_shape=jax.ShapeDtypeStruct((rows, d), F32),
        grid=(rows // tm,),
        in_specs=[yspec, yspec,
                  pl.BlockSpec((tm, GROUP_W), lambda i: (i, C_GZ // GROUP_W)),
                  pl.BlockSpec((1, HEAD_DIM), lambda i: (0, 0)),
                  yspec, yspec, yspec,
                  pl.BlockSpec((None, 4 * GROUP_W, d), lambda i: (l, 0, 0)),
                  pl.BlockSpec((tm, d), lambda i: (i, 0)),
                  pl.BlockSpec((1, MOD_ROWS, 6 * d), lambda i: (l, 0, 0)),
                  pl.BlockSpec((1, d), lambda i: (0, 0)),
                  pl.BlockSpec((1, d), lambda i: (0, 0))],
        out_specs=pl.BlockSpec((tm, d), lambda i: (i, 0)),
        compiler_params=_cparams(("parallel",)),
        name="outproj",
    )(o_f, o_b, pq, norm_g, *ys, w_out, xu, mod, ln_g, ln_b)


def _ffn_kernel(x_ref, prev_ref, next_ref, mod_ref, wg_ref, wv_ref, cg_ref, cv_ref, wd_ref, g_ref, b_ref,
                o_ref, h_scr, hb_scr, acc_scr, *, nsub, nblk_b, nb, d, nj, alpha):
    i = pl.program_id(0)
    j = pl.program_id(1)
    tm = nsub * SUB

    @pl.when(j == 0)
    def _():
        acc_scr[...] = jnp.zeros_like(acc_scr)
        for s in range(nsub):
            g, _ = _group_of(i * nsub + s, nblk_b, nb)
            shift = mod_ref[0, pl.ds(g, 1), 3 * d:4 * d]
            scale = mod_ref[0, pl.ds(g, 1), 4 * d:5 * d]
            h_scr[8 + s * SUB:8 + (s + 1) * SUB, :] = x_ref[s * SUB:(s + 1) * SUB, :] * (1.0 + scale) + shift
            if s == 0:
                h_scr[0:8, :] = prev_ref[...] * (1.0 + scale) + shift
            if s == nsub - 1:
                h_scr[tm + 8:tm + 16, :] = next_ref[...] * (1.0 + scale) + shift
        hb_scr[...] = h_scr[...].astype(BF16)

    row = lax.broadcasted_iota(jnp.int32, (tm, 1), 0)
    keep_prev = jnp.ones((tm, 1), F32)
    keep_next = jnp.ones((tm, 1), F32)
    for s in range(nsub):
        _, t = _group_of(i * nsub + s, nblk_b, nb)
        first = (t == 0) | (t == 1)
        last = (t == 0) | (t == nblk_b - 1)
        keep_prev = jnp.where((row == s * SUB) & first, 0.0, keep_prev)
        keep_next = jnp.where((row == s * SUB + SUB - 1) & last, 0.0, keep_next)

    hb = hb_scr[...]
    ext = tm + 16

    def conv(u, cw_ref):
        up = pltpu.roll(u, 1, 0)[8:tm + 8, :] * keep_prev
        un = pltpu.roll(u, ext - 1, 0)[8:tm + 8, :] * keep_next
        return up * cw_ref[0:1, :] + u[8:tm + 8, :] * cw_ref[1:2, :] + un * cw_ref[2:3, :]

    gate = conv(_dot(hb, wg_ref[...]), cg_ref)
    val = conv(_dot(hb, wv_ref[...]), cv_ref)
    act = (gate * _sigmoid(gate) * val).astype(BF16)
    acc_scr[...] += _dot(act, wd_ref[...])

    @pl.when(j == nj - 1)
    def _():
        for s in range(nsub):
            g, _ = _group_of(i * nsub + s, nblk_b, nb)
            gt = mod_ref[0, pl.ds(g, 1), 5 * d:6 * d]
            sl = slice(s * SUB, (s + 1) * SUB)
            z = alpha * x_ref[sl, :] + gt * acc_scr[sl, :]
            o_ref[sl, :] = _layernorm_rows(z, g_ref[...], b_ref[...])


def _ffn(xu, mod, w_up, conv_w, w_down, ln_g, ln_b, l, *, nblk_b, nb, alpha):
    rows, d = xu.shape
    dff = w_down.shape[1]
    tm, tc = 2 * SUB, 512
    nj = dff // tc
    nhalo = rows // 8
    kern = functools.partial(_ffn_kernel, nsub=tm // SUB, nblk_b=nblk_b, nb=nb, d=d, nj=nj, alpha=alpha)
    return pl.pallas_call(
        kern,
        out_shape=jax.ShapeDtypeStruct((rows, d), F32),
        grid=(rows // tm, nj),
        in_specs=[
            pl.BlockSpec((tm, d), lambda i, j: (i, 0)),
            pl.BlockSpec((8, d), lambda i, j: (jnp.maximum(i * (tm // 8) - 1, 0), 0)),
            pl.BlockSpec((8, d), lambda i, j: (jnp.minimum((i + 1) * (tm // 8), nhalo - 1), 0)),
            pl.BlockSpec((1, MOD_ROWS, 6 * d), lambda i, j: (l, 0, 0)),
            pl.BlockSpec((None, d, tc), lambda i, j: (l, 0, j)),
            pl.BlockSpec((None, d, tc), lambda i, j: (l, 0, nj + j)),
            pl.BlockSpec((None, 3, tc), lambda i, j: (l, 0, j)),
            pl.BlockSpec((None, 3, tc), lambda i, j: (l, 0, nj + j)),
            pl.BlockSpec((None, tc, d), lambda i, j: (l, j, 0)),
            pl.BlockSpec((1, d), lambda i, j: (0, 0)),
            pl.BlockSpec((1, d), lambda i, j: (0, 0)),
        ],
        out_specs=pl.BlockSpec((tm, d), lambda i, j: (i, 0)),
        scratch_shapes=[pltpu.VMEM((tm + 16, d), F32), pltpu.VMEM((tm + 16, d), BF16), pltpu.VMEM((tm, d), F32)],
        compiler_params=_cparams(("parallel", "arbitrary")),
        name="ffn",
    )(xu, xu, xu, mod, w_up, w_up, conv_w, conv_w, w_down, ln_g, ln_b)


def _rope_tables(n_lat):
    pos = jnp.arange(n_lat)
    row = (pos // GRID_W).astype(F32)
    col = (pos % GRID_W).astype(F32)
    half = MLA_ROPE // 2
    inv = ROPE_THETA ** (-jnp.arange(0, half, 2, dtype=F32) / half)
    ar = row[:, None] * inv[None, :]
    ac = col[:, None] * inv[None, :]
    cos64 = jnp.concatenate([jnp.cos(ar), jnp.cos(ar), jnp.cos(ac), jnp.cos(ac)], axis=-1)
    sin64 = jnp.concatenate([-jnp.sin(ar), jnp.sin(ar), -jnp.sin(ac), jnp.sin(ac)], axis=-1)
    one = jnp.ones((n_lat, 64), F32)
    zero = jnp.zeros((n_lat, 64), F32)

    def with_ctx(c, s):
        c = jnp.concatenate([jnp.ones((SUB, 128), F32), c], axis=0)
        s = jnp.concatenate([jnp.zeros((SUB, 128), F32), s], axis=0)
        return c, s

    diff = with_ctx(jnp.concatenate([cos64, cos64], -1), jnp.concatenate([sin64, sin64], -1))
    mla = with_ctx(jnp.concatenate([cos64, one], -1), jnp.concatenate([sin64, zero], -1))
    return diff, mla


def _prep_w_in(w_in):
    depth, d, _ = w_in.shape
    z = lambda n: jnp.zeros((depth, d, n), w_in.dtype)
    o_diff = 4 * GROUP_W + 4 * HEADS
    o_mla = o_diff + 3 * GROUP_W
    o_na = o_mla + MLA_Q_RANK + MLA_KV_RANK + MLA_ROPE
    w = jnp.concatenate([w_in[..., :4 * GROUP_W], w_in[..., o_diff:o_mla], w_in[..., o_na:],
                         w_in[..., o_mla:o_na], z(MLA_W - (o_na - o_mla)),
                         w_in[..., 4 * GROUP_W:o_diff], z(128 - 4 * HEADS)], axis=-1)
    assert w.shape[-1] == N_PROJ
    return lax.optimization_barrier(w).astype(BF16)


def _prep_mla_weights(w_uq, w_ukv):
    depth = w_uq.shape[0]
    dq = MLA_NOPE + MLA_ROPE
    wq = w_uq.reshape(depth, MLA_Q_RANK, HEADS, dq)
    wq = jnp.concatenate([wq, jnp.zeros((depth, MLA_Q_RANK, HEADS, 256 - dq), w_uq.dtype)], axis=-1)
    wq = wq.reshape(depth, MLA_Q_RANK, HEADS * 256)
    wkv = w_ukv.reshape(depth, MLA_KV_RANK, HEADS, MLA_NOPE + HEAD_DIM)
    wkv = jnp.concatenate([wkv[..., :MLA_NOPE].reshape(depth, MLA_KV_RANK, GROUP_W),
                           wkv[..., MLA_NOPE:].reshape(depth, MLA_KV_RANK, GROUP_W)], axis=-1)
    return wq.astype(BF16), wkv.astype(BF16)


def _pad_lanes(v, n=128):
    return jnp.concatenate([v, jnp.zeros(v.shape[:-1] + (n - v.shape[-1],), v.dtype)], axis=-1)


def _mixers(pq, pg, lw, l, tabs, *, nb, rb):
    nblk_b = rb // SUB
    (dcos, dsin), (mcos, msin) = tabs
    gx, gg = _gdn_prep(pq, pg, lw["gdn_conv"], lw["gdn_a"], lw["gdn_dtb"], nblk_b=nblk_b)
    o_f, o_b = _gdn(gx, gg, nb=nb, rb=rb)
    ya = (o_f, o_b, pq, lw["gdn_norm_g"])
    qk_d, vt_d = _diff_prep(pq, dcos, dsin, nblk_b=nblk_b)
    yb = _flash(qk_d, qk_d, vt_d, nb=nb, rb=rb, dq=HEAD_DIM, qcol=0, kcol=GROUP_W, nsub=2,
                lam=lw["diff_lam"], norm_g=lw["diff_norm_g"])
    qm, km, vm = _mla_prep(pq, lw["mla_gq"], lw["mla_gkv"], lw["mla_wq"], lw["mla_wkv"], mcos, msin,
                           nblk_b=nblk_b)
    ym = _flash(qm, km, vm, nb=nb, rb=rb, dq=256, qcol=0, kcol=0)
    yn = _na(pq, lw["na_bias"], l, nb=nb, rb=rb)
    return ya, yb, ym, yn


def _layer(xu, lw, l, tabs, *, nb, rb, alpha):
    nblk_b = rb // SUB
    mod = lw["mod"]
    pq, pg = _inproj(xu, mod, lw["w_in"], l, nblk_b=nblk_b, nb=nb)
    ya, yb, ym, yn = _mixers(pq, pg, lw, l, tabs, nb=nb, rb=rb)
    x1 = _outproj(ya, (yb, ym, yn), lw["w_out"], xu, mod, lw["ln_g0"], lw["ln_b0"], l, nblk_b=nblk_b, nb=nb, alpha=alpha)
    return _ffn(x1, mod, lw["ffn_w_up"], lw["ffn_conv"], lw["ffn_w_down"], lw["ln_g1"], lw["ln_b1"], l,
                nblk_b=nblk_b, nb=nb, alpha=alpha)


def _layer_weights(l, mod, w_in_p, gdn_conv, gdn_a_log, gdn_dt_bias, gdn_norm_g, diff_lambda, diff_norm_g,
                   mla_q_norm_g, mla_kv_norm_g, wq, wkv, na_bias, w_out_b, ln_g, ln_b, w_up_b, ffn_conv,
                   w_down_b):
    lf = diff_lambda[l].astype(F32)
    lam_init = 0.8 - 0.6 * math.exp(-0.3 * l)
    lam = jnp.exp(jnp.sum(lf[0] * lf[1])) - jnp.exp(jnp.sum(lf[2] * lf[3])) + lam_init
    return {
        "mod": mod,
        "w_in": w_in_p,
        "gdn_conv": gdn_conv[l],
        "gdn_a": _pad_lanes(jnp.exp(gdn_a_log[l].astype(F32)).reshape(1, 2 * HEADS)),
        "gdn_dtb": _pad_lanes(gdn_dt_bias[l].astype(F32).reshape(1, 2 * HEADS)),
        "gdn_norm_g": gdn_norm_g[l].reshape(1, HEAD_DIM),
        "diff_lam": jnp.stack([lam, jnp.asarray(1.0 - lam_init, F32)]),
        "diff_norm_g": diff_norm_g[l].reshape(1, HEAD_DIM),
        "mla_gq": mla_q_norm_g[l].reshape(1, MLA_Q_RANK),
        "mla_gkv": mla_kv_norm_g[l].reshape(1, MLA_KV_RANK),
        "mla_wq": wq[l],
        "mla_wkv": wkv[l],
        "na_bias": na_bias,
        "w_out": w_out_b,
        "ln_g0": ln_g[l, 0:1], "ln_b0": ln_b[l, 0:1], "ln_g1": ln_g[l, 1:2], "ln_b1": ln_b[l, 1:2],
        "ffn_w_up": w_up_b,
        "ffn_conv": ffn_conv,
        "ffn_w_down": w_down_b,
    }


def kernel(x, c, ctx, c_ctx, w_mod, b_mod, w_in, gdn_conv, gdn_a_log, gdn_dt_bias, gdn_norm_g, diff_lambda,
           diff_norm_g, mla_q_norm_g, mla_kv_norm_g, mla_w_uq, mla_w_ukv, na_rpb, w_out, ln_g, ln_b,
           ffn_w_up, ffn_conv, ffn_w_down):
    nb, n_lat, d = x.shape
    depth = w_mod.shape[0]
    assert ctx.shape[1] == SUB and n_lat % (2 * SUB) == 0 and n_lat % GRID_W == 0
    assert (nb * (SUB + n_lat)) % (2 * SUB) == 0 and nb + 1 <= MOD_ROWS
    rb = SUB + n_lat
    alpha = (2 * depth) ** 0.25

    cvec = jnp.concatenate([c, c_ctx[None, :], jnp.zeros((MOD_ROWS - nb - 1, d), c.dtype)], axis=0)
    mod = _modulation(cvec, w_mod, b_mod)
    tabs = _rope_tables(n_lat)
    w_in_p = _prep_w_in(w_in)
    wq, wkv = _prep_mla_weights(mla_w_uq, mla_w_ukv)
    na_bias = _na_bias_table(na_rpb, n_lat // GRID_W)
    w_out_b = w_out.astype(BF16)
    w_up_b = ffn_w_up.astype(BF16)
    w_down_b = ffn_w_down.astype(BF16)

    xu = jnp.concatenate([ctx, x], axis=1).reshape(nb * rb, d)
    for l in range(depth):
        lw = _layer_weights(l, mod, w_in_p, gdn_conv, gdn_a_log, gdn_dt_bias, gdn_norm_g, diff_lambda,
                            diff_norm_g, mla_q_norm_g, mla_kv_norm_g, wq, wkv, na_bias, w_out_b, ln_g, ln_b,
                            w_up_b, ffn_conv, w_down_b)
        xu = _layer(xu, lw, l, tabs, nb=nb, rb=rb, alpha=alpha)
    return xu.reshape(nb, rb, d)[:, SUB:, :]
```

```python
import functools
import math

import numpy as np
import jax
import jax.numpy as jnp
from jax import lax
from jax.experimental import pallas as pl
from jax.experimental.pallas import tpu as pltpu

F32 = jnp.float32
BF16 = jnp.bfloat16

HEADS = 4
HEAD_DIM = 128
GROUP_W = HEADS * HEAD_DIM
GRID_W = 64
NA_KR = 8
NA_KC = 16
NA_BLOCK_ROWS = 4
NA_UNION_ROWS = NA_BLOCK_ROWS + NA_KR - 1
MLA_Q_RANK = 384
MLA_KV_RANK = 128
MLA_NOPE = 128
MLA_ROPE = 64
ROPE_THETA = 10000.0
LN_EPS = 1e-5
RMS_EPS = 1e-6
NEG = -1e30
LOG2E = math.log2(math.e)

SUB = 256
VT_ROWS = HEAD_DIM + 16
GDN_CHUNK = 128
MOD_ROWS = 16
VMEM_LIMIT = 56 * 1024 * 1024

C_GQ, C_GK, C_GV, C_GZ = 0, 512, 1024, 1536
C_DQ, C_DK, C_DV = 2048, 2560, 3072
C_NQ, C_NK, C_NV = 3584, 4096, 4608
C_MLA = 5120
MLA_W = 640
C_AB = 5760
N_PROJ = 5888


def _cparams(sem):
    return pltpu.CompilerParams(dimension_semantics=sem, vmem_limit_bytes=VMEM_LIMIT)


def _sigmoid(x):
    return 1.0 / (1.0 + jnp.exp(-x))


def _dot(a, b):
    return jnp.dot(a, b, preferred_element_type=F32)


def _dot_nt(a, b):
    return lax.dot_general(a, b, (((1,), (1,)), ((), ())), preferred_element_type=F32)


def _group_of(blk, nblk_b, nb):
    t = blk % nblk_b
    return jnp.where(t == 0, nb, blk // nblk_b), t


def _mod_kernel(c_ref, w_ref, b_ref, o_ref):
    c = c_ref[...]
    s = (c * _sigmoid(c)).astype(BF16)
    o_ref[0] = _dot(s, w_ref[0].astype(BF16)) + b_ref[0]


def _modulation(cvec, w_mod, b_mod):
    depth, d, n = w_mod.shape
    tn = 1536
    return pl.pallas_call(
        _mod_kernel,
        out_shape=jax.ShapeDtypeStruct((depth, MOD_ROWS, n), F32),
        grid=(depth, n // tn),
        in_specs=[
            pl.BlockSpec((MOD_ROWS, d), lambda l, j: (0, 0)),
            pl.BlockSpec((1, d, tn), lambda l, j: (l, 0, j)),
            pl.BlockSpec((1, 1, tn), lambda l, j: (l, 0, j)),
        ],
        out_specs=pl.BlockSpec((1, MOD_ROWS, tn), lambda l, j: (l, 0, j)),
        compiler_params=_cparams(("parallel", "parallel")),
        name="modulation",
    )(cvec, w_mod, b_mod.reshape(depth, 1, n))


def _inproj_kernel(x_ref, mod_ref, w_ref, o_ref, og_ref, h_scr, *, nsub, nblk_b, nb, d, nj):
    i = pl.program_id(0)
    j = pl.program_id(1)

    @pl.when(j == 0)
    def _():
        for s in range(nsub):
            g, _ = _group_of(i * nsub + s, nblk_b, nb)
            shift = mod_ref[0, pl.ds(g, 1), 0:d]
            scale = mod_ref[0, pl.ds(g, 1), d:2 * d]
            xs = x_ref[s * SUB:(s + 1) * SUB, :]
            h_scr[s * SUB:(s + 1) * SUB, :] = (xs * (1.0 + scale) + shift).astype(BF16)

    acc = _dot(h_scr[...], w_ref[...])
    o_ref[...] = acc.astype(BF16)

    @pl.when(j == nj - 1)
    def _():
        og_ref[...] = acc[:, acc.shape[1] - 128:]


def _inproj(xu, mod, w_in_p, l, *, nblk_b, nb):
    rows, d = xu.shape
    n = w_in_p.shape[2]
    tm, tn = 2 * SUB, n // 2
    nj = n // tn
    kern = functools.partial(_inproj_kernel, nsub=tm // SUB, nblk_b=nblk_b, nb=nb, d=d, nj=nj)
    return pl.pallas_call(
        kern,
        out_shape=(jax.ShapeDtypeStruct((rows, n), BF16), jax.ShapeDtypeStruct((rows, 128), F32)),
        grid=(rows // tm, nj),
        in_specs=[
            pl.BlockSpec((tm, d), lambda i, j: (i, 0)),
            pl.BlockSpec((1, MOD_ROWS, 6 * d), lambda i, j: (l, 0, 0)),
            pl.BlockSpec((None, d, tn), lambda i, j: (l, 0, j)),
        ],
        out_specs=(pl.BlockSpec((tm, tn), lambda i, j: (i, j)),
                   pl.BlockSpec((tm, 128), lambda i, j: (i, 0))),
        scratch_shapes=[pltpu.VMEM((tm, d), BF16)],
        compiler_params=_cparams(("parallel", "arbitrary")),
        name="inproj",
    )(xu, mod, w_in_p)


def _rope_slab(x, cos, sin):
    lane = lax.broadcasted_iota(jnp.int32, x.shape, 1)
    odd = (lane & 16) != 0
    partner = jnp.where(odd, pltpu.roll(x, 16, 1), pltpu.roll(x, 128 - 16, 1))
    return x * cos + partner * sin


def _store_vt(vt_ref, v):
    vt = v.T.astype(BF16)
    rows = v.shape[0]
    ext = jnp.where(lax.broadcasted_iota(jnp.int32, (VT_ROWS - HEAD_DIM, rows), 0) == 0, 1.0, 0.0).astype(BF16)
    for h in range(HEADS):
        vt_ref[0, h * VT_ROWS:h * VT_ROWS + HEAD_DIM, :] = vt[h * HEAD_DIM:(h + 1) * HEAD_DIM, :]
        vt_ref[0, h * VT_ROWS + HEAD_DIM:(h + 1) * VT_ROWS, :] = ext


def _diff_prep_kernel(p_ref, v_ref, cos_ref, sin_ref, o_ref, vt_ref, *, scale):
    _store_vt(vt_ref, v_ref[...].astype(F32))
    cos = cos_ref[...]
    sin = sin_ref[...]
    for s in range(2 * HEADS):
        x = p_ref[:, s * 128:(s + 1) * 128].astype(F32)
        r = _rope_slab(x, cos, sin)
        if s < HEADS:
            r = r * scale
        o_ref[:, s * 128:(s + 1) * 128] = r.astype(BF16)


def _diff_prep(pq, cos, sin, *, nblk_b):
    rows = pq.shape[0]
    tm = SUB
    kern = functools.partial(_diff_prep_kernel, scale=(HEAD_DIM // 2) ** -0.5 * LOG2E)
    return pl.pallas_call(
        kern,
        out_shape=(jax.ShapeDtypeStruct((rows, 2 * GROUP_W), BF16),
                   jax.ShapeDtypeStruct((rows // tm, HEADS * VT_ROWS, tm), BF16)),
        grid=(rows // tm,),
        in_specs=[
            pl.BlockSpec((tm, 2 * GROUP_W), lambda i: (i, C_DQ // (2 * GROUP_W))),
            pl.BlockSpec((tm, GROUP_W), lambda i: (i, C_DV // GROUP_W)),
            pl.BlockSpec((tm, 128), lambda i: (i % nblk_b, 0)),
            pl.BlockSpec((tm, 128), lambda i: (i % nblk_b, 0)),
        ],
        out_specs=(pl.BlockSpec((tm, 2 * GROUP_W), lambda i: (i, 0)),
                   pl.BlockSpec((1, HEADS * VT_ROWS, tm), lambda i: (i, 0, 0))),
        compiler_params=_cparams(("parallel",)),
        name="diff_prep",
    )(pq, pq, cos, sin)


def _mla_prep_kernel(p_ref, gq_ref, gkv_ref, wq_ref, wkv_ref, cos_ref, sin_ref,
                     q_ref, k_ref, v_ref, *, scale):
    cos = cos_ref[...]
    sin = sin_ref[...]
    cq = p_ref[:, 0:MLA_Q_RANK].astype(F32)
    cq = cq * lax.rsqrt(jnp.mean(cq * cq, axis=-1, keepdims=True) + RMS_EPS) * gq_ref[...]
    q = _dot(cq.astype(BF16), wq_ref[...])
    ckv = p_ref[:, MLA_Q_RANK:MLA_Q_RANK + MLA_KV_RANK].astype(F32)
    ckv = ckv * lax.rsqrt(jnp.mean(ckv * ckv, axis=-1, keepdims=True) + RMS_EPS) * gkv_ref[...]
    kv = _dot(ckv.astype(BF16), wkv_ref[...])
    kr = _rope_slab(p_ref[:, MLA_Q_RANK + MLA_KV_RANK:MLA_W].astype(F32), cos, sin).astype(BF16)
    for h in range(HEADS):
        q_ref[:, h * 256:h * 256 + 128] = (q[:, h * 256:h * 256 + 128] * scale).astype(BF16)
        qr = _rope_slab(q[:, h * 256 + 128:(h + 1) * 256], cos, sin)
        q_ref[:, h * 256 + 128:(h + 1) * 256] = (qr * scale).astype(BF16)
        k_ref[:, h * 256:h * 256 + 128] = kv[:, h * 128:(h + 1) * 128].astype(BF16)
        k_ref[:, h * 256 + 128:(h + 1) * 256] = kr
    _store_vt(v_ref, kv[:, GROUP_W:])


def _mla_prep(pq, gq, gkv, wq, wkv, cos, sin, *, nblk_b):
    rows = pq.shape[0]
    tm = SUB
    kern = functools.partial(_mla_prep_kernel, scale=(MLA_NOPE + MLA_ROPE) ** -0.5 * LOG2E)
    return pl.pallas_call(
        kern,
        out_shape=(jax.ShapeDtypeStruct((rows, HEADS * 256), BF16),
                   jax.ShapeDtypeStruct((rows, HEADS * 256), BF16),
                   jax.ShapeDtypeStruct((rows // tm, HEADS * VT_ROWS, tm), BF16)),
        grid=(rows // tm,),
        in_specs=[
            pl.BlockSpec((tm, MLA_W), lambda i: (i, C_MLA // MLA_W)),
            pl.BlockSpec((1, MLA_Q_RANK), lambda i: (0, 0)),
            pl.BlockSpec((1, MLA_KV_RANK), lambda i: (0, 0)),
            pl.BlockSpec((MLA_Q_RANK, HEADS * 256), lambda i: (0, 0)),
            pl.BlockSpec((MLA_KV_RANK, 2 * GROUP_W), lambda i: (0, 0)),
            pl.BlockSpec((tm, 128), lambda i: (i % nblk_b, 0)),
            pl.BlockSpec((tm, 128), lambda i: (i % nblk_b, 0)),
        ],
        out_specs=(pl.BlockSpec((tm, HEADS * 256), lambda i: (i, 0)),
                   pl.BlockSpec((tm, HEADS * 256), lambda i: (i, 0)),
                   pl.BlockSpec((1, HEADS * VT_ROWS, tm), lambda i: (i, 0, 0))),
        compiler_params=_cparams(("parallel",)),
        name="mla_prep",
    )(pq, gq, gkv, wq, wkv, cos, sin)


def _flash_kernel(*refs, nsub, hp, dq, n_lat_pairs):
    if nsub == 2:
        lam_ref, g_ref, q_ref, k_ref, vt_ref, o_ref, s_scr, p_scr, acc_scr = refs
    else:
        q_ref, k_ref, vt_ref, o_ref, s_scr, p_scr, acc_scr = refs
    qt = pl.program_id(2)
    tq = q_ref.shape[0]
    tk = 2 * SUB
    qs = []
    for hh in range(hp):
        q = q_ref[:, hh * dq:(hh + 1) * dq]
        if nsub == 2:
            lane = lax.broadcasted_iota(jnp.int32, q.shape, 1)
            zero = jnp.zeros_like(q)
            qs.append((hh, jnp.where(lane < 64, q, zero)))
            qs.append((hh, jnp.where(lane >= 64, q, zero)))
        else:
            qs.append((hh, q))
    nch = len(qs)

    def scores_into(slot, rows, nrows):
        maxes = []
        for c, (hh, qi) in enumerate(qs):
            s = _dot_nt(k_ref[rows, hh * dq:(hh + 1) * dq], qi)
            s_scr[slot, c, 0:nrows, :] = s
            maxes.append(jnp.max(s, axis=0, keepdims=True))
        return maxes

    def pv_from(slot, vblocks):
        pvs = []
        for c, (hh, _) in enumerate(qs):
            pv = None
            for i, blk in enumerate(vblocks):
                part = _dot(vt_ref[blk, hh * VT_ROWS:(hh + 1) * VT_ROWS, :],
                            p_scr[slot, c, i * SUB:(i + 1) * SUB, :])
                pv = part if pv is None else pv + part
            pvs.append(pv)
        return pvs

    def softmax(slot, nrows, stats, maxes, pvs=None):
        ch = 64
        new = []
        for c, (m, mb) in enumerate(zip(stats, maxes)):
            m_new = jnp.maximum(m, mb)
            for r in range(0, nrows, ch):
                p_scr[slot, c, r:r + ch, :] = jnp.exp2(s_scr[slot, c, r:r + ch, :] - m_new).astype(BF16)
            acc_scr[c] = (acc_scr[c] if pvs is None else acc_scr[c] + pvs[c]) * jnp.exp2(m - m_new)
            new.append(m_new)
        return new

    def lat_rows(j):
        return pl.ds(pl.multiple_of(SUB + j * tk, SUB), tk)

    acc_scr[...] = jnp.zeros_like(acc_scr)
    stats = [jnp.full((1, tq), NEG, F32) for _ in qs]
    mx_ctx = scores_into(1, slice(0, SUB), SUB)
    mx0 = scores_into(0, lat_rows(0), tk)
    p_scr[1, :, SUB:, :] = jnp.zeros((nch, SUB, tq), BF16)
    stats = softmax(1, SUB, stats, mx_ctx)
    unroll = 4 if n_lat_pairs % 4 == 0 else (2 if n_lat_pairs % 2 == 0 else 1)
    n = jnp.where(qt == 0, 0, n_lat_pairs // unroll)
    last = 2 * n_lat_pairs - 1

    def pair(a, stats, mx0):
        bp = jnp.where(a == 0, 0, 2 * a - 1)
        pvs = pv_from(1, (bp, bp + 1))
        mx1 = scores_into(1, lat_rows(a + 1), tk)
        stats = softmax(0, tk, stats, mx0, pvs)
        pvs = pv_from(0, (1 + 2 * a, 2 + 2 * a))
        mx0 = scores_into(0, lat_rows(jnp.minimum(a + 2, last)), tk)
        return softmax(1, tk, stats, mx1, pvs), mx0

    def body(jj, carry):
        stats, mx0 = list(carry[:nch]), list(carry[nch:])
        for u in range(unroll):
            stats, mx0 = pair(2 * (jj * unroll + u), stats, mx0)
        return tuple(stats) + tuple(mx0)

    lax.fori_loop(0, n, body, tuple(stats) + tuple(mx0))
    done = 2 * unroll * n
    bl = jnp.where(done == 0, 0, 2 * done - 1)
    for c, pv in enumerate(pv_from(1, (bl, bl + 1))):
        acc_scr[c] += pv

    for hh in range(hp):
        def normalised(c):
            return acc_scr[c, 0:HEAD_DIM, :] / acc_scr[c, HEAD_DIM:HEAD_DIM + 1, :]

        if nsub == 2:
            o = (normalised(2 * hh) - lam_ref[0] * normalised(2 * hh + 1)).T
            o = o * lax.rsqrt(jnp.mean(o * o, axis=-1, keepdims=True) + RMS_EPS) * g_ref[...] * lam_ref[1]
        else:
            o = normalised(hh).T
        o_ref[:, hh * HEAD_DIM:(hh + 1) * HEAD_DIM] = o.astype(o_ref.dtype)


def _flash(q_arr, k_arr, vt_arr, *, nb, rb, dq, qcol, kcol, nsub=1, lam=None, norm_g=None):
    rows = q_arr.shape[0]
    nblk_b = rb // SUB
    lat = rb - SUB
    hp = 2
    assert lat % (2 * SUB) == 0 and qcol % (hp * dq) == 0 and kcol % (hp * dq) == 0
    assert lat % (4 * SUB) == 0
    nch = hp * nsub
    kern = functools.partial(_flash_kernel, nsub=nsub, hp=hp, dq=dq, n_lat_pairs=lat // (4 * SUB))
    in_specs = [
        pl.BlockSpec((SUB, hp * dq), lambda b, h, t: (b * nblk_b + t, qcol // (hp * dq) + h)),
        pl.BlockSpec((rb, hp * dq), lambda b, h, t: (b, kcol // (hp * dq) + h)),
        pl.BlockSpec((nblk_b, hp * VT_ROWS, SUB), lambda b, h, t: (b, h, 0)),
    ]
    args = [q_arr, k_arr, vt_arr]
    if nsub == 2:
        in_specs = [pl.BlockSpec(memory_space=pltpu.SMEM),
                    pl.BlockSpec((1, HEAD_DIM), lambda b, h, t: (0, 0))] + in_specs
        args = [lam, norm_g] + args
    return pl.pallas_call(
        kern,
        out_shape=jax.ShapeDtypeStruct((rows, GROUP_W), BF16),
        grid=(nb, HEADS // hp, nblk_b),
        in_specs=in_specs,
        out_specs=pl.BlockSpec((SUB, hp * HEAD_DIM), lambda b, h, t: (b * nblk_b + t, h)),
        scratch_shapes=[pltpu.VMEM((2, nch, 2 * SUB, SUB), F32), pltpu.VMEM((2, nch, 2 * SUB, SUB), BF16),
                        pltpu.VMEM((nch, VT_ROWS, SUB), F32)],
        compiler_params=_cparams(("parallel", "parallel", "arbitrary")),
        name="flash_diff" if nsub == 2 else "flash_mla",
    )(*args)


def _na_kernel(q_ref, k_ref, v_ref, bias_ref, o_ref, *, grid_rows, scale, hp):
    qt = pl.program_id(2)
    qb = jnp.maximum(qt - 1, 0)
    u0 = jnp.clip(qb * NA_BLOCK_ROWS - NA_KR // 2, 0, grid_rows - NA_UNION_ROWS)
    start = pl.multiple_of(SUB + u0 * GRID_W, GRID_W)
    nn = NA_UNION_ROWS * GRID_W
    hs = [slice(h * HEAD_DIM, (h + 1) * HEAD_DIM) for h in range(hp)]
    qs = [(q_ref[:, sl].astype(F32) * scale).astype(BF16) for sl in hs]
    s_cs = [_dot_nt(q, k_ref[0:SUB, sl]) for q, sl in zip(qs, hs)]
    s_ns = [_dot_nt(q, k_ref[pl.ds(start, nn), sl]) + bias_ref[h, 0] for h, (q, sl) in enumerate(zip(qs, hs))]
    ms = [jnp.maximum(jnp.max(s_c, axis=-1, keepdims=True), jnp.max(s_n, axis=-1, keepdims=True))
          for s_c, s_n in zip(s_cs, s_ns)]
    e_cs = [jnp.exp2(s_c - m) for s_c, m in zip(s_cs, ms)]
    e_ns = [jnp.exp2(s_n - m) for s_n, m in zip(s_ns, ms)]
    for sl, e_c, e_n in zip(hs, e_cs, e_ns):
        l = jnp.sum(e_c, axis=-1, keepdims=True) + jnp.sum(e_n, axis=-1, keepdims=True)
        o = (_dot(e_c.astype(BF16), v_ref[0:SUB, sl]) + _dot(e_n.astype(BF16), v_ref[pl.ds(start, nn), sl])) / l
        o_ref[:, sl] = o.astype(o_ref.dtype)


def _na_bias_table(rpb, grid_rows):
    assert grid_rows >= NA_UNION_ROWS + NA_BLOCK_ROWS
    j = np.arange(NA_BLOCK_ROWS)[:, None]
    iu = np.arange(NA_UNION_ROWS)[None, :]
    cq = np.arange(GRID_W)[:, None]
    ck = np.arange(GRID_W)[None, :]
    cs = np.clip(cq - NA_KC // 2, 0, GRID_W - NA_KC)
    col_ok = (ck >= cs) & (ck < cs + NA_KC)
    dx = np.clip(ck - cq + NA_KC - 1, 0, 2 * NA_KC - 2)
    ex = np.eye(2 * NA_KC - 1, dtype=np.float32)[dx]
    eys, oks = [], []
    for r_off, w_off in ((j, 0 * j), (j + NA_KR // 2, j),
                         (j + NA_UNION_ROWS - NA_BLOCK_ROWS, 0 * j + NA_UNION_ROWS - NA_KR)):
        dy = np.clip(iu - r_off + NA_KR - 1, 0, 2 * NA_KR - 2)
        eys.append(np.eye(2 * NA_KR - 1, dtype=np.float32)[dy])
        oks.append((iu - w_off >= 0) & (iu - w_off < NA_KR))
    ey = np.stack(eys)
    ok = np.stack(oks)[:, :, None, :, None] & col_ok[None, None, :, None, :]
    vals = jnp.einsum("tjia,lhab,qkb->lhtjqik", ey, rpb.astype(F32), ex, precision=lax.Precision.HIGHEST)
    nb_part = jnp.where(ok[None, None], vals * LOG2E, NEG)
    nb_part = jnp.concatenate([nb_part, jnp.full_like(nb_part[:, :, :1], NEG)], axis=2)
    return nb_part.reshape(rpb.shape[:2] + (4, NA_BLOCK_ROWS * GRID_W, NA_UNION_ROWS * GRID_W))


def _na(pq, bias, l, *, nb, rb):
    rows = pq.shape[0]
    nblk_b = rb // SUB
    grid_rows = (rb - SUB) // GRID_W
    nqb = nblk_b - 1
    hp = 2
    hw = hp * HEAD_DIM
    kern = functools.partial(_na_kernel, grid_rows=grid_rows, scale=HEAD_DIM ** -0.5 * LOG2E, hp=hp)

    def bias_map(b, h, t):
        ty = jnp.where(t == 0, 3, jnp.where(t == 1, 0, jnp.where(t == nqb, 2, 1)))
        return (l, h, ty, 0, 0)

    return pl.pallas_call(
        kern,
        out_shape=jax.ShapeDtypeStruct((rows, GROUP_W), BF16),
        grid=(nb, HEADS // hp, nblk_b),
        in_specs=[
            pl.BlockSpec((SUB, hw), lambda b, h, t: (b * nblk_b + t, C_NQ // hw + h)),
            pl.BlockSpec((rb, hw), lambda b, h, t: (b, C_NK // hw + h)),
            pl.BlockSpec((rb, hw), lambda b, h, t: (b, C_NV // hw + h)),
            pl.BlockSpec((None, hp, 1, SUB, bias.shape[-1]), bias_map),
        ],
        out_specs=pl.BlockSpec((SUB, hw), lambda b, h, t: (b * nblk_b + t, h)),
        compiler_params=_cparams(("parallel", "parallel", "arbitrary")),
        name="na",
    )(pq, pq, pq, bias)


def _gdn_prep_kernel(x_ref, prev_ref, next_ref, w_ref, g_ref, av_ref, dtb_ref, o_ref, og_ref, *, nblk_b):
    i = pl.program_id(0)
    t = i % nblk_b
    keep_prev = jnp.where((t == 0) | (t == 1), 0.0, 1.0)
    keep_next = jnp.where((t == 0) | (t == nblk_b - 1), 0.0, 1.0)
    x = x_ref[...].astype(F32)
    tm = x.shape[0]
    row = lax.broadcasted_iota(jnp.int32, x.shape, 0)
    halo_p = prev_ref[15:16, :].astype(F32) * keep_prev
    halo_n = next_ref[0:1, :].astype(F32) * keep_next
    xp = jnp.where(row == 0, halo_p, pltpu.roll(x, 1, 0))
    xn = jnp.where(row == tm - 1, halo_n, pltpu.roll(x, tm - 1, 0))
    c = xp * w_ref[0:1, :] + x * w_ref[1:2, :] + xn * w_ref[2:3, :]
    s = c * _sigmoid(c)
    for h in range(2 * HEADS):
        sh = s[:, h * 128:(h + 1) * 128]
        nrm = lax.rsqrt(jnp.sum(sh * sh, axis=-1, keepdims=True) + RMS_EPS)
        if h < HEADS:
            nrm = nrm * HEAD_DIM ** -0.5
        o_ref[:, h * 128:(h + 1) * 128] = (sh * nrm).astype(BF16)
    o_ref[:, 2 * GROUP_W:] = s[:, 2 * GROUP_W:].astype(BF16)
    g = g_ref[...]
    z = g + dtb_ref[...]
    softplus = jnp.maximum(z, 0.0) + jnp.log(1.0 + jnp.exp(-jnp.abs(z)))
    lane = lax.broadcasted_iota(jnp.int32, g.shape, 1)
    og_ref[...] = jnp.where(lane < 2 * HEADS, -av_ref[...] * softplus, _sigmoid(g))


def _gdn_prep(pq, pg, conv_w, a_vec, dtb_vec, *, nblk_b):
    rows = pq.shape[0]
    tm = SUB
    w3 = 3 * GROUP_W
    nhalo = rows // 16
    kern = functools.partial(_gdn_prep_kernel, nblk_b=nblk_b)
    return pl.pallas_call(
        kern,
        out_shape=(jax.ShapeDtypeStruct((rows, w3), BF16), jax.ShapeDtypeStruct((rows, 128), F32)),
        grid=(rows // tm,),
        in_specs=[
            pl.BlockSpec((tm, w3), lambda i: (i, 0)),
            pl.BlockSpec((16, w3), lambda i: (jnp.maximum(i * (tm // 16) - 1, 0), 0)),
            pl.BlockSpec((16, w3), lambda i: (jnp.minimum((i + 1) * (tm // 16), nhalo - 1), 0)),
            pl.BlockSpec((3, w3), lambda i: (0, 0)),
            pl.BlockSpec((tm, 128), lambda i: (i, 0)),
            pl.BlockSpec((1, 128), lambda i: (0, 0)),
            pl.BlockSpec((1, 128), lambda i: (0, 0)),
        ],
        out_specs=(pl.BlockSpec((tm, w3), lambda i: (i, 0)), pl.BlockSpec((tm, 128), lambda i: (i, 0))),
        compiler_params=_cparams(("parallel",)),
        name="gdn_prep",
    )(pq, pq, pq, conv_w, pg, a_vec, dtb_vec)


def _split3(x):
    hi = x.astype(BF16)
    r = x - hi.astype(F32)
    mid = r.astype(BF16)
    lo = (r - mid.astype(F32)).astype(BF16)
    return hi, mid, lo


def _gdn_chains(chains):
    c = chains[0][0].shape[0]
    row = lax.broadcasted_iota(jnp.int32, (c, c), 0)
    col = lax.broadcasted_iota(jnp.int32, (c, c), 1)
    xor = row ^ col
    eye = jnp.where(row == col, 1.0, 0.0)
    a_mats, qks, kbs, kfs, egs = [], [], [], [], []
    for q, k, v, gcol, grow, beta, glast, s_prev, lower in chains:
        incl = (row >= col) if lower else (row <= col)
        strict = (row > col) if lower else (row < col)
        decay = jnp.exp(jnp.where(incl, gcol - grow, NEG))
        kf = k.astype(F32)
        kb = kf * beta
        a_mats.append(jnp.where(strict, _dot_nt(kb.astype(BF16), k) * decay, 0.0))
        qks.append(jnp.where(incl, _dot_nt(q, k) * decay, 0.0).astype(BF16))
        kbs.append(kb)
        kfs.append(kf)
        egs.append(jnp.exp(gcol))
    tinvs = [eye - jnp.where(xor < 2, a, 0.0) for a in a_mats]
    s = 2
    while s < c:
        ys = [_dot(jnp.where((xor >= s) & (xor < 2 * s), a, 0.0).astype(BF16), t.astype(BF16))
              for a, t in zip(a_mats, tinvs)]
        tinvs = [t - _dot(t.astype(BF16), y.astype(BF16)) for t, y in zip(tinvs, ys)]
        s *= 2
    wus = []
    for (q, k, v, gcol, grow, beta, glast, s_prev, lower), kb, eg, t in zip(chains, kbs, egs, tinvs):
        rhs = jnp.concatenate([kb * eg, v.astype(F32) * beta], axis=1).astype(BF16)
        wus.append(_dot(t.astype(BF16), rhs))
    sbs = [ch[7].astype(BF16) for ch in chains]
    vnbs = [(wu[:, HEAD_DIM:] - _dot(wu[:, :HEAD_DIM].astype(BF16), sb)).astype(BF16) for wu, sb in zip(wus, sbs)]
    outs = []
    for (q, k, v, gcol, grow, beta, glast, s_prev, lower), kf, eg, qk, sb, vnb in zip(chains, kfs, egs, qks, sbs, vnbs):
        o = _dot((q.astype(F32) * eg).astype(BF16), sb) + _dot(qk, vnb)
        kd_t = (kf * jnp.exp(glast - gcol)).T.astype(BF16)
        s_new = s_prev * jnp.exp(glast) + _dot(kd_t, vnb)
        outs.append((o, s_new))
    return outs


def _gdn_kernel(xf_ref, xb_ref, gf_ref, gb_ref, of_ref, ob_ref, s_scr):
    st = pl.program_id(1)

    @pl.when(st == 0)
    def _():
        s_scr[...] = jnp.zeros_like(s_scr)

    c = GDN_CHUNK
    row = lax.broadcasted_iota(jnp.int32, (c, c), 0)
    col = lax.broadcasted_iota(jnp.int32, (c, c), 1)
    chains = []
    for d, (x_ref, g_ref) in enumerate(((xf_ref, gf_ref), (xb_ref, gb_ref))):
        lower = d == 0
        gates = g_ref[...]
        tri = jnp.where((row >= col) if lower else (row <= col), 1.0, 0.0).astype(BF16)
        hi, mid, lo = _split3(gates)
        gsum = _dot(tri, hi) + _dot(tri, mid) + _dot(tri, lo)
        gsum_t = gsum.T
        for h in range(HEADS):
            gi = d * HEADS + h
            bi = 2 * HEADS + gi
            glast = gsum[c - 1:c, gi:gi + 1] if lower else gsum[0:1, gi:gi + 1]
            chains.append((x_ref[:, h * 128:(h + 1) * 128],
                           x_ref[:, GROUP_W + h * 128:GROUP_W + (h + 1) * 128],
                           x_ref[:, 2 * GROUP_W + h * 128:2 * GROUP_W + (h + 1) * 128],
                           gsum[:, gi:gi + 1], gsum_t[gi:gi + 1, :], gates[:, bi:bi + 1], glast,
                           s_scr[d, h], lower))
    outs = _gdn_chains(chains)
    for idx, (o, s_new) in enumerate(outs):
        d, h = divmod(idx, HEADS)
        s_scr[d, h] = s_new
        (of_ref, ob_ref)[d][:, h * 128:(h + 1) * 128] = o


def _gdn(gx, gg, *, nb, rb):
    rows = gx.shape[0]
    c = GDN_CHUNK
    nch = rb // c
    nctx = SUB // c
    w3 = 3 * GROUP_W

    def fmap(b, s):
        return (b * nch + s, 0)

    def bmap(b, s):
        return (b * nch + jnp.where(s < nctx, nctx - 1 - s, nch + nctx - 1 - s), 0)

    return pl.pallas_call(
        _gdn_kernel,
        out_shape=(jax.ShapeDtypeStruct((rows, GROUP_W), F32), jax.ShapeDtypeStruct((rows, GROUP_W), F32)),
        grid=(nb, nch),
        in_specs=[
            pl.BlockSpec((c, w3), fmap), pl.BlockSpec((c, w3), bmap),
            pl.BlockSpec((c, 128), fmap), pl.BlockSpec((c, 128), bmap),
        ],
        out_specs=(pl.BlockSpec((c, GROUP_W), fmap), pl.BlockSpec((c, GROUP_W), bmap)),
        scratch_shapes=[pltpu.VMEM((2, HEADS, HEAD_DIM, HEAD_DIM), F32)],
        compiler_params=_cparams(("parallel", "arbitrary")),
        name="gdn",
    )(gx, gx, gg, gg)


def _layernorm_rows(z, g, b):
    mu = jnp.mean(z, axis=-1, keepdims=True)
    zc = z - mu
    var = jnp.mean(zc * zc, axis=-1, keepdims=True)
    return zc * lax.rsqrt(var + LN_EPS) * g + b


def _outproj_kernel(of_ref, ob_ref, z_ref, gn_ref, yb_ref, ym_ref, yn_ref, w_ref, x_ref, mod_ref, g_ref, b_ref,
                    o_ref, *, nsub, nblk_b, nb, d, alpha):
    i = pl.program_id(0)
    z = z_ref[...].astype(F32)
    gate = z * _sigmoid(z)
    ya = []
    for h in range(HEADS):
        sl = slice(h * HEAD_DIM, (h + 1) * HEAD_DIM)
        o = of_ref[:, sl] + ob_ref[:, sl]
        o = o * lax.rsqrt(jnp.mean(o * o, axis=-1, keepdims=True) + RMS_EPS) * gn_ref[...]
        ya.append((o * gate[:, sl]).astype(BF16))
    y = _dot(jnp.concatenate(ya, axis=1), w_ref[0:GROUP_W, :])
    y = y + _dot(yb_ref[...], w_ref[GROUP_W:2 * GROUP_W, :])
    y = y + _dot(ym_ref[...], w_ref[2 * GROUP_W:3 * GROUP_W, :])
    y = y + _dot(yn_ref[...], w_ref[3 * GROUP_W:4 * GROUP_W, :])
    for s in range(nsub):
        g, _ = _group_of(i * nsub + s, nblk_b, nb)
        gate = mod_ref[0, pl.ds(g, 1), 2 * d:3 * d]
        sl = slice(s * SUB, (s + 1) * SUB)
        z = alpha * x_ref[sl, :] + gate * y[sl, :]
        o_ref[sl, :] = _layernorm_rows(z, g_ref[...], b_ref[...])


def _outproj(gdn, ys, w_out, xu, mod, ln_g, ln_b, l, *, nblk_b, nb, alpha):
    o_f, o_b, pq, norm_g = gdn
    rows, d = xu.shape
    tm = 2 * SUB
    kern = functools.partial(_outproj_kernel, nsub=tm // SUB, nblk_b=nblk_b, nb=nb, d=d, alpha=alpha)
    yspec = pl.BlockSpec((tm, GROUP_W), lambda i: (i, 0))
    return pl.pallas_call(
        kern,
        out_shape=jax.ShapeDtypeStruct((rows, d), F32),
        grid=(rows // tm,),
        in_specs=[yspec, yspec,
                  pl.BlockSpec((tm, GROUP_W), lambda i: (i, C_GZ // GROUP_W)),
                  pl.BlockSpec((1, HEAD_DIM), lambda i: (0, 0)),
                  yspec, yspec, yspec,
                  pl.BlockSpec((None, 4 * GROUP_W, d), lambda i: (l, 0, 0)),
                  pl.BlockSpec((tm, d), lambda i: (i, 0)),
                  pl.BlockSpec((1, MOD_ROWS, 6 * d), lambda i: (l, 0, 0)),
                  pl.BlockSpec((1, d), lambda i: (0, 0)),
                  pl.BlockSpec((1, d), lambda i: (0, 0))],
        out_specs=pl.BlockSpec((tm, d), lambda i: (i, 0)),
        compiler_params=_cparams(("parallel",)),
        name="outproj",
    )(o_f, o_b, pq, norm_g, *ys, w_out, xu, mod, ln_g, ln_b)


def _ffn_kernel(x_ref, prev_ref, next_ref, mod_ref, wg_ref, wv_ref, cg_ref, cv_ref, wd_ref, g_ref, b_ref,
                o_ref, h_scr, hb_scr, acc_scr, *, nsub, nblk_b, nb, d, nj, alpha):
    i = pl.program_id(0)
    j = pl.program_id(1)
    tm = nsub * SUB

    @pl.when(j == 0)
    def _():
        acc_scr[...] = jnp.zeros_like(acc_scr)
        for s in range(nsub):
            g, _ = _group_of(i * nsub + s, nblk_b, nb)
            shift = mod_ref[0, pl.ds(g, 1), 3 * d:4 * d]
            scale = mod_ref[0, pl.ds(g, 1), 4 * d:5 * d]
            h_scr[8 + s * SUB:8 + (s + 1) * SUB, :] = x_ref[s * SUB:(s + 1) * SUB, :] * (1.0 + scale) + shift
            if s == 0:
                h_scr[0:8, :] = prev_ref[...] * (1.0 + scale) + shift
            if s == nsub - 1:
                h_scr[tm + 8:tm + 16, :] = next_ref[...] * (1.0 + scale) + shift
        hb_scr[...] = h_scr[...].astype(BF16)

    row = lax.broadcasted_iota(jnp.int32, (tm, 1), 0)
    keep_prev = jnp.ones((tm, 1), F32)
    keep_next = jnp.ones((tm, 1), F32)
    for s in range(nsub):
        _, t = _group_of(i * nsub + s, nblk_b, nb)
        first = (t == 0) | (t == 1)
        last = (t == 0) | (t == nblk_b - 1)
        keep_prev = jnp.where((row == s * SUB) & first, 0.0, keep_prev)
        keep_next = jnp.where((row == s * SUB + SUB - 1) & last, 0.0, keep_next)

    hb = hb_scr[...]
    ext = tm + 16

    def conv(u, cw_ref):
        up = pltpu.roll(u, 1, 0)[8:tm + 8, :] * keep_prev
        un = pltpu.roll(u, ext - 1, 0)[8:tm + 8, :] * keep_next
        return up * cw_ref[0:1, :] + u[8:tm + 8, :] * cw_ref[1:2, :] + un * cw_ref[2:3, :]

    gate = conv(_dot(hb, wg_ref[...]), cg_ref)
    val = conv(_dot(hb, wv_ref[...]), cv_ref)
    act = (gate * _sigmoid(gate) * val).astype(BF16)
    acc_scr[...] += _dot(act, wd_ref[...])

    @pl.when(j == nj - 1)
    def _():
        for s in range(nsub):
            g, _ = _group_of(i * nsub + s, nblk_b, nb)
            gt = mod_ref[0, pl.ds(g, 1), 5 * d:6 * d]
            sl = slice(s * SUB, (s + 1) * SUB)
            z = alpha * x_ref[sl, :] + gt * acc_scr[sl, :]
            o_ref[sl, :] = _layernorm_rows(z, g_ref[...], b_ref[...])


def _ffn(xu, mod, w_up, conv_w, w_down, ln_g, ln_b, l, *, nblk_b, nb, alpha):
    rows, d = xu.shape
    dff = w_down.shape[1]
    tm, tc = 2 * SUB, 512
    nj = dff // tc
    nhalo = rows // 8
    kern = functools.partial(_ffn_kernel, nsub=tm // SUB, nblk_b=nblk_b, nb=nb, d=d, nj=nj, alpha=alpha)
    return pl.pallas_call(
        kern,
        out_shape=jax.ShapeDtypeStruct((rows, d), F32),
        grid=(rows // tm, nj),
        in_specs=[
            pl.BlockSpec((tm, d), lambda i, j: (i, 0)),
            pl.BlockSpec((8, d), lambda i, j: (jnp.maximum(i * (tm // 8) - 1, 0), 0)),
            pl.BlockSpec((8, d), lambda i, j: (jnp.minimum((i + 1) * (tm // 8), nhalo - 1), 0)),
            pl.BlockSpec((1, MOD_ROWS, 6 * d), lambda i, j: (l, 0, 0)),
            pl.BlockSpec((None, d, tc), lambda i, j: (l, 0, j)),
            pl.BlockSpec((None, d, tc), lambda i, j: (l, 0, nj + j)),
            pl.BlockSpec((None, 3, tc), lambda i, j: (l, 0, j)),
            pl.BlockSpec((None, 3, tc), lambda i, j: (l, 0, nj + j)),
            pl.BlockSpec((None, tc, d), lambda i, j: (l, j, 0)),
            pl.BlockSpec((1, d), lambda i, j: (0, 0)),
            pl.BlockSpec((1, d), lambda i, j: (0, 0)),
        ],
        out_specs=pl.BlockSpec((tm, d), lambda i, j: (i, 0)),
        scratch_shapes=[pltpu.VMEM((tm + 16, d), F32), pltpu.VMEM((tm + 16, d), BF16), pltpu.VMEM((tm, d), F32)],
        compiler_params=_cparams(("parallel", "arbitrary")),
        name="ffn",
    )(xu, xu, xu, mod, w_up, w_up, conv_w, conv_w, w_down, ln_g, ln_b)


def _rope_tables(n_lat):
    pos = jnp.arange(n_lat)
    row = (pos // GRID_W).astype(F32)
    col = (pos % GRID_W).astype(F32)
    half = MLA_ROPE // 2
    inv = ROPE_THETA ** (-jnp.arange(0, half, 2, dtype=F32) / half)
    ar = row[:, None] * inv[None, :]
    ac = col[:, None] * inv[None, :]
    cos64 = jnp.concatenate([jnp.cos(ar), jnp.cos(ar), jnp.cos(ac), jnp.cos(ac)], axis=-1)
    sin64 = jnp.concatenate([-jnp.sin(ar), jnp.sin(ar), -jnp.sin(ac), jnp.sin(ac)], axis=-1)
    one = jnp.ones((n_lat, 64), F32)
    zero = jnp.zeros((n_lat, 64), F32)

    def with_ctx(c, s):
        c = jnp.concatenate([jnp.ones((SUB, 128), F32), c], axis=0)
        s = jnp.concatenate([jnp.zeros((SUB, 128), F32), s], axis=0)
        return c, s

    diff = with_ctx(jnp.concatenate([cos64, cos64], -1), jnp.concatenate([sin64, sin64], -1))
    mla = with_ctx(jnp.concatenate([cos64, one], -1), jnp.concatenate([sin64, zero], -1))
    return diff, mla


def _prep_w_in(w_in):
    depth, d, _ = w_in.shape
    z = lambda n: jnp.zeros((depth, d, n), w_in.dtype)
    o_diff = 4 * GROUP_W + 4 * HEADS
    o_mla = o_diff + 3 * GROUP_W
    o_na = o_mla + MLA_Q_RANK + MLA_KV_RANK + MLA_ROPE
    w = jnp.concatenate([w_in[..., :4 * GROUP_W], w_in[..., o_diff:o_mla], w_in[..., o_na:],
                         w_in[..., o_mla:o_na], z(MLA_W - (o_na - o_mla)),
                         w_in[..., 4 * GROUP_W:o_diff], z(128 - 4 * HEADS)], axis=-1)
    assert w.shape[-1] == N_PROJ
    return lax.optimization_barrier(w).astype(BF16)


def _prep_mla_weights(w_uq, w_ukv):
    depth = w_uq.shape[0]
    dq = MLA_NOPE + MLA_ROPE
    wq = w_uq.reshape(depth, MLA_Q_RANK, HEADS, dq)
    wq = jnp.concatenate([wq, jnp.zeros((depth, MLA_Q_RANK, HEADS, 256 - dq), w_uq.dtype)], axis=-1)
    wq = wq.reshape(depth, MLA_Q_RANK, HEADS * 256)
    wkv = w_ukv.reshape(depth, MLA_KV_RANK, HEADS, MLA_NOPE + HEAD_DIM)
    wkv = jnp.concatenate([wkv[..., :MLA_NOPE].reshape(depth, MLA_KV_RANK, GROUP_W),
                           wkv[..., MLA_NOPE:].reshape(depth, MLA_KV_RANK, GROUP_W)], axis=-1)
    return wq.astype(BF16), wkv.astype(BF16)


def _pad_lanes(v, n=128):
    return jnp.concatenate([v, jnp.zeros(v.shape[:-1] + (n - v.shape[-1],), v.dtype)], axis=-1)


def _mixers(pq, pg, lw, l, tabs, *, nb, rb):
    nblk_b = rb // SUB
    (dcos, dsin), (mcos, msin) = tabs
    gx, gg = _gdn_prep(pq, pg, lw["gdn_conv"], lw["gdn_a"], lw["gdn_dtb"], nblk_b=nblk_b)
    o_f, o_b = _gdn(gx, gg, nb=nb, rb=rb)
    ya = (o_f, o_b, pq, lw["gdn_norm_g"])
    qk_d, vt_d = _diff_prep(pq, dcos, dsin, nblk_b=nblk_b)
    yb = _flash(qk_d, qk_d, vt_d, nb=nb, rb=rb, dq=HEAD_DIM, qcol=0, kcol=GROUP_W, nsub=2,
                lam=lw["diff_lam"], norm_g=lw["diff_norm_g"])
    qm, km, vm = _mla_prep(pq, lw["mla_gq"], lw["mla_gkv"], lw["mla_wq"], lw["mla_wkv"], mcos, msin,
                           nblk_b=nblk_b)
    ym = _flash(qm, km, vm, nb=nb, rb=rb, dq=256, qcol=0, kcol=0)
    yn = _na(pq, lw["na_bias"], l, nb=nb, rb=rb)
    return ya, yb, ym, yn


def _layer(xu, lw, l, tabs, *, nb, rb, alpha):
    nblk_b = rb // SUB
    mod = lw["mod"]
    pq, pg = _inproj(xu, mod, lw["w_in"], l, nblk_b=nblk_b, nb=nb)
    ya, yb, ym, yn = _mixers(pq, pg, lw, l, tabs, nb=nb, rb=rb)
    x1 = _outproj(ya, (yb, ym, yn), lw["w_out"], xu, mod, lw["ln_g0"], lw["ln_b0"], l, nblk_b=nblk_b, nb=nb, alpha=alpha)
    return _ffn(x1, mod, lw["ffn_w_up"], lw["ffn_conv"], lw["ffn_w_down"], lw["ln_g1"], lw["ln_b1"], l,
                nblk_b=nblk_b, nb=nb, alpha=alpha)


def _layer_weights(l, mod, w_in_p, gdn_conv, gdn_a_log, gdn_dt_bias, gdn_norm_g, diff_lambda, diff_norm_g,
                   mla_q_norm_g, mla_kv_norm_g, wq, wkv, na_bias, w_out_b, ln_g, ln_b, w_up_b, ffn_conv,
                   w_down_b):
    lf = diff_lambda[l].astype(F32)
    lam_init = 0.8 - 0.6 * math.exp(-0.3 * l)
    lam = jnp.exp(jnp.sum(lf[0] * lf[1])) - jnp.exp(jnp.sum(lf[2] * lf[3])) + lam_init
    return {
        "mod": mod,
        "w_in": w_in_p,
        "gdn_conv": gdn_conv[l],
        "gdn_a": _pad_lanes(jnp.exp(gdn_a_log[l].astype(F32)).reshape(1, 2 * HEADS)),
        "gdn_dtb": _pad_lanes(gdn_dt_bias[l].astype(F32).reshape(1, 2 * HEADS)),
        "gdn_norm_g": gdn_norm_g[l].reshape(1, HEAD_DIM),
        "diff_lam": jnp.stack([lam, jnp.asarray(1.0 - lam_init, F32)]),
        "diff_norm_g": diff_norm_g[l].reshape(1, HEAD_DIM),
        "mla_gq": mla_q_norm_g[l].reshape(1, MLA_Q_RANK),
        "mla_gkv": mla_kv_norm_g[l].reshape(1, MLA_KV_RANK),
        "mla_wq": wq[l],
        "mla_wkv": wkv[l],
        "na_bias": na_bias,
        "w_out": w_out_b,
        "ln_g0": ln_g[l, 0:1], "ln_b0": ln_b[l, 0:1], "ln_g1": ln_g[l, 1:2], "ln_b1": ln_b[l, 1:2],
        "ffn_w_up": w_up_b,
        "ffn_conv": ffn_conv,
        "ffn_w_down": w_down_b,
    }


def kernel(x, c, ctx, c_ctx, w_mod, b_mod, w_in, gdn_conv, gdn_a_log, gdn_dt_bias, gdn_norm_g, diff_lambda,
           diff_norm_g, mla_q_norm_g, mla_kv_norm_g, mla_w_uq, mla_w_ukv, na_rpb, w_out, ln_g, ln_b,
           ffn_w_up, ffn_conv, ffn_w_down):
    nb, n_lat, d = x.shape
    depth = w_mod.shape[0]
    assert ctx.shape[1] == SUB and n_lat % (2 * SUB) == 0 and n_lat % GRID_W == 0
    assert (nb * (SUB + n_lat)) % (2 * SUB) == 0 and nb + 1 <= MOD_ROWS
    rb = SUB + n_lat
    alpha = (2 * depth) ** 0.25

    cvec = jnp.concatenate([c, c_ctx[None, :], jnp.zeros((MOD_ROWS - nb - 1, d), c.dtype)], axis=0)
    mod = _modulation(cvec, w_mod, b_mod)
    tabs = _rope_tables(n_lat)
    w_in_p = _prep_w_in(w_in)
    wq, wkv = _prep_mla_weights(mla_w_uq, mla_w_ukv)
    na_bias = _na_bias_table(na_rpb, n_lat // GRID_W)
    w_out_b = w_out.astype(BF16)
    w_up_b = ffn_w_up.astype(BF16)
    w_down_b = ffn_w_down.astype(BF16)

    xu = jnp.concatenate([ctx, x], axis=1).reshape(nb * rb, d)
    for l in range(depth):
        lw = _layer_weights(l, mod, w_in_p, gdn_conv, gdn_a_log, gdn_dt_bias, gdn_norm_g, diff_lambda,
                            diff_norm_g, mla_q_norm_g, mla_kv_norm_g, wq, wkv, na_bias, w_out_b, ln_g, ln_b,
                            w_up_b, ffn_conv, w_down_b)
        xu = _layer(xu, lw, l, tabs, nb=nb, rb=rb, alpha=alpha)
    return xu.reshape(nb, rb, d)[:, SUB:, :]
```
